```python
import jax, jax.numpy as jnp
from jax import lax
import numpy as np

D_MODEL = 2048
BATCH = 4
SEQ = 4096
DEPTH = 1
DEC_BATCH = 16
DEC_SEQ = 32
PAST_LEN = 1024

CHUNK = 64
Q_BLOCK = 128
PLE_DIM = 256
SB_HEADS = 8
SB_HEAD_DIM = 128
SB_WIDTH = SB_HEADS * SB_HEAD_DIM
MLA_HEADS = 8
MLA_NOPE_DIM = 128
MLA_ROPE_DIM = 64
MLA_V_DIM = 128
MLA_Q_LORA = 512
MLA_KV_LORA = 512
BRANCH_WIDTH = 1024
N_BRANCH = 2
D_FF = 4 * D_MODEL
IN_WIDTH = 3 * SB_WIDTH + MLA_Q_LORA + MLA_KV_LORA + MLA_ROPE_DIM + N_BRANCH * D_MODEL
ROPE_THETA = 10000.0
EPS = 1e-6
SB_SCALE = SB_HEAD_DIM ** -0.5
MLA_SCALE = (MLA_NOPE_DIM + MLA_ROPE_DIM) ** -0.5

kernel_name = "stickbreak_mla_parallel_streaming_step"


def rmsnorm(x, g):
    xf = x.astype(jnp.float32)
    y = xf * lax.rsqrt(jnp.mean(xf * xf, axis=-1, keepdims=True) + EPS)
    return (y * g.astype(jnp.float32)).astype(x.dtype)


def rope(x, pos):
    half = x.shape[-1] // 2
    freqs = ROPE_THETA ** (-jnp.arange(half, dtype=jnp.float32) / half)
    ang = pos.astype(jnp.float32)[:, None] * freqs[None, :]
    ang = ang.reshape((1, pos.shape[0]) + (1,) * (x.ndim - 3) + (half,))
    cos, sin = jnp.cos(ang), jnp.sin(ang)
    xf = x.astype(jnp.float32)
    x1, x2 = xf[..., :half], xf[..., half:]
    return jnp.concatenate([x1 * cos - x2 * sin, x1 * sin + x2 * cos], axis=-1).astype(x.dtype)


def stick_breaking(q, k, v, q_pos, k_pos):
    z = jnp.einsum('bqhd,bkhd->bhqk', q, k).astype(jnp.float32) * SB_SCALE
    mask = k_pos[None, :] < q_pos[:, None]
    log_beta = jax.nn.log_sigmoid(z)
    log_1mb = jnp.where(mask, log_beta - z, 0.0)
    after = lax.cumsum(log_1mb, axis=3, reverse=True) - log_1mb
    w = jnp.where(mask, jnp.exp(log_beta + after), 0.0)
    return jnp.einsum('bhqk,bkhd->bqhd', w.astype(v.dtype), v)


def mla_attend(q_nope, q_rope, k_nope, k_rope, v, q_pos, k_pos):
    s = (jnp.einsum('bqhd,bkhd->bhqk', q_nope, k_nope)
         + jnp.einsum('bqhr,bkr->bhqk', q_rope, k_rope)).astype(jnp.float32) * MLA_SCALE
    mask = (k_pos // CHUNK)[None, :] <= (q_pos // CHUNK)[:, None]
    p = jax.nn.softmax(jnp.where(mask, s, -jnp.inf), axis=-1)
    return jnp.einsum('bhqk,bkhd->bqhd', p.astype(v.dtype), v)


def layer(x, ple, past, pos0, lw):
    b, n, _ = x.shape
    q_pos = pos0 + jnp.arange(n)
    h = rmsnorm(x, lw['g_mix_pre'])
    proj = h @ lw['w_in']
    sizes = (SB_WIDTH, SB_WIDTH, SB_WIDTH, MLA_Q_LORA, MLA_KV_LORA, MLA_ROPE_DIM, N_BRANCH * D_MODEL)
    points = np.cumsum(sizes)[:-1].tolist()
    sb_q, sb_k, sb_v, c_q, c_kv, k_rope, gate_logits = jnp.split(proj, points, axis=-1)
    sb_q = sb_q.reshape(b, n, SB_HEADS, SB_HEAD_DIM)
    sb_k = sb_k.reshape(b, n, SB_HEADS, SB_HEAD_DIM)
    sb_v = sb_v.reshape(b, n, SB_HEADS, SB_HEAD_DIM)
    q = (rmsnorm(c_q, lw['g_q']) @ lw['w_uq']).reshape(b, n, MLA_HEADS, MLA_NOPE_DIM + MLA_ROPE_DIM)
    q_nope, q_rope = q[..., :MLA_NOPE_DIM], rope(q[..., MLA_NOPE_DIM:], q_pos)
    c_kv = rmsnorm(c_kv, lw['g_kv'])
    k_rope = rope(k_rope, q_pos)
    new_state = (sb_k, sb_v, c_kv, k_rope)
    if past is not None:
        sb_k = jnp.concatenate([past[0], sb_k], axis=1)
        sb_v = jnp.concatenate([past[1], sb_v], axis=1)
        c_kv = jnp.concatenate([past[2], c_kv], axis=1)
        k_rope = jnp.concatenate([past[3], k_rope], axis=1)
    total = sb_k.shape[1]
    past_len = total - n
    k_pos = jnp.arange(total)
    k_nope = (c_kv @ lw['w_uk']).reshape(b, total, MLA_HEADS, MLA_NOPE_DIM)
    v_mla = (c_kv @ lw['w_uv']).reshape(b, total, MLA_HEADS, MLA_V_DIM)
    o_sb, o_mla = [], []
    for i0 in range(0, n, Q_BLOCK):
        i1 = min(i0 + Q_BLOCK, n)
        ke = past_len + i1
        qp, kp = q_pos[i0:i1], k_pos[:ke]
        o_sb.append(stick_breaking(sb_q[:, i0:i1], sb_k[:, :ke], sb_v[:, :ke], qp, kp))
        o_mla.append(mla_attend(q_nope[:, i0:i1], q_rope[:, i0:i1], k_nope[:, :ke],
                                k_rope[:, :ke], v_mla[:, :ke], qp, kp))
    o_sb = jnp.concatenate(o_sb, axis=1).reshape(b, n, BRANCH_WIDTH)
    o_mla = jnp.concatenate(o_mla, axis=1).reshape(b, n, BRANCH_WIDTH)
    branches = jnp.stack([o_sb, o_mla], axis=2)
    gates = jax.nn.sigmoid(gate_logits.reshape(b, n, N_BRANCH, D_MODEL))
    merged = jnp.sum(gates * jnp.einsum('bnkc,kcd->bnkd', branches, lw['w_branch']), axis=2)
    x = x + rmsnorm(merged @ lw['w_out'], lw['g_mix_post'])
    f = jnp.square(jax.nn.relu(rmsnorm(x, lw['g_ffn_pre']) @ lw['w_up'])) @ lw['w_down']
    x = x + rmsnorm(f, lw['g_ffn_post'])
    pg = jax.nn.sigmoid(rmsnorm(x, lw['g_ple_gate']) @ lw['w_ple_gate'])
    x = x + rmsnorm((ple @ lw['w_ple']) * pg, lw['g_ple_post'])
    return x, new_state


def setup_inputs(seed: int = 0) -> dict:
    key = jax.random.key(seed)
    ks = jax.random.split(key, 32)
    f32 = jnp.float32

    def nrm(k, shape, fan_in):
        return jax.random.normal(k, shape, f32) * (fan_in ** -0.5)

    def gain(k, dim):
        return 1.0 + 0.05 * jax.random.normal(k, (DEPTH, dim), f32)

    return {
        'x_prompt': jax.random.normal(ks[0], (BATCH, SEQ, D_MODEL), f32),
        'x_sample': jax.random.normal(ks[1], (DEC_BATCH, DEC_SEQ, D_MODEL), f32),
        'cache_sb_k': jax.random.normal(ks[2], (DEPTH, DEC_BATCH, PAST_LEN, SB_HEADS, SB_HEAD_DIM), f32),
        'cache_sb_v': jax.random.normal(ks[3], (DEPTH, DEC_BATCH, PAST_LEN, SB_HEADS, SB_HEAD_DIM), f32),
        'cache_mla_ckv': jax.random.normal(ks[4], (DEPTH, DEC_BATCH, PAST_LEN, MLA_KV_LORA), f32),
        'cache_mla_krope': jax.random.normal(ks[5], (DEPTH, DEC_BATCH, PAST_LEN, MLA_ROPE_DIM), f32),
        'p_prompt': jax.random.normal(ks[6], (DEPTH, BATCH, SEQ, PLE_DIM), f32),
        'p_sample': jax.random.normal(ks[7], (DEPTH, DEC_BATCH, DEC_SEQ, PLE_DIM), f32),
        'g_mix_pre': gain(ks[8], D_MODEL),
        'w_in': nrm(ks[9], (DEPTH, D_MODEL, IN_WIDTH), D_MODEL),
        'g_q': gain(ks[10], MLA_Q_LORA),
        'w_uq': nrm(ks[11], (DEPTH, MLA_Q_LORA, MLA_HEADS * (MLA_NOPE_DIM + MLA_ROPE_DIM)), MLA_Q_LORA),
        'g_kv': gain(ks[12], MLA_KV_LORA),
        'w_uk': nrm(ks[13], (DEPTH, MLA_KV_LORA, MLA_HEADS * MLA_NOPE_DIM), MLA_KV_LORA),
        'w_uv': nrm(ks[14], (DEPTH, MLA_KV_LORA, MLA_HEADS * MLA_V_DIM), MLA_KV_LORA),
        'w_branch': nrm(ks[15], (DEPTH, N_BRANCH, BRANCH_WIDTH, D_MODEL), BRANCH_WIDTH),
        'w_out': nrm(ks[16], (DEPTH, D_MODEL, D_MODEL), D_MODEL),
        'g_mix_post': gain(ks[17], D_MODEL),
        'g_ffn_pre': gain(ks[18], D_MODEL),
        'w_up': nrm(ks[19], (DEPTH, D_MODEL, D_FF), D_MODEL),
        'w_down': nrm(ks[20], (DEPTH, D_FF, D_MODEL), D_FF),
        'g_ffn_post': gain(ks[21], D_MODEL),
        'g_ple_gate': gain(ks[22], D_MODEL),
        'w_ple_gate': nrm(ks[23], (DEPTH, D_MODEL, D_MODEL), D_MODEL),
        'w_ple': nrm(ks[24], (DEPTH, PLE_DIM, D_MODEL), PLE_DIM),
        'g_ple_post': gain(ks[25], D_MODEL),
    }


def reference(x_prompt, x_sample, cache_sb_k, cache_sb_v, cache_mla_ckv, cache_mla_krope,
              p_prompt, p_sample, g_mix_pre, w_in, g_q, w_uq, g_kv, w_uk, w_uv, w_branch,
              w_out, g_mix_post, g_ffn_pre, w_up, w_down, g_ffn_post, g_ple_gate,
              w_ple_gate, w_ple, g_ple_post):
    past_len = cache_sb_k.shape[2]
    yp, ys = x_prompt, x_sample
    st_p, st_s = [], []
    for i in range(DEPTH):
        lw = {'g_mix_pre': g_mix_pre[i], 'w_in': w_in[i], 'g_q': g_q[i], 'w_uq': w_uq[i],
              'g_kv': g_kv[i], 'w_uk': w_uk[i], 'w_uv': w_uv[i], 'w_branch': w_branch[i],
              'w_out': w_out[i], 'g_mix_post': g_mix_post[i], 'g_ffn_pre': g_ffn_pre[i],
              'w_up': w_up[i], 'w_down': w_down[i], 'g_ffn_post': g_ffn_post[i],
              'g_ple_gate': g_ple_gate[i], 'w_ple_gate': w_ple_gate[i], 'w_ple': w_ple[i],
              'g_ple_post': g_ple_post[i]}
        yp, sp = layer(yp, p_prompt[i], None, 0, lw)
        past = (cache_sb_k[i], cache_sb_v[i], cache_mla_ckv[i], cache_mla_krope[i])
        ys, ss = layer(ys, p_sample[i], past, past_len, lw)
        st_p.append(sp)
        st_s.append(ss)
    sb_k_p = jnp.stack([s[0] for s in st_p])
    sb_v_p = jnp.stack([s[1] for s in st_p])
    ckv_p = jnp.stack([s[2] for s in st_p])
    kr_p = jnp.stack([s[3] for s in st_p])
    sb_k_s = jnp.stack([s[0] for s in st_s])
    sb_v_s = jnp.stack([s[1] for s in st_s])
    ckv_s = jnp.stack([s[2] for s in st_s])
    kr_s = jnp.stack([s[3] for s in st_s])
    return (yp, ys, sb_k_p, sb_v_p, ckv_p, kr_p, sb_k_s, sb_v_s, ckv_s, kr_s)
```

```python
import functools

import jax
import jax.numpy as jnp
from jax import lax
from jax.experimental import pallas as pl
from jax.experimental.pallas import tpu as pltpu

F32 = jnp.float32
BF16 = jnp.bfloat16

D_MODEL = 2048
CHUNK = 64
PLE_DIM = 256
HEADS = 8
HEAD_DIM = 128
SB_WIDTH = HEADS * HEAD_DIM
ROPE_DIM = 64
Q_LORA = 512
KV_LORA = 512
QK_PAD = 256
D_FF = 4 * D_MODEL
ROPE_THETA = 10000.0
EPS = 1e-6
SB_SCALE = HEAD_DIM ** -0.5
MLA_SCALE = (HEAD_DIM + ROPE_DIM) ** -0.5

LANES = 128
KEY_BLOCK = 128
SB_EXIT = -104.0
MASKED = -1e30
VMEM_LIMIT = 56 * 1024 * 1024


def _params(*sem):
    return pltpu.CompilerParams(dimension_semantics=sem, vmem_limit_bytes=VMEM_LIMIT)


def _rms(xf, g):
    ms = jnp.mean(xf * xf, axis=-1, keepdims=True)
    return xf * lax.rsqrt(ms + EPS) * g


def _dot(a, b):
    return jnp.dot(a, b, preferred_element_type=F32)


def _dot_nt(a, b):
    return lax.dot_general(a, b, (((1,), (1,)), ((), ())), preferred_element_type=F32)


def _norm_kernel(x_ref, g_ref, o_ref):
    o_ref[...] = _rms(x_ref[...], g_ref[...]).astype(BF16)


def _norm(x, g, tm):
    t, d = x.shape
    return pl.pallas_call(
        _norm_kernel,
        grid=(t // tm,),
        in_specs=[pl.BlockSpec((tm, d), lambda i: (i, 0)),
                  pl.BlockSpec((1, d), lambda i: (0, 0))],
        out_specs=pl.BlockSpec((tm, d), lambda i: (i, 0)),
        out_shape=jax.ShapeDtypeStruct((t, d), BF16),
        compiler_params=_params("parallel"),
        name="pre_norm",
    )(x, g)


def _qkv_kernel(h_ref, w_ref, q_ref, kf_ref, vf_ref, kb_ref, vb_ref, *, tn):
    j = pl.program_id(1)
    per = SB_WIDTH // tn
    acc = _dot(h_ref[...], w_ref[...])
    for seg in range(3):
        for c in range(per):
            cols = slice(c * tn, (c + 1) * tn)

            @pl.when(j == seg * per + c)
            def _(seg=seg, cols=cols):
                if seg == 0:
                    q_ref[:, cols] = (acc * SB_SCALE).astype(BF16)
                elif seg == 1:
                    kf_ref[:, cols] = acc
                    kb_ref[:, cols] = acc.astype(BF16)
                else:
                    vf_ref[:, cols] = acc
                    vb_ref[:, cols] = acc.astype(BF16)


def _qkv(h, w, tm, tn):
    t, d = h.shape
    row = lambda i, j: (i, 0)
    blk = pl.BlockSpec((tm, SB_WIDTH), row)
    return pl.pallas_call(
        functools.partial(_qkv_kernel, tn=tn),
        grid=(t // tm, 3 * SB_WIDTH // tn),
        in_specs=[pl.BlockSpec((tm, d), row),
                  pl.BlockSpec((d, tn), lambda i, j: (0, j))],
        out_specs=[blk, blk, blk, blk, blk],
        out_shape=[jax.ShapeDtypeStruct((t, SB_WIDTH), BF16),
                   jax.ShapeDtypeStruct((t, SB_WIDTH), F32),
                   jax.ShapeDtypeStruct((t, SB_WIDTH), F32),
                   jax.ShapeDtypeStruct((t, SB_WIDTH), BF16),
                   jax.ShapeDtypeStruct((t, SB_WIDTH), BF16)],
        compiler_params=_params("parallel", "arbitrary"),
        name="sb_qkv_proj",
    )(h, w)


def _latent_kernel(h_ref, w_ref, gq_ref, gkv_ref, cos_ref, sin_ref,
                   cq_ref, ckvf_ref, ckvb_ref, krf_ref, krb_ref):
    acc = _dot(h_ref[...], w_ref[...])
    cq_ref[...] = _rms(acc[:, :Q_LORA], gq_ref[...]).astype(BF16)
    ckv = _rms(acc[:, Q_LORA:Q_LORA + KV_LORA], gkv_ref[...])
    ckvf_ref[...] = ckv
    ckvb_ref[...] = ckv.astype(BF16)
    base = Q_LORA + KV_LORA
    kr = acc[:, base:base + LANES] * cos_ref[...] + acc[:, base + LANES:base + 2 * LANES] * sin_ref[...]
    krf_ref[...] = kr[:, :ROPE_DIM]
    krb_ref[...] = kr.astype(BF16)


def _latent(h, w, gq, gkv, cos2, sin2, tm):
    t, d = h.shape
    n_tab = cos2.shape[0] // tm
    row = lambda i: (i, 0)
    const = lambda i: (0, 0)
    tab = lambda i: (i % n_tab, 0)
    wn = w.shape[1]
    return pl.pallas_call(
        _latent_kernel,
        grid=(t // tm,),
        in_specs=[pl.BlockSpec((tm, d), row), pl.BlockSpec((d, wn), const),
                  pl.BlockSpec((1, Q_LORA), const), pl.BlockSpec((1, KV_LORA), const),
                  pl.BlockSpec((tm, LANES), tab), pl.BlockSpec((tm, LANES), tab)],
        out_specs=[pl.BlockSpec((tm, Q_LORA), row), pl.BlockSpec((tm, KV_LORA), row),
                   pl.BlockSpec((tm, KV_LORA), row), pl.BlockSpec((tm, ROPE_DIM), row),
                   pl.BlockSpec((tm, LANES), row)],
        out_shape=[jax.ShapeDtypeStruct((t, Q_LORA), BF16),
                   jax.ShapeDtypeStruct((t, KV_LORA), F32),
                   jax.ShapeDtypeStruct((t, KV_LORA), BF16),
                   jax.ShapeDtypeStruct((t, ROPE_DIM), F32),
                   jax.ShapeDtypeStruct((t, LANES), BF16)],
        compiler_params=_params("parallel"),
        name="mla_latent_proj",
    )(h, w, gq, gkv, cos2, sin2)


def _mla_q_kernel(cq_ref, w_ref, cos_ref, sin_ref, q_ref):
    acc = _dot(cq_ref[...], w_ref[...])
    cos, sin = cos_ref[...], sin_ref[...]
    for h in range(HEADS):
        b = h * QK_PAD
        q_ref[:, b:b + LANES] = (acc[:, b:b + LANES] * MLA_SCALE).astype(BF16)
        sw = HEADS * QK_PAD + h * LANES
        rot = acc[:, b + LANES:b + QK_PAD] * cos + acc[:, sw:sw + LANES] * sin
        q_ref[:, b + LANES:b + QK_PAD] = (rot * MLA_SCALE).astype(BF16)


def _mla_q(cq, w, cos2, sin2, tm):
    t = cq.shape[0]
    n_tab = cos2.shape[0] // tm
    row = lambda i: (i, 0)
    tab = lambda i: (i % n_tab, 0)
    return pl.pallas_call(
        _mla_q_kernel,
        grid=(t // tm,),
        in_specs=[pl.BlockSpec((tm, Q_LORA), row),
                  pl.BlockSpec(w.shape, lambda i: (0, 0)),
                  pl.BlockSpec((tm, LANES), tab), pl.BlockSpec((tm, LANES), tab)],
        out_specs=pl.BlockSpec((tm, HEADS * QK_PAD), row),
        out_shape=jax.ShapeDtypeStruct((t, HEADS * QK_PAD), BF16),
        compiler_params=_params("parallel"),
        name="mla_q_proj",
    )(cq, w, cos2, sin2)


def _mla_kv_kernel(ckv_ref, kr_ref, w_ref, k_ref, v_ref):
    acc = _dot(ckv_ref[...], w_ref[...])
    kr = kr_ref[...]
    for h in range(HEADS):
        b = h * QK_PAD
        k_ref[:, b:b + LANES] = acc[:, h * LANES:(h + 1) * LANES].astype(BF16)
        k_ref[:, b + LANES:b + QK_PAD] = kr
    v_ref[...] = acc[:, SB_WIDTH:].astype(BF16)


def _mla_kv(ckv, kr, w, tm):
    t = ckv.shape[0]
    row = lambda i: (i, 0)
    return pl.pallas_call(
        _mla_kv_kernel,
        grid=(t // tm,),
        in_specs=[pl.BlockSpec((tm, KV_LORA), row), pl.BlockSpec((tm, LANES), row),
                  pl.BlockSpec(w.shape, lambda i: (0, 0))],
        out_specs=[pl.BlockSpec((tm, HEADS * QK_PAD), row), pl.BlockSpec((tm, SB_WIDTH), row)],
        out_shape=[jax.ShapeDtypeStruct((t, HEADS * QK_PAD), BF16),
                   jax.ShapeDtypeStruct((t, SB_WIDTH), BF16)],
        compiler_params=_params("parallel"),
        name="mla_kv_proj",
    )(ckv, kr, w)


def _gate_kernel(h_ref, w_ref, o_ref):
    o_ref[...] = jax.nn.sigmoid(_dot(h_ref[...], w_ref[...])).astype(BF16)


def _gates(h, w, tm, tn):
    t, d = h.shape
    n = w.shape[1]
    return pl.pallas_call(
        _gate_kernel,
        grid=(t // tm, n // tn),
        in_specs=[pl.BlockSpec((tm, d), lambda i, j: (i, 0)),
                  pl.BlockSpec((d, tn), lambda i, j: (0, j))],
        out_specs=pl.BlockSpec((tm, tn), lambda i, j: (i, j)),
        out_shape=jax.ShapeDtypeStruct((t, n), BF16),
        compiler_params=_params("parallel", "arbitrary"),
        name="branch_gates",
    )(h, w)


def _merge_kernel(osb_ref, omla_ref, wb_ref, g0_ref, g1_ref, o_ref):
    m = (g0_ref[...].astype(F32) * _dot(osb_ref[...], wb_ref[0])
         + g1_ref[...].astype(F32) * _dot(omla_ref[...], wb_ref[1]))
    o_ref[...] = m.astype(BF16)


def _merge(osb, omla, wb, gates, tm, tn):
    t = osb.shape[0]
    nj = D_MODEL // tn
    row = lambda i, j: (i, 0)
    return pl.pallas_call(
        _merge_kernel,
        grid=(t // tm, nj),
        in_specs=[pl.BlockSpec((tm, SB_WIDTH), row), pl.BlockSpec((tm, SB_WIDTH), row),
                  pl.BlockSpec((2, SB_WIDTH, tn), lambda i, j: (0, 0, j)),
                  pl.BlockSpec((tm, tn), lambda i, j: (i, j)),
                  pl.BlockSpec((tm, tn), lambda i, j: (i, j + nj))],
        out_specs=pl.BlockSpec((tm, tn), lambda i, j: (i, j)),
        out_shape=jax.ShapeDtypeStruct((t, D_MODEL), BF16),
        compiler_params=_params("parallel", "arbitrary"),
        name="branch_merge",
    )(osb, omla, wb, gates, gates)


def _out_proj_kernel(m_ref, w_ref, x_ref, g_ref, o_ref):
    o_ref[...] = x_ref[...] + _rms(_dot(m_ref[...], w_ref[...]), g_ref[...])


def _out_proj(m, w, x, g, tm):
    t, d = x.shape
    row = lambda i: (i, 0)
    const = lambda i: (0, 0)
    return pl.pallas_call(
        _out_proj_kernel,
        grid=(t // tm,),
        in_specs=[pl.BlockSpec((tm, d), row), pl.BlockSpec((d, d), const),
                  pl.BlockSpec((tm, d), row), pl.BlockSpec((1, d), const)],
        out_specs=pl.BlockSpec((tm, d), row),
        out_shape=jax.ShapeDtypeStruct((t, d), F32),
        compiler_params=_params("parallel"),
        name="mix_out_proj",
    )(m, w, x, g)


def _ffn_kernel(x_ref, gpre_ref, wu_ref, wd_ref, gpost_ref, o_ref, h_scr, acc_scr):
    j = pl.program_id(1)

    @pl.when(j == 0)
    def _():
        h_scr[...] = _rms(x_ref[...], gpre_ref[...]).astype(BF16)
        acc_scr[...] = jnp.zeros_like(acc_scr)

    u = jnp.maximum(_dot(h_scr[...], wu_ref[...]), 0.0)
    acc_scr[...] += _dot((u * u).astype(BF16), wd_ref[...])

    @pl.when(j == pl.num_programs(1) - 1)
    def _():
        o_ref[...] = x_ref[...] + _rms(acc_scr[...], gpost_ref[...])


def _ffn(x, gpre, wu, wd, gpost, tm, tf):
    t, d = x.shape
    row = lambda i, j: (i, 0)
    const = lambda i, j: (0, 0)
    return pl.pallas_call(
        _ffn_kernel,
        grid=(t // tm, D_FF // tf),
        in_specs=[pl.BlockSpec((tm, d), row), pl.BlockSpec((1, d), const),
                  pl.BlockSpec((d, tf), lambda i, j: (0, j)),
                  pl.BlockSpec((tf, d), lambda i, j: (j, 0)),
                  pl.BlockSpec((1, d), const)],
        out_specs=pl.BlockSpec((tm, d), row),
        out_shape=jax.ShapeDtypeStruct((t, d), F32),
        scratch_shapes=[pltpu.VMEM((tm, d), BF16), pltpu.VMEM((tm, d), F32)],
        compiler_params=_params("parallel", "arbitrary"),
        name="sqrelu_ffn",
    )(x, gpre, wu, wd, gpost)


def _ple_kernel(x_ref, p_ref, gg_ref, wg_ref, wp_ref, gpost_ref, o_ref):
    x = x_ref[...]
    gate = jax.nn.sigmoid(_dot(_rms(x, gg_ref[...]).astype(BF16), wg_ref[...]))
    pe = _dot(p_ref[...].astype(BF16), wp_ref[...])
    o_ref[...] = x + _rms(pe * gate, gpost_ref[...])


def _ple(x, p, gg, wg, wp, gpost, tm):
    t, d = x.shape
    row = lambda i: (i, 0)
    const = lambda i: (0, 0)
    return pl.pallas_call(
        _ple_kernel,
        grid=(t // tm,),
        in_specs=[pl.BlockSpec((tm, d), row), pl.BlockSpec((tm, PLE_DIM), row),
                  pl.BlockSpec((1, d), const), pl.BlockSpec((d, d), const),
                  pl.BlockSpec((PLE_DIM, d), const), pl.BlockSpec((1, d), const)],
        out_specs=pl.BlockSpec((tm, d), row),
        out_shape=jax.ShapeDtypeStruct((t, d), F32),
        compiler_params=_params("parallel"),
        name="ple_embed",
    )(x, p, gg, wg, wp, gpost)


def _sb_block(q, k, v, tri, carry, acc, mask):
    z = _dot_nt(q, k)
    lb = jnp.minimum(z, 0.0) - jnp.log(1.0 + jnp.exp(-jnp.abs(z)))
    l1 = lb - z
    if mask is not None:
        l1 = jnp.where(mask, l1, 0.0)
    hi = l1.astype(BF16)
    lo = (l1 - hi.astype(F32)).astype(BF16)
    after = _dot(jnp.concatenate([hi, lo], axis=1), tri) + carry
    w = jnp.exp(lb + after)
    if mask is not None:
        w = jnp.where(mask, w, 0.0)
    acc = acc + _dot(w.astype(BF16), v)
    carry = after[:, 0:1] + l1[:, 0:1]
    return carry, acc


def _sb_kernel(q_ref, k_ref, v_ref, tri_ref, o_ref, *, bq, q_off):
    bk = KEY_BLOCK
    qi = pl.program_id(1)
    tri = tri_ref[...]
    first = q_off + (qi + 1) * bq - bk
    r = lax.broadcasted_iota(jnp.int32, (bq, bk), 0)
    c = lax.broadcasted_iota(jnp.int32, (bq, bk), 1)
    mask = c < r + (bk - bq)
    for h in range(HEADS):
        cols = slice(h * HEAD_DIM, (h + 1) * HEAD_DIM)
        q = q_ref[:, cols]
        start = pl.multiple_of(first, bk)
        carry, acc = _sb_block(q, k_ref[pl.ds(start, bk), cols], v_ref[pl.ds(start, bk), cols], tri,
                               jnp.zeros((bq, 1), F32), jnp.zeros((bq, HEAD_DIM), F32), mask)

        def cond(st):
            return jnp.logical_and(st[0] >= 0, st[1] > SB_EXIT)

        def body(st, q=q, cols=cols):
            kb, _, carry, acc = st
            s0 = pl.multiple_of(kb * bk, bk)
            carry, acc = _sb_block(q, k_ref[pl.ds(s0, bk), cols], v_ref[pl.ds(s0, bk), cols], tri,
                                   carry, acc, None)
            return kb - 1, jnp.max(carry), carry, acc

        st = lax.while_loop(cond, body, (first // bk - 1, jnp.max(carry), carry, acc))
        o_ref[:, cols] = st[3].astype(BF16)


def _sb_attention(q, k, v, tri, bq, q_off):
    b, nq, w = q.shape
    klen = k.shape[1]
    return pl.pallas_call(
        functools.partial(_sb_kernel, bq=bq, q_off=q_off),
        grid=(b, nq // bq),
        in_specs=[pl.BlockSpec((None, bq, w), lambda i, j: (i, j, 0)),
                  pl.BlockSpec((None, klen, w), lambda i, j: (i, 0, 0)),
                  pl.BlockSpec((None, klen, w), lambda i, j: (i, 0, 0)),
                  pl.BlockSpec(tri.shape, lambda i, j: (0, 0))],
        out_specs=pl.BlockSpec((None, bq, w), lambda i, j: (i, j, 0)),
        out_shape=jax.ShapeDtypeStruct((b, nq, w), BF16),
        compiler_params=_params("parallel", "arbitrary"),
        name="stickbreak_attn",
    )(q, k, v, tri)


def _mla_block(q, k, v, m, l, acc, vis):
    s = _dot_nt(q, k)
    if vis is not None:
        s = jnp.where(vis, s, MASKED)
    m_new = jnp.maximum(m, jnp.max(s, axis=-1, keepdims=True))
    alpha = jnp.exp(m - m_new)
    p = jnp.exp(s - m_new)
    l = alpha * l + jnp.sum(p, axis=-1, keepdims=True)
    acc = alpha * acc + _dot(p.astype(BF16), v)
    return m_new, l, acc


def _mla_kernel(q_ref, k_ref, v_ref, o_ref, *, bq, bk, q_off, pad, klen):
    qi = pl.program_id(2)
    q = q_ref[...]
    q_row0 = q_off + qi * bq
    qchunk = (q_row0 - pad + lax.broadcasted_iota(jnp.int32, (bq, bk), 0)) // CHUNK
    kcol = lax.broadcasted_iota(jnp.int32, (bq, bk), 1)

    def blk(kb, st, masked):
        s0 = pl.multiple_of(kb * bk, bk)
        vis = None
        if masked:
            kpos = s0 - pad + kcol
            vis = jnp.logical_and(kpos >= 0, kpos // CHUNK <= qchunk)
        return _mla_block(q, k_ref[pl.ds(s0, bk), :], v_ref[pl.ds(s0, bk), :], *st, vis)

    st = (jnp.full((bq, 1), MASKED, F32), jnp.zeros((bq, 1), F32), jnp.zeros((bq, HEAD_DIM), F32))
    last_chunk_end = ((q_row0 + bq - 1 - pad) // CHUNK + 1) * CHUNK + pad
    n_blk = (jnp.minimum(last_chunk_end, klen) + bk - 1) // bk
    first_masked = q_row0 // bk
    lo = 0
    if pad > 0:
        st = blk(0, st, True)
        lo = 1
    st = lax.fori_loop(lo, jnp.maximum(first_masked, lo), lambda kb, s: blk(kb, s, False), st)
    st = lax.fori_loop(jnp.maximum(first_masked, lo), n_blk, lambda kb, s: blk(kb, s, True), st)
    o_ref[...] = (st[2] / st[1]).astype(BF16)


def _mla_attention(q, k, v, bq, bk, q_off, pad):
    b, nq, _ = q.shape
    klen = k.shape[1]
    return pl.pallas_call(
        functools.partial(_mla_kernel, bq=bq, bk=bk, q_off=q_off, pad=pad, klen=klen),
        grid=(b, HEADS, nq // bq),
        in_specs=[pl.BlockSpec((None, bq, QK_PAD), lambda i, h, j: (i, j, h)),
                  pl.BlockSpec((None, klen, QK_PAD), lambda i, h, j: (i, 0, h)),
                  pl.BlockSpec((None, klen, HEAD_DIM), lambda i, h, j: (i, 0, h))],
        out_specs=pl.BlockSpec((None, bq, HEAD_DIM), lambda i, h, j: (i, j, h)),
        out_shape=jax.ShapeDtypeStruct((b, nq, SB_WIDTH), BF16),
        compiler_params=_params("parallel", "parallel", "arbitrary"),
        name="mla_attn",
    )(q, k, v)


def _rope_tables(pos):
    half = ROPE_DIM // 2
    freqs = ROPE_THETA ** (-jnp.arange(half, dtype=F32) / half)
    ang = pos.astype(F32)[:, None] * freqs[None, :]
    cos, sin = jnp.cos(ang), jnp.sin(ang)
    z = jnp.zeros((pos.shape[0], LANES - ROPE_DIM), F32)
    return jnp.concatenate([cos, cos, z], axis=1), jnp.concatenate([-sin, sin, z], axis=1)


def _swap_halves(w):
    half = w.shape[-1] // 2
    return jnp.concatenate([w[..., half:], w[..., :half]], axis=-1)


def _prep_weights(w_in, w_uq, w_uk, w_uv, w_branch, w_out, w_up, w_down, w_ple_gate, w_ple):
    d = w_in.shape[0]
    o = 3 * SB_WIDTH
    w_kr = w_in[:, o + Q_LORA + KV_LORA:o + Q_LORA + KV_LORA + ROPE_DIM]
    z64 = jnp.zeros((d, LANES - ROPE_DIM), w_in.dtype)
    w_lat = jnp.concatenate([w_in[:, o:o + Q_LORA + KV_LORA], w_kr, z64, _swap_halves(w_kr), z64], axis=1)
    wq3 = w_uq.reshape(Q_LORA, HEADS, HEAD_DIM + ROPE_DIM)
    rp = wq3[:, :, HEAD_DIM:]
    zq = jnp.zeros((Q_LORA, HEADS, LANES - ROPE_DIM), w_uq.dtype)
    w_q = jnp.concatenate([
        jnp.concatenate([wq3[:, :, :HEAD_DIM], rp, zq], axis=-1).reshape(Q_LORA, HEADS * QK_PAD),
        jnp.concatenate([_swap_halves(rp), zq], axis=-1).reshape(Q_LORA, HEADS * LANES)], axis=1)
    return dict(
        qkv=w_in[:, :o].astype(BF16),
        lat=w_lat.astype(BF16),
        gate=w_in[:, o + Q_LORA + KV_LORA + ROPE_DIM:].astype(BF16),
        q=w_q.astype(BF16),
        kv=jnp.concatenate([w_uk, w_uv], axis=1).astype(BF16),
        branch=w_branch.astype(BF16),
        out=w_out.astype(BF16),
        up=w_up.astype(BF16),
        down=w_down.astype(BF16),
        ple_gate=w_ple_gate.astype(BF16),
        ple=w_ple.astype(BF16),
    )


def _tile(t, want):
    return want if t % want == 0 else t


def _layer(x, ple, past, w, g, tri):
    b, n, d = x.shape
    t = b * n
    xf = x.reshape(t, d)
    tm = _tile(t, 1024)
    tm_s = _tile(t, 512)
    past_len = 0 if past is None else past[0].shape[1]
    if n >= tm:
        pos = jnp.arange(n) + past_len
    else:
        pos = jnp.tile(jnp.arange(n) + past_len, tm // n)
    cos2, sin2 = _rope_tables(pos)

    h = _norm(xf, g['mix_pre'], tm)
    q_sb, k_f, v_f, k_b, v_b = _qkv(h, w['qkv'], tm, 512)
    cq, ckv_f, ckv_b, kr_f, kr_b = _latent(h, w['lat'], g['q'], g['kv'], cos2, sin2, tm)
    q_mla = _mla_q(cq, w['q'], cos2, sin2, tm_s)
    gates = _gates(h, w['gate'], tm, 1024)

    if past is None:
        k_all, v_all = k_b.reshape(b, n, -1), v_b.reshape(b, n, -1)
        ckv_all, kr_all = ckv_b, kr_b
        pad, klen = 0, n
        sb_bq, mla_bq, mla_bk = KEY_BLOCK, 512, 512
    else:
        klen = -(-(past_len + n) // KEY_BLOCK) * KEY_BLOCK
        pad = klen - past_len - n

        def cat(old, new):
            old = old.reshape(b, past_len, -1).astype(BF16)
            new = new.reshape(b, n, -1)
            if old.shape[-1] < new.shape[-1]:
                old = jnp.pad(old, ((0, 0), (0, 0), (0, new.shape[-1] - old.shape[-1])))
            return jnp.concatenate([jnp.zeros((b, pad, new.shape[-1]), BF16), old, new], axis=1)

        k_all, v_all = cat(past[0], k_b), cat(past[1], v_b)
        ckv_all = cat(past[2], ckv_b).reshape(b * klen, -1)
        kr_all = cat(past[3], kr_b).reshape(b * klen, -1)
        sb_bq, mla_bq, mla_bk = n, n, KEY_BLOCK
    q_off = klen - n if past is not None else 0

    k_mla, v_mla = _mla_kv(ckv_all, kr_all, w['kv'], _tile(b * klen, 1024))
    o_sb = _sb_attention(q_sb.reshape(b, n, -1), k_all, v_all, tri, sb_bq, q_off)
    o_mla = _mla_attention(q_mla.reshape(b, n, -1), k_mla.reshape(b, klen, -1),
                           v_mla.reshape(b, klen, -1), mla_bq, mla_bk, q_off, pad)

    merged = _merge(o_sb.reshape(t, -1), o_mla.reshape(t, -1), w['branch'], gates, tm, 1024)
    x1 = _out_proj(merged, w['out'], xf, g['mix_post'], tm_s)
    x2 = _ffn(x1, g['ffn_pre'], w['up'], w['down'], g['ffn_post'], tm_s, 1024)
    x3 = _ple(x2, ple.reshape(t, -1), g['ple_gate'], w['ple_gate'], w['ple'], g['ple_post'], tm_s)
    state = (k_f.reshape(1, b, n, HEADS, HEAD_DIM), v_f.reshape(1, b, n, HEADS, HEAD_DIM),
             ckv_f.reshape(1, b, n, KV_LORA), kr_f.reshape(1, b, n, ROPE_DIM))
    return x3.reshape(b, n, d), state


def kernel(x_prompt, x_sample, cache_sb_k, cache_sb_v, cache_mla_ckv, cache_mla_krope, p_prompt, p_sample, g_mix_pre, w_in, g_q, w_uq, g_kv, w_uk, w_uv, w_branch, w_out, g_mix_post, g_ffn_pre, w_up, w_down, g_ffn_post, g_ple_gate, w_ple_gate, w_ple, g_ple_post):
    assert w_in.shape[0] == 1, "single layer"
    w = _prep_weights(w_in[0], w_uq[0], w_uk[0], w_uv[0], w_branch[0], w_out[0], w_up[0], w_down[0],
                      w_ple_gate[0], w_ple[0])
    g = dict(mix_pre=g_mix_pre, q=g_q, kv=g_kv, mix_post=g_mix_post, ffn_pre=g_ffn_pre,
             ffn_post=g_ffn_post, ple_gate=g_ple_gate, ple_post=g_ple_post)
    idx = jnp.arange(KEY_BLOCK)
    tri = (idx[:, None] > idx[None, :]).astype(BF16)
    tri = jnp.concatenate([tri, tri], axis=0)

    yp, sp = _layer(x_prompt, p_prompt[0], None, w, g, tri)
    past = (cache_sb_k[0], cache_sb_v[0], cache_mla_ckv[0], cache_mla_krope[0])
    ys, ss = _layer(x_sample, p_sample[0], past, w, g, tri)
    return (yp, ys) + sp + ss
```

```python
import functools

import jax
import jax.numpy as jnp
from jax import lax
from jax.experimental import pallas as pl
from jax.experimental.pallas import tpu as pltpu

F32 = jnp.float32
BF16 = jnp.bfloat16

D_MODEL = 2048
CHUNK = 64
PLE_DIM = 256
HEADS = 8
HEAD_DIM = 128
SB_WIDTH = HEADS * HEAD_DIM
ROPE_DIM = 64
Q_LORA = 512
KV_LORA = 512
QK_PAD = 256
D_FF = 4 * D_MODEL
ROPE_THETA = 10000.0
EPS = 1e-6
SB_SCALE = HEAD_DIM ** -0.5
MLA_SCALE = (HEAD_DIM + ROPE_DIM) ** -0.5

LANES = 128
KEY_BLOCK = 128
SB_EXIT = -104.0
MASKED = -1e30
VMEM_LIMIT = 56 * 1024 * 1024


def _params(*sem):
    return pltpu.CompilerParams(dimension_semantics=sem, vmem_limit_bytes=VMEM_LIMIT)


def _rms(xf, g):
    ms = jnp.mean(xf * xf, axis=-1, keepdims=True)
    return xf * lax.rsqrt(ms + EPS) * g


def _dot(a, b):
    return jnp.dot(a, b, preferred_element_type=F32)


def _dot_nt(a, b):
    return lax.dot_general(a, b, (((1,), (1,)), ((), ())), preferred_element_type=F32)


def _norm_kernel(x_ref, g_ref, o_ref):
    o_ref[...] = _rms(x_ref[...], g_ref[...]).astype(BF16)


def _norm(x, g, tm):
    t, d = x.shape
    return pl.pallas_call(
        _norm_kernel,
        grid=(t // tm,),
        in_specs=[pl.BlockSpec((tm, d), lambda i: (i, 0)),
                  pl.BlockSpec((1, d), lambda i: (0, 0))],
        out_specs=pl.BlockSpec((tm, d), lambda i: (i, 0)),
        out_shape=jax.ShapeDtypeStruct((t, d), BF16),
        compiler_params=_params("parallel"),
        name="pre_norm",
    )(x, g)


def _qkv_kernel(h_ref, w_ref, qkv_ref, kvf_ref, *, per):
    acc = _dot(h_ref[...], w_ref[...])
    kvf_ref[...] = acc
    scale = jnp.where(pl.program_id(1) < per, SB_SCALE, 1.0)
    qkv_ref[...] = (acc * scale).astype(BF16)


def _qkv(h, w, tm, tn):
    t, d = h.shape
    per = SB_WIDTH // tn
    return pl.pallas_call(
        functools.partial(_qkv_kernel, per=per),
        grid=(t // tm, 3 * per),
        in_specs=[pl.BlockSpec((tm, d), lambda i, j: (i, 0)),
                  pl.BlockSpec((d, tn), lambda i, j: (0, j))],
        out_specs=[pl.BlockSpec((tm, tn), lambda i, j: (i, j)),
                   pl.BlockSpec((tm, tn), lambda i, j: (i, jnp.maximum(j - per, 0)))],
        out_shape=[jax.ShapeDtypeStruct((t, 3 * SB_WIDTH), BF16),
                   jax.ShapeDtypeStruct((t, 2 * SB_WIDTH), F32)],
        compiler_params=_params("parallel", "arbitrary"),
        name="sb_qkv_proj",
    )(h, w)


def _latent_kernel(h_ref, w_ref, gq_ref, gkv_ref, cos_ref, sin_ref,
                   cq_ref, ckvf_ref, ckvb_ref, krf_ref, krb_ref):
    acc = _dot(h_ref[...], w_ref[...])
    cq_ref[...] = _rms(acc[:, :Q_LORA], gq_ref[...]).astype(BF16)
    ckv = _rms(acc[:, Q_LORA:Q_LORA + KV_LORA], gkv_ref[...])
    ckvf_ref[...] = ckv
    ckvb_ref[...] = ckv.astype(BF16)
    base = Q_LORA + KV_LORA
    kr = acc[:, base:base + LANES] * cos_ref[...] + acc[:, base + LANES:base + 2 * LANES] * sin_ref[...]
    krf_ref[...] = kr[:, :ROPE_DIM]
    krb_ref[...] = kr[:, :ROPE_DIM].astype(BF16)


def _latent(h, w, gq, gkv, cos2, sin2, tm):
    t, d = h.shape
    n_tab = cos2.shape[0] // tm
    row = lambda i: (i, 0)
    const = lambda i: (0, 0)
    tab = lambda i: (i % n_tab, 0)
    wn = w.shape[1]
    return pl.pallas_call(
        _latent_kernel,
        grid=(t // tm,),
        in_specs=[pl.BlockSpec((tm, d), row), pl.BlockSpec((d, wn), const),
                  pl.BlockSpec((1, Q_LORA), const), pl.BlockSpec((1, KV_LORA), const),
                  pl.BlockSpec((tm, LANES), tab), pl.BlockSpec((tm, LANES), tab)],
        out_specs=[pl.BlockSpec((tm, Q_LORA), row), pl.BlockSpec((tm, KV_LORA), row),
                   pl.BlockSpec((tm, KV_LORA), row), pl.BlockSpec((tm, ROPE_DIM), row),
                   pl.BlockSpec((tm, ROPE_DIM), row)],
        out_shape=[jax.ShapeDtypeStruct((t, Q_LORA), BF16),
                   jax.ShapeDtypeStruct((t, KV_LORA), F32),
                   jax.ShapeDtypeStruct((t, KV_LORA), BF16),
                   jax.ShapeDtypeStruct((t, ROPE_DIM), F32),
                   jax.ShapeDtypeStruct((t, ROPE_DIM), BF16)],
        compiler_params=_params("parallel"),
        name="mla_latent_proj",
    )(h, w, gq, gkv, cos2, sin2)


def _mla_q_kernel(cq_ref, w_ref, cos_ref, sin_ref, q_ref):
    acc = _dot(cq_ref[...], w_ref[...])
    cos, sin = cos_ref[...], sin_ref[...]
    for h in range(HEADS):
        b = h * QK_PAD
        q_ref[:, b:b + LANES] = (acc[:, b:b + LANES] * MLA_SCALE).astype(BF16)
        sw = HEADS * QK_PAD + h * LANES
        rot = acc[:, b + LANES:b + QK_PAD] * cos + acc[:, sw:sw + LANES] * sin
        q_ref[:, b + LANES:b + QK_PAD] = (rot * MLA_SCALE).astype(BF16)


def _mla_q(cq, w, cos2, sin2, tm):
    t = cq.shape[0]
    n_tab = cos2.shape[0] // tm
    row = lambda i: (i, 0)
    tab = lambda i: (i % n_tab, 0)
    return pl.pallas_call(
        _mla_q_kernel,
        grid=(t // tm,),
        in_specs=[pl.BlockSpec((tm, Q_LORA), row),
                  pl.BlockSpec(w.shape, lambda i: (0, 0)),
                  pl.BlockSpec((tm, LANES), tab), pl.BlockSpec((tm, LANES), tab)],
        out_specs=pl.BlockSpec((tm, HEADS * QK_PAD), row),
        out_shape=jax.ShapeDtypeStruct((t, HEADS * QK_PAD), BF16),
        compiler_params=_params("parallel"),
        name="mla_q_proj",
    )(cq, w, cos2, sin2)


def _mla_kv_kernel(ckv_ref, kr_ref, w_ref, k_ref, v_ref):
    acc = _dot(ckv_ref[...].astype(BF16), w_ref[...])
    kr = kr_ref[...].astype(BF16)
    zero = jnp.zeros((kr.shape[0], QK_PAD - LANES - ROPE_DIM), BF16)
    for h in range(HEADS):
        b = h * QK_PAD
        k_ref[:, b:b + LANES] = acc[:, h * LANES:(h + 1) * LANES].astype(BF16)
        k_ref[:, b + LANES:b + LANES + ROPE_DIM] = kr
        k_ref[:, b + LANES + ROPE_DIM:b + QK_PAD] = zero
    v_ref[...] = acc[:, SB_WIDTH:].astype(BF16)


def _mla_kv(ckv, kr, w, tm):
    t = ckv.shape[0]
    row = lambda i: (i, 0)
    return pl.pallas_call(
        _mla_kv_kernel,
        grid=(t // tm,),
        in_specs=[pl.BlockSpec((tm, KV_LORA), row), pl.BlockSpec((tm, ROPE_DIM), row),
                  pl.BlockSpec(w.shape, lambda i: (0, 0))],
        out_specs=[pl.BlockSpec((tm, HEADS * QK_PAD), row), pl.BlockSpec((tm, SB_WIDTH), row)],
        out_shape=[jax.ShapeDtypeStruct((t, HEADS * QK_PAD), BF16),
                   jax.ShapeDtypeStruct((t, SB_WIDTH), BF16)],
        compiler_params=_params("parallel"),
        name="mla_kv_proj",
    )(ckv, kr, w)


def _gate_kernel(h_ref, w_ref, o_ref):
    o_ref[...] = jax.nn.sigmoid(_dot(h_ref[...], w_ref[...])).astype(BF16)


def _gates(h, w, tm, tn):
    t, d = h.shape
    n = w.shape[1]
    return pl.pallas_call(
        _gate_kernel,
        grid=(t // tm, n // tn),
        in_specs=[pl.BlockSpec((tm, d), lambda i, j: (i, 0)),
                  pl.BlockSpec((d, tn), lambda i, j: (0, j))],
        out_specs=pl.BlockSpec((tm, tn), lambda i, j: (i, j)),
        out_shape=jax.ShapeDtypeStruct((t, n), BF16),
        compiler_params=_params("parallel", "arbitrary"),
        name="branch_gates",
    )(h, w)


def _merge_kernel(osb_ref, omla_ref, wb_ref, g0_ref, g1_ref, o_ref):
    m = (g0_ref[...].astype(F32) * _dot(osb_ref[...], wb_ref[0])
         + g1_ref[...].astype(F32) * _dot(omla_ref[...], wb_ref[1]))
    o_ref[...] = m.astype(BF16)


def _merge(osb, omla, wb, gates, tm, tn):
    t = osb.shape[0]
    nj = D_MODEL // tn
    row = lambda i, j: (i, 0)
    return pl.pallas_call(
        _merge_kernel,
        grid=(t // tm, nj),
        in_specs=[pl.BlockSpec((tm, SB_WIDTH), row), pl.BlockSpec((tm, SB_WIDTH), row),
                  pl.BlockSpec((2, SB_WIDTH, tn), lambda i, j: (0, 0, j)),
                  pl.BlockSpec((tm, tn), lambda i, j: (i, j)),
                  pl.BlockSpec((tm, tn), lambda i, j: (i, j + nj))],
        out_specs=pl.BlockSpec((tm, tn), lambda i, j: (i, j)),
        out_shape=jax.ShapeDtypeStruct((t, D_MODEL), BF16),
        compiler_params=_params("parallel", "arbitrary"),
        name="branch_merge",
    )(osb, omla, wb, gates, gates)


def _out_proj_kernel(m_ref, w_ref, x_ref, g_ref, o_ref):
    o_ref[...] = x_ref[...] + _rms(_dot(m_ref[...], w_ref[...]), g_ref[...])


def _out_proj(m, w, x, g, tm):
    t, d = x.shape
    row = lambda i: (i, 0)
    const = lambda i: (0, 0)
    return pl.pallas_call(
        _out_proj_kernel,
        grid=(t // tm,),
        in_specs=[pl.BlockSpec((tm, d), row), pl.BlockSpec((d, d), const),
                  pl.BlockSpec((tm, d), row), pl.BlockSpec((1, d), const)],
        out_specs=pl.BlockSpec((tm, d), row),
        out_shape=jax.ShapeDtypeStruct((t, d), F32),
        compiler_params=_params("parallel"),
        name="mix_out_proj",
    )(m, w, x, g)


def _ffn_kernel(x_ref, gpre_ref, wu_ref, wd_ref, gpost_ref, o_ref, h_scr, acc_scr):
    j = pl.program_id(1)

    @pl.when(j == 0)
    def _():
        h_scr[...] = _rms(x_ref[...], gpre_ref[...]).astype(BF16)
        acc_scr[...] = jnp.zeros_like(acc_scr)

    u = jnp.maximum(_dot(h_scr[...], wu_ref[...]), 0.0)
    acc_scr[...] += _dot((u * u).astype(BF16), wd_ref[...])

    @pl.when(j == pl.num_programs(1) - 1)
    def _():
        o_ref[...] = x_ref[...] + _rms(acc_scr[...], gpost_ref[...])


def _ffn(x, gpre, wu, wd, gpost, tm, tf):
    t, d = x.shape
    row = lambda i, j: (i, 0)
    const = lambda i, j: (0, 0)
    return pl.pallas_call(
        _ffn_kernel,
        grid=(t // tm, D_FF // tf),
        in_specs=[pl.BlockSpec((tm, d), row), pl.BlockSpec((1, d), const),
                  pl.BlockSpec((d, tf), lambda i, j: (0, j)),
                  pl.BlockSpec((tf, d), lambda i, j: (j, 0)),
                  pl.BlockSpec((1, d), const)],
        out_specs=pl.BlockSpec((tm, d), row),
        out_shape=jax.ShapeDtypeStruct((t, d), F32),
        scratch_shapes=[pltpu.VMEM((tm, d), BF16), pltpu.VMEM((tm, d), F32)],
        compiler_params=_params("parallel", "arbitrary"),
        name="sqrelu_ffn",
    )(x, gpre, wu, wd, gpost)


def _ple_kernel(x_ref, p_ref, gg_ref, wg_ref, wp_ref, gpost_ref, o_ref):
    x = x_ref[...]
    gate = jax.nn.sigmoid(_dot(_rms(x, gg_ref[...]).astype(BF16), wg_ref[...]))
    pe = _dot(p_ref[...].astype(BF16), wp_ref[...])
    o_ref[...] = x + _rms(pe * gate, gpost_ref[...])


def _ple(x, p, gg, wg, wp, gpost, tm):
    t, d = x.shape
    row = lambda i: (i, 0)
    const = lambda i: (0, 0)
    return pl.pallas_call(
        _ple_kernel,
        grid=(t // tm,),
        in_specs=[pl.BlockSpec((tm, d), row), pl.BlockSpec((tm, PLE_DIM), row),
                  pl.BlockSpec((1, d), const), pl.BlockSpec((d, d), const),
                  pl.BlockSpec((PLE_DIM, d), const), pl.BlockSpec((1, d), const)],
        out_specs=pl.BlockSpec((tm, d), row),
        out_shape=jax.ShapeDtypeStruct((t, d), F32),
        compiler_params=_params("parallel"),
        name="ple_embed",
    )(x, p, gg, wg, wp, gpost)


def _sb_step(qs, ks, vs, tri, carry, mask):
    rows = qs[0].shape[0]
    z = jnp.concatenate([_dot_nt(q, k) for q, k in zip(qs, ks)], axis=0)
    sp = jnp.maximum(z, 0.0) + jnp.log(1.0 + jnp.exp(-jnp.abs(z)))
    spm = sp if mask is None else jnp.where(mask, sp, 0.0)
    hi = spm.astype(BF16)
    lo = (spm - hi.astype(F32)).astype(BF16)
    cs = _dot(jnp.concatenate([hi, lo], axis=1), tri)
    after = cs[:, :KEY_BLOCK] if carry is None else cs[:, :KEY_BLOCK] + carry
    w = jnp.exp((z - sp) + after)
    if mask is not None:
        w = jnp.where(mask, w, 0.0)
    w = w.astype(BF16)
    pvs = [_dot(w[i * rows:(i + 1) * rows], v) for i, v in enumerate(vs)]
    carry = cs[:, KEY_BLOCK:] if carry is None else carry + cs[:, KEY_BLOCK:]
    return carry, pvs


def _sb_walk(first_blocks, later_blocks, tri_ref, carry_scr, acc_scr, o_ref):
    def run(chains, carry, mask, first):
        carry, pvs = _sb_step([c[2] for c in chains], [c[3] for c in chains], [c[4] for c in chains],
                              tri_ref[...], carry, mask)
        carry_scr[...] = carry
        for (rows, cols, _, _, _), pv in zip(chains, pvs):
            if first:
                acc_scr[rows, cols] = pv
            else:
                acc_scr[rows, cols] += pv
        return jnp.max(carry)

    def step0():
        chains, mask = first_blocks()
        return run(chains, None, mask, True)

    def step(j):
        chains, valid = later_blocks(j)
        rows = chains[0][2].shape[0]
        carry = jnp.concatenate(
            [jnp.where(ok, carry_scr[i * rows:(i + 1) * rows, :], MASKED) for i, ok in enumerate(valid)],
            axis=0)
        return run(chains, carry, None, False)

    lax.while_loop(lambda st: st[1] > SB_EXIT,
                   lambda st: (st[0] + 1, step(st[0])),
                   (jnp.int32(1), step0()))
    o_ref[...] = acc_scr[...].astype(BF16)


def _head_cols(h):
    return slice(h * HEAD_DIM, (h + 1) * HEAD_DIM)


def _sb_prompt_kernel(q_ref, k_ref, v_ref, tri_ref, o_ref, carry_scr, acc_scr, *, rsub):
    bk = KEY_BLOCK
    base = pl.program_id(1) * rsub
    n_chain = rsub * HEADS
    r = lax.broadcasted_iota(jnp.int32, (n_chain * bk, bk), 0)
    c = lax.broadcasted_iota(jnp.int32, (n_chain * bk, bk), 1)
    causal = c < jnp.bitwise_and(r, bk - 1)

    def blocks(j):
        chains, valid = [], []
        for s in range(rsub):
            kb = base + s - j
            start = pl.multiple_of(jnp.maximum(kb, 0) * bk, bk)
            rows = slice(s * bk, (s + 1) * bk)
            for h in range(HEADS):
                cols = _head_cols(h)
                chains.append((rows, cols, q_ref[rows, cols],
                               k_ref[pl.ds(start, bk), cols], v_ref[pl.ds(start, bk), cols]))
                valid.append(kb >= 0)
        return chains, valid

    _sb_walk(lambda: (blocks(0)[0], causal), blocks, tri_ref, carry_scr, acc_scr, o_ref)


def _sb_prompt(qkv, tri, rsub):
    b, n, _ = qkv.shape
    bq = rsub * KEY_BLOCK
    full = (None, n, SB_WIDTH)
    return pl.pallas_call(
        functools.partial(_sb_prompt_kernel, rsub=rsub),
        grid=(b, n // bq),
        in_specs=[pl.BlockSpec((None, bq, SB_WIDTH), lambda i, j: (i, j, 0)),
                  pl.BlockSpec(full, lambda i, j: (i, 0, 1)),
                  pl.BlockSpec(full, lambda i, j: (i, 0, 2)),
                  pl.BlockSpec(tri.shape, lambda i, j: (0, 0))],
        out_specs=pl.BlockSpec((None, bq, SB_WIDTH), lambda i, j: (i, j, 0)),
        out_shape=jax.ShapeDtypeStruct((b, n, SB_WIDTH), BF16),
        scratch_shapes=[pltpu.VMEM((rsub * HEADS * KEY_BLOCK, KEY_BLOCK), F32),
                        pltpu.VMEM((bq, SB_WIDTH), F32)],
        compiler_params=_params("parallel", "arbitrary"),
        name="stickbreak_attn",
    )(qkv, qkv, qkv, tri)


def _sb_decode_kernel(q_ref, kn_ref, vn_ref, kp_ref, vp_ref, tri_ref, o_ref, carry_scr, acc_scr,
                      *, n, past):
    bk = KEY_BLOCK
    nb = past // bk
    assert n & (n - 1) == 0
    r = lax.broadcasted_iota(jnp.int32, (HEADS * n, bk), 0)
    c = lax.broadcasted_iota(jnp.int32, (HEADS * n, bk), 1)
    causal = c < jnp.bitwise_and(r, n - 1)
    zpad = jnp.zeros((bk - n, HEAD_DIM), BF16)
    rows = slice(0, n)

    def first():
        return [(rows, _head_cols(h), q_ref[:, _head_cols(h)],
                 jnp.concatenate([kn_ref[:, _head_cols(h)], zpad], axis=0),
                 jnp.concatenate([vn_ref[:, _head_cols(h)], zpad], axis=0))
                for h in range(HEADS)], causal

    def later(j):
        kb = nb - j
        start = jnp.maximum(kb, 0) * (bk * HEADS)
        chains = []
        for h in range(HEADS):
            sl = pl.ds(start + h, bk, stride=HEADS)
            chains.append((rows, _head_cols(h), q_ref[:, _head_cols(h)],
                           kp_ref[sl, :].astype(BF16), vp_ref[sl, :].astype(BF16)))
        return chains, [kb >= 0] * HEADS

    _sb_walk(first, later, tri_ref, carry_scr, acc_scr, o_ref)


def _sb_decode(qkv, k_past, v_past, tri):
    b, n, _ = qkv.shape
    past = k_past.shape[1]
    assert past % KEY_BLOCK == 0 and n <= KEY_BLOCK and n % 16 == 0
    new = (None, n, SB_WIDTH)
    old = (None, past * HEADS, HEAD_DIM)
    return pl.pallas_call(
        functools.partial(_sb_decode_kernel, n=n, past=past),
        grid=(b,),
        in_specs=[pl.BlockSpec(new, lambda i: (i, 0, 0)),
                  pl.BlockSpec(new, lambda i: (i, 0, 1)),
                  pl.BlockSpec(new, lambda i: (i, 0, 2)),
                  pl.BlockSpec(old, lambda i: (i, 0, 0)),
                  pl.BlockSpec(old, lambda i: (i, 0, 0)),
                  pl.BlockSpec(tri.shape, lambda i: (0, 0))],
        out_specs=pl.BlockSpec(new, lambda i: (i, 0, 0)),
        out_shape=jax.ShapeDtypeStruct((b, n, SB_WIDTH), BF16),
        scratch_shapes=[pltpu.VMEM((HEADS * n, KEY_BLOCK), F32),
                        pltpu.VMEM((n, SB_WIDTH), F32)],
        compiler_params=_params("parallel"),
        name="stickbreak_attn_decode",
    )(qkv, qkv, qkv, k_past.reshape(b, past * HEADS, HEAD_DIM),
      v_past.reshape(b, past * HEADS, HEAD_DIM), tri)


CHUNK_SHIFT = CHUNK.bit_length() - 1
assert 1 << CHUNK_SHIFT == CHUNK


def _chunk(pos):
    return lax.shift_right_arithmetic(pos, CHUNK_SHIFT)


def _mla_block(q, k, v, m, l, acc, vis):
    s = _dot_nt(q, k)
    if vis is not None:
        s = jnp.where(vis, s, MASKED)
    m_new = jnp.maximum(m, jnp.max(s, axis=-1, keepdims=True))
    alpha = jnp.exp(m - m_new)
    p = jnp.exp(s - m_new)
    l = alpha * l + jnp.sum(p, axis=-1, keepdims=True)
    acc = alpha * acc + _dot(p.astype(BF16), v)
    return m_new, l, acc


def _mla_prompt_kernel(q_ref, k_ref, v_ref, o_ref, *, bq, bk, hps):
    qi = pl.program_id(2)
    qchunk = _chunk(qi * bq + lax.broadcasted_iota(jnp.int32, (bq, bk), 0))
    kcol = lax.broadcasted_iota(jnp.int32, (bq, bk), 1)

    def blk(kb, st, masked):
        s0 = pl.multiple_of(kb * bk, bk)
        vis = (_chunk(s0 + kcol) <= qchunk) if masked else None
        return tuple(
            _mla_block(q_ref[:, h * QK_PAD:(h + 1) * QK_PAD],
                       k_ref[pl.ds(s0, bk), h * QK_PAD:(h + 1) * QK_PAD],
                       v_ref[pl.ds(s0, bk), _head_cols(h)], *st[h], vis)
            for h in range(hps))

    init = (jnp.full((bq, 1), MASKED, F32), jnp.zeros((bq, 1), F32), jnp.zeros((bq, HEAD_DIM), F32))
    n_open = (qi * bq) // bk
    n_blk = ((qi + 1) * bq + bk - 1) // bk
    st = lax.fori_loop(0, n_open, lambda kb, s: blk(kb, s, False), (init,) * hps)
    st = lax.fori_loop(n_open, n_blk, lambda kb, s: blk(kb, s, True), st)
    for h in range(hps):
        o_ref[:, _head_cols(h)] = (st[h][2] / st[h][1]).astype(BF16)


def _mla_prompt(q, k, v, bq, bk, hps):
    b, n, _ = q.shape
    assert bq % CHUNK == 0 and n % bq == 0 and n % bk == 0 and HEADS % hps == 0
    return pl.pallas_call(
        functools.partial(_mla_prompt_kernel, bq=bq, bk=bk, hps=hps),
        grid=(b, HEADS // hps, n // bq),
        in_specs=[pl.BlockSpec((None, bq, hps * QK_PAD), lambda i, h, j: (i, j, h)),
                  pl.BlockSpec((None, n, hps * QK_PAD), lambda i, h, j: (i, 0, h)),
                  pl.BlockSpec((None, n, hps * HEAD_DIM), lambda i, h, j: (i, 0, h))],
        out_specs=pl.BlockSpec((None, bq, hps * HEAD_DIM), lambda i, h, j: (i, j, h)),
        out_shape=jax.ShapeDtypeStruct((b, n, SB_WIDTH), BF16),
        compiler_params=_params("parallel", "parallel", "arbitrary"),
        name="mla_attn",
    )(q, k, v)


def _mla_decode_kernel(q_ref, kp_ref, vp_ref, kn_ref, vn_ref, o_ref, *, n, past):
    r = lax.broadcasted_iota(jnp.int32, (n, LANES), 0)
    c = lax.broadcasted_iota(jnp.int32, (n, LANES), 1)
    vis = jnp.logical_and(c < n, _chunk(past + c) <= _chunk(past + r))
    kpad = jnp.zeros((LANES - n, QK_PAD), BF16)
    vpad = jnp.zeros((LANES - n, HEAD_DIM), BF16)
    for h in range(HEADS):
        qk = slice(h * QK_PAD, (h + 1) * QK_PAD)
        q = q_ref[:, qk]
        s_old = _dot_nt(q, kp_ref[:, qk])
        s_new = jnp.where(vis, _dot_nt(q, jnp.concatenate([kn_ref[:, qk], kpad], axis=0)), MASKED)
        m = jnp.maximum(jnp.max(s_old, axis=-1, keepdims=True), jnp.max(s_new, axis=-1, keepdims=True))
        p_old = jnp.exp(s_old - m)
        p_new = jnp.exp(s_new - m)
        l = jnp.sum(p_old, axis=-1, keepdims=True) + jnp.sum(p_new, axis=-1, keepdims=True)
        acc = (_dot(p_old.astype(BF16), vp_ref[:, _head_cols(h)])
               + _dot(p_new.astype(BF16), jnp.concatenate([vn_ref[:, _head_cols(h)], vpad], axis=0)))
        o_ref[:, _head_cols(h)] = (acc / l).astype(BF16)


def _mla_decode(q, k_past, v_past, k_new, v_new):
    b, n, _ = q.shape
    past = k_past.shape[1]
    assert n <= LANES and n % 16 == 0
    spec = lambda rows, width: pl.BlockSpec((None, rows, width), lambda i: (i, 0, 0))
    return pl.pallas_call(
        functools.partial(_mla_decode_kernel, n=n, past=past),
        grid=(b,),
        in_specs=[spec(n, HEADS * QK_PAD), spec(past, HEADS * QK_PAD), spec(past, SB_WIDTH),
                  spec(n, HEADS * QK_PAD), spec(n, SB_WIDTH)],
        out_specs=spec(n, SB_WIDTH),
        out_shape=jax.ShapeDtypeStruct((b, n, SB_WIDTH), BF16),
        compiler_params=_params("parallel"),
        name="mla_attn_decode",
    )(q, k_past, v_past, k_new, v_new)


def _rope_tables(pos):
    half = ROPE_DIM // 2
    freqs = ROPE_THETA ** (-jnp.arange(half, dtype=F32) / half)
    ang = pos.astype(F32)[:, None] * freqs[None, :]
    cos, sin = jnp.cos(ang), jnp.sin(ang)
    z = jnp.zeros((pos.shape[0], LANES - ROPE_DIM), F32)
    return jnp.concatenate([cos, cos, z], axis=1), jnp.concatenate([-sin, sin, z], axis=1)


def _swap_halves(w):
    half = w.shape[-1] // 2
    return jnp.concatenate([w[..., half:], w[..., :half]], axis=-1)


def _prep_weights(w_in, w_uq, w_uk, w_uv, w_branch, w_out, w_up, w_down, w_ple_gate, w_ple):
    d = w_in.shape[0]
    o = 3 * SB_WIDTH
    w_kr = w_in[:, o + Q_LORA + KV_LORA:o + Q_LORA + KV_LORA + ROPE_DIM]
    z64 = jnp.zeros((d, LANES - ROPE_DIM), w_in.dtype)
    w_lat = jnp.concatenate([w_in[:, o:o + Q_LORA + KV_LORA], w_kr, z64, _swap_halves(w_kr), z64], axis=1)
    wq3 = w_uq.reshape(Q_LORA, HEADS, HEAD_DIM + ROPE_DIM)
    rp = wq3[:, :, HEAD_DIM:]
    zq = jnp.zeros((Q_LORA, HEADS, LANES - ROPE_DIM), w_uq.dtype)
    w_q = jnp.concatenate([
        jnp.concatenate([wq3[:, :, :HEAD_DIM], rp, zq], axis=-1).reshape(Q_LORA, HEADS * QK_PAD),
        jnp.concatenate([_swap_halves(rp), zq], axis=-1).reshape(Q_LORA, HEADS * LANES)], axis=1)
    return dict(
        qkv=w_in[:, :o].astype(BF16),
        lat=w_lat.astype(BF16),
        gate=w_in[:, o + Q_LORA + KV_LORA + ROPE_DIM:].astype(BF16),
        q=w_q.astype(BF16),
        kv=jnp.concatenate([w_uk, w_uv], axis=1).astype(BF16),
        branch=w_branch.astype(BF16),
        out=w_out.astype(BF16),
        up=w_up.astype(BF16),
        down=w_down.astype(BF16),
        ple_gate=w_ple_gate.astype(BF16),
        ple=w_ple.astype(BF16),
    )


def _tile(t, want):
    return want if t % want == 0 else t


def _layer(x, ple, past, w, g, tri):
    b, n, d = x.shape
    t = b * n
    xf = x.reshape(t, d)
    tm = _tile(t, 1024)
    tm_s = _tile(t, 512)
    past_len = 0 if past is None else past[0].shape[1]
    if n >= tm:
        pos = jnp.arange(n) + past_len
    else:
        pos = jnp.tile(jnp.arange(n) + past_len, tm // n)
    cos2, sin2 = _rope_tables(pos)

    h = _norm(xf, g['mix_pre'], tm)
    qkv_b, kv_f = _qkv(h, w['qkv'], tm, 1024)
    cq, ckv_f, ckv_b, kr_f, kr_b = _latent(h, w['lat'], g['q'], g['kv'], cos2, sin2, tm)
    q_mla = _mla_q(cq, w['q'], cos2, sin2, tm_s).reshape(b, n, -1)
    gates = _gates(h, w['gate'], tm, 1024)
    k_mla, v_mla = _mla_kv(ckv_b, kr_b, w['kv'], tm)
    k_mla, v_mla = k_mla.reshape(b, n, -1), v_mla.reshape(b, n, -1)
    qkv_b = qkv_b.reshape(b, n, -1)

    if past is None:
        rsub = 2 if n % (2 * KEY_BLOCK) == 0 else 1
        o_sb = _sb_prompt(qkv_b, tri, rsub)
        o_mla = _mla_prompt(q_mla, k_mla, v_mla, _tile(n, 512), _tile(n, 512), 2)
    else:
        o_sb = _sb_decode(qkv_b, past[0], past[1], tri)
        k_old, v_old = _mla_kv(past[2].reshape(b * past_len, -1), past[3].reshape(b * past_len, -1),
                               w['kv'], _tile(b * past_len, 1024))
        o_mla = _mla_decode(q_mla, k_old.reshape(b, past_len, -1), v_old.reshape(b, past_len, -1),
                            k_mla, v_mla)

    merged = _merge(o_sb.reshape(t, -1), o_mla.reshape(t, -1), w['branch'], gates, tm, 1024)
    x1 = _out_proj(merged, w['out'], xf, g['mix_post'], tm_s)
    x2 = _ffn(x1, g['ffn_pre'], w['up'], w['down'], g['ffn_post'], tm_s, 1024)
    x3 = _ple(x2, ple.reshape(t, -1), g['ple_gate'], w['ple_gate'], w['ple'], g['ple_post'], tm_s)
    state = (kv_f[:, :SB_WIDTH].reshape(1, b, n, HEADS, HEAD_DIM),
             kv_f[:, SB_WIDTH:].reshape(1, b, n, HEADS, HEAD_DIM),
             ckv_f.reshape(1, b, n, KV_LORA), kr_f.reshape(1, b, n, ROPE_DIM))
    return x3.reshape(b, n, d), state


def kernel(x_prompt, x_sample, cache_sb_k, cache_sb_v, cache_mla_ckv, cache_mla_krope, p_prompt, p_sample, g_mix_pre, w_in, g_q, w_uq, g_kv, w_uk, w_uv, w_branch, w_out, g_mix_post, g_ffn_pre, w_up, w_down, g_ffn_post, g_ple_gate, w_ple_gate, w_ple, g_ple_post):
    assert w_in.shape[0] == 1, "single layer"
    w = _prep_weights(w_in[0], w_uq[0], w_uk[0], w_uv[0], w_branch[0], w_out[0], w_up[0], w_down[0],
                      w_ple_gate[0], w_ple[0])
    g = dict(mix_pre=g_mix_pre, q=g_q, kv=g_kv, mix_post=g_mix_post, ffn_pre=g_ffn_pre,
             ffn_post=g_ffn_post, ple_gate=g_ple_gate, ple_post=g_ple_post)
    idx = jnp.arange(KEY_BLOCK)
    lower = -(idx[:, None] > idx[None, :]).astype(BF16)
    tri = jnp.concatenate([lower, jnp.full((KEY_BLOCK, KEY_BLOCK), -1, BF16)], axis=1)
    tri = jnp.concatenate([tri, tri], axis=0)

    yp, sp = _layer(x_prompt, p_prompt[0], None, w, g, tri)
    past = (cache_sb_k[0], cache_sb_v[0], cache_mla_ckv[0], cache_mla_krope[0])
    ys, ss = _layer(x_sample, p_sample[0], past, w, g, tri)
    return (yp, ys) + sp + ss
```

```python
import functools

import jax
import jax.numpy as jnp
from jax import lax
from jax.experimental import pallas as pl
from jax.experimental.pallas import tpu as pltpu

F32 = jnp.float32
BF16 = jnp.bfloat16

D_MODEL = 2048
CHUNK = 64
PLE_DIM = 256
HEADS = 8
HEAD_DIM = 128
SB_WIDTH = HEADS * HEAD_DIM
ROPE_DIM = 64
Q_LORA = 512
KV_LORA = 512
QK_PAD = 256
D_FF = 4 * D_MODEL
ROPE_THETA = 10000.0
EPS = 1e-6
SB_SCALE = HEAD_DIM ** -0.5
MLA_SCALE = (HEAD_DIM + ROPE_DIM) ** -0.5
LOG2E = 1.4426950408889634

LANES = 128
KEY_BLOCK = 128
SB_EXIT = -104.0
MASKED = -1e30
VMEM_LIMIT = 56 * 1024 * 1024


def _params(*sem):
    return pltpu.CompilerParams(dimension_semantics=sem, vmem_limit_bytes=VMEM_LIMIT)


def _rms(xf, g):
    ms = jnp.mean(xf * xf, axis=-1, keepdims=True)
    return xf * lax.rsqrt(ms + EPS) * g


def _dot(a, b):
    return jnp.dot(a, b, preferred_element_type=F32)


def _dot_nt(a, b):
    return lax.dot_general(a, b, (((1,), (1,)), ((), ())), preferred_element_type=F32)


def _norm_kernel(x_ref, g_ref, o_ref):
    o_ref[...] = _rms(x_ref[...], g_ref[...]).astype(BF16)


def _norm(x, g, tm):
    t, d = x.shape
    return pl.pallas_call(
        _norm_kernel,
        grid=(t // tm,),
        in_specs=[pl.BlockSpec((tm, d), lambda i: (i, 0)),
                  pl.BlockSpec((1, d), lambda i: (0, 0))],
        out_specs=pl.BlockSpec((tm, d), lambda i: (i, 0)),
        out_shape=jax.ShapeDtypeStruct((t, d), BF16),
        compiler_params=_params("parallel"),
        name="pre_norm",
    )(x, g)


def _qkv_kernel(h_ref, w_ref, qkv_ref, kf_ref, vf_ref):
    j = pl.program_id(1)
    acc = _dot(h_ref[...], w_ref[...])
    qkv_ref[...] = (acc * jnp.where(j == 0, SB_SCALE, 1.0)).astype(BF16)

    def store_state(ref):
        for h in range(HEADS):
            ref[pl.ds(h, acc.shape[0], stride=HEADS), :] = acc[:, _head_cols(h)]

    pl.when(j == 1)(lambda: store_state(kf_ref))
    pl.when(j == 2)(lambda: store_state(vf_ref))


def _qkv(h, w, tm):
    t, d = h.shape
    state = pl.BlockSpec((tm * HEADS, HEAD_DIM), lambda i, j: (i, 0))
    return pl.pallas_call(
        _qkv_kernel,
        grid=(t // tm, 3),
        in_specs=[pl.BlockSpec((tm, d), lambda i, j: (i, 0)),
                  pl.BlockSpec((d, SB_WIDTH), lambda i, j: (0, j))],
        out_specs=[pl.BlockSpec((tm, SB_WIDTH), lambda i, j: (i, j)), state, state],
        out_shape=[jax.ShapeDtypeStruct((t, 3 * SB_WIDTH), BF16),
                   jax.ShapeDtypeStruct((t * HEADS, HEAD_DIM), F32),
                   jax.ShapeDtypeStruct((t * HEADS, HEAD_DIM), F32)],
        compiler_params=_params("parallel", "arbitrary"),
        name="sb_qkv_proj",
    )(h, w)


def _latent_kernel(h_ref, w_ref, gq_ref, gkv_ref, cos_ref, sin_ref,
                   cq_ref, ckvf_ref, ckvb_ref, krf_ref, krb_ref):
    acc = _dot(h_ref[...], w_ref[...])
    cq_ref[...] = _rms(acc[:, :Q_LORA], gq_ref[...]).astype(BF16)
    ckv = _rms(acc[:, Q_LORA:Q_LORA + KV_LORA], gkv_ref[...])
    ckvf_ref[...] = ckv
    ckvb_ref[...] = ckv.astype(BF16)
    base = Q_LORA + KV_LORA
    kr = acc[:, base:base + LANES] * cos_ref[...] + acc[:, base + LANES:base + 2 * LANES] * sin_ref[...]
    krf_ref[...] = kr[:, :ROPE_DIM]
    krb_ref[...] = kr[:, :ROPE_DIM].astype(BF16)


def _latent(h, w, gq, gkv, cos2, sin2, tm):
    t, d = h.shape
    n_tab = cos2.shape[0] // tm
    row = lambda i: (i, 0)
    const = lambda i: (0, 0)
    tab = lambda i: (i % n_tab, 0)
    wn = w.shape[1]
    return pl.pallas_call(
        _latent_kernel,
        grid=(t // tm,),
        in_specs=[pl.BlockSpec((tm, d), row), pl.BlockSpec((d, wn), const),
                  pl.BlockSpec((1, Q_LORA), const), pl.BlockSpec((1, KV_LORA), const),
                  pl.BlockSpec((tm, LANES), tab), pl.BlockSpec((tm, LANES), tab)],
        out_specs=[pl.BlockSpec((tm, Q_LORA), row), pl.BlockSpec((tm, KV_LORA), row),
                   pl.BlockSpec((tm, KV_LORA), row), pl.BlockSpec((tm, ROPE_DIM), row),
                   pl.BlockSpec((tm, ROPE_DIM), row)],
        out_shape=[jax.ShapeDtypeStruct((t, Q_LORA), BF16),
                   jax.ShapeDtypeStruct((t, KV_LORA), F32),
                   jax.ShapeDtypeStruct((t, KV_LORA), BF16),
                   jax.ShapeDtypeStruct((t, ROPE_DIM), F32),
                   jax.ShapeDtypeStruct((t, ROPE_DIM), BF16)],
        compiler_params=_params("parallel"),
        name="mla_latent_proj",
    )(h, w, gq, gkv, cos2, sin2)


def _mla_q_kernel(cq_ref, w_ref, cos_ref, sin_ref, q_ref):
    acc = _dot(cq_ref[...], w_ref[...])
    cos, sin = cos_ref[...], sin_ref[...]
    for h in range(HEADS):
        b = h * QK_PAD
        q_ref[:, b:b + LANES] = (acc[:, b:b + LANES] * MLA_SCALE).astype(BF16)
        sw = HEADS * QK_PAD + h * LANES
        rot = acc[:, b + LANES:b + QK_PAD] * cos + acc[:, sw:sw + LANES] * sin
        q_ref[:, b + LANES:b + QK_PAD] = (rot * MLA_SCALE).astype(BF16)


def _mla_q(cq, w, cos2, sin2, tm):
    t = cq.shape[0]
    n_tab = cos2.shape[0] // tm
    row = lambda i: (i, 0)
    tab = lambda i: (i % n_tab, 0)
    return pl.pallas_call(
        _mla_q_kernel,
        grid=(t // tm,),
        in_specs=[pl.BlockSpec((tm, Q_LORA), row),
                  pl.BlockSpec(w.shape, lambda i: (0, 0)),
                  pl.BlockSpec((tm, LANES), tab), pl.BlockSpec((tm, LANES), tab)],
        out_specs=pl.BlockSpec((tm, HEADS * QK_PAD), row),
        out_shape=jax.ShapeDtypeStruct((t, HEADS * QK_PAD), BF16),
        compiler_params=_params("parallel"),
        name="mla_q_proj",
    )(cq, w, cos2, sin2)


def _mla_qt_kernel(cq_ref, wt_ref, cos_ref, sin_ref, qt_ref):
    acc = _dot_nt(wt_ref[...], cq_ref[...])
    cos, sin = cos_ref[...], sin_ref[...]
    for h in range(HEADS):
        b = h * QK_PAD
        qt_ref[b:b + LANES, :] = (acc[b:b + LANES] * (MLA_SCALE * LOG2E)).astype(BF16)
        sw = HEADS * QK_PAD + h * LANES
        rot = acc[b + LANES:b + QK_PAD] * cos + acc[sw:sw + LANES] * sin
        qt_ref[b + LANES:b + QK_PAD, :] = (rot * (MLA_SCALE * LOG2E)).astype(BF16)


def _mla_qt(cq, wt, cos2t, sin2t, tm):
    t = cq.shape[0]
    n_tab = cos2t.shape[1] // tm
    tab = lambda i: (0, i % n_tab)
    return pl.pallas_call(
        _mla_qt_kernel,
        grid=(t // tm,),
        in_specs=[pl.BlockSpec((tm, Q_LORA), lambda i: (i, 0)),
                  pl.BlockSpec(wt.shape, lambda i: (0, 0)),
                  pl.BlockSpec((LANES, tm), tab), pl.BlockSpec((LANES, tm), tab)],
        out_specs=pl.BlockSpec((HEADS * QK_PAD, tm), lambda i: (0, i)),
        out_shape=jax.ShapeDtypeStruct((HEADS * QK_PAD, t), BF16),
        compiler_params=_params("parallel"),
        name="mla_q_proj_t",
    )(cq, wt, cos2t, sin2t)


def _mla_kv_kernel(ckv_ref, kr_ref, w_ref, k_ref, v_ref):
    acc = _dot(ckv_ref[...].astype(BF16), w_ref[...])
    kr = kr_ref[...].astype(BF16)
    zero = jnp.zeros((kr.shape[0], QK_PAD - LANES - ROPE_DIM), BF16)
    for h in range(HEADS):
        b = h * QK_PAD
        k_ref[:, b:b + LANES] = acc[:, h * LANES:(h + 1) * LANES].astype(BF16)
        k_ref[:, b + LANES:b + LANES + ROPE_DIM] = kr
        k_ref[:, b + LANES + ROPE_DIM:b + QK_PAD] = zero
    v_ref[...] = acc[:, SB_WIDTH:].astype(BF16)


def _mla_kv(ckv, kr, w, tm):
    t = ckv.shape[0]
    row = lambda i: (i, 0)
    return pl.pallas_call(
        _mla_kv_kernel,
        grid=(t // tm,),
        in_specs=[pl.BlockSpec((tm, KV_LORA), row), pl.BlockSpec((tm, ROPE_DIM), row),
                  pl.BlockSpec(w.shape, lambda i: (0, 0))],
        out_specs=[pl.BlockSpec((tm, HEADS * QK_PAD), row), pl.BlockSpec((tm, SB_WIDTH), row)],
        out_shape=[jax.ShapeDtypeStruct((t, HEADS * QK_PAD), BF16),
                   jax.ShapeDtypeStruct((t, SB_WIDTH), BF16)],
        compiler_params=_params("parallel"),
        name="mla_kv_proj",
    )(ckv, kr, w)


def _mla_kvt_kernel(ckv_ref, kr_ref, wk_ref, wvt_ref, k_ref, vt_ref):
    ckv = ckv_ref[...]
    acc = _dot(ckv, wk_ref[...])
    kr = kr_ref[...]
    zero = jnp.zeros((kr.shape[0], QK_PAD - LANES - ROPE_DIM), BF16)
    for h in range(HEADS):
        b = h * QK_PAD
        k_ref[:, b:b + LANES] = acc[:, h * LANES:(h + 1) * LANES].astype(BF16)
        k_ref[:, b + LANES:b + LANES + ROPE_DIM] = kr
        k_ref[:, b + LANES + ROPE_DIM:b + QK_PAD] = zero
    vt_ref[...] = _dot_nt(wvt_ref[...], ckv).astype(BF16)


def _mla_kvt(ckv, kr, wk, wvt, tm):
    t = ckv.shape[0]
    row = lambda i: (i, 0)
    const = lambda i: (0, 0)
    return pl.pallas_call(
        _mla_kvt_kernel,
        grid=(t // tm,),
        in_specs=[pl.BlockSpec((tm, KV_LORA), row), pl.BlockSpec((tm, ROPE_DIM), row),
                  pl.BlockSpec(wk.shape, const), pl.BlockSpec(wvt.shape, const)],
        out_specs=[pl.BlockSpec((tm, HEADS * QK_PAD), row),
                   pl.BlockSpec((None, SB_WIDTH, tm), lambda i: (i, 0, 0))],
        out_shape=[jax.ShapeDtypeStruct((t, HEADS * QK_PAD), BF16),
                   jax.ShapeDtypeStruct((t // tm, SB_WIDTH, tm), BF16)],
        compiler_params=_params("parallel"),
        name="mla_kv_proj_t",
    )(ckv, kr, wk, wvt)


def _gate_kernel(h_ref, w_ref, o_ref):
    o_ref[...] = jax.nn.sigmoid(_dot(h_ref[...], w_ref[...])).astype(BF16)


def _gates(h, w, tm, tn):
    t, d = h.shape
    n = w.shape[1]
    return pl.pallas_call(
        _gate_kernel,
        grid=(t // tm, n // tn),
        in_specs=[pl.BlockSpec((tm, d), lambda i, j: (i, 0)),
                  pl.BlockSpec((d, tn), lambda i, j: (0, j))],
        out_specs=pl.BlockSpec((tm, tn), lambda i, j: (i, j)),
        out_shape=jax.ShapeDtypeStruct((t, n), BF16),
        compiler_params=_params("parallel", "arbitrary"),
        name="branch_gates",
    )(h, w)


def _merge_kernel(osb_ref, omla_ref, wb_ref, g0_ref, g1_ref, o_ref):
    m = (g0_ref[...].astype(F32) * _dot(osb_ref[...], wb_ref[0])
         + g1_ref[...].astype(F32) * _dot(omla_ref[...], wb_ref[1]))
    o_ref[...] = m.astype(BF16)


def _merge(osb, omla, wb, gates, tm, tn):
    t = osb.shape[0]
    nj = D_MODEL // tn
    row = lambda i, j: (i, 0)
    return pl.pallas_call(
        _merge_kernel,
        grid=(t // tm, nj),
        in_specs=[pl.BlockSpec((tm, SB_WIDTH), row), pl.BlockSpec((tm, SB_WIDTH), row),
                  pl.BlockSpec((2, SB_WIDTH, tn), lambda i, j: (0, 0, j)),
                  pl.BlockSpec((tm, tn), lambda i, j: (i, j)),
                  pl.BlockSpec((tm, tn), lambda i, j: (i, j + nj))],
        out_specs=pl.BlockSpec((tm, tn), lambda i, j: (i, j)),
        out_shape=jax.ShapeDtypeStruct((t, D_MODEL), BF16),
        compiler_params=_params("parallel", "arbitrary"),
        name="branch_merge",
    )(osb, omla, wb, gates, gates)


def _out_proj_kernel(m_ref, w_ref, x_ref, g_ref, o_ref):
    o_ref[...] = x_ref[...] + _rms(_dot(m_ref[...], w_ref[...]), g_ref[...])


def _out_proj(m, w, x, g, tm):
    t, d = x.shape
    row = lambda i: (i, 0)
    const = lambda i: (0, 0)
    return pl.pallas_call(
        _out_proj_kernel,
        grid=(t // tm,),
        in_specs=[pl.BlockSpec((tm, d), row), pl.BlockSpec((d, d), const),
                  pl.BlockSpec((tm, d), row), pl.BlockSpec((1, d), const)],
        out_specs=pl.BlockSpec((tm, d), row),
        out_shape=jax.ShapeDtypeStruct((t, d), F32),
        compiler_params=_params("parallel"),
        name="mix_out_proj",
    )(m, w, x, g)


def _ffn_kernel(x_ref, gpre_ref, wu_ref, wd_ref, gpost_ref, o_ref, h_scr, acc_scr):
    j = pl.program_id(1)

    @pl.when(j == 0)
    def _():
        h_scr[...] = _rms(x_ref[...], gpre_ref[...]).astype(BF16)
        acc_scr[...] = jnp.zeros_like(acc_scr)

    u = jnp.maximum(_dot(h_scr[...], wu_ref[...]), 0.0)
    acc_scr[...] += _dot((u * u).astype(BF16), wd_ref[...])

    @pl.when(j == pl.num_programs(1) - 1)
    def _():
        o_ref[...] = x_ref[...] + _rms(acc_scr[...], gpost_ref[...])


def _ffn(x, gpre, wu, wd, gpost, tm, tf):
    t, d = x.shape
    row = lambda i, j: (i, 0)
    const = lambda i, j: (0, 0)
    return pl.pallas_call(
        _ffn_kernel,
        grid=(t // tm, D_FF // tf),
        in_specs=[pl.BlockSpec((tm, d), row), pl.BlockSpec((1, d), const),
                  pl.BlockSpec((d, tf), lambda i, j: (0, j)),
                  pl.BlockSpec((tf, d), lambda i, j: (j, 0)),
                  pl.BlockSpec((1, d), const)],
        out_specs=pl.BlockSpec((tm, d), row),
        out_shape=jax.ShapeDtypeStruct((t, d), F32),
        scratch_shapes=[pltpu.VMEM((tm, d), BF16), pltpu.VMEM((tm, d), F32)],
        compiler_params=_params("parallel", "arbitrary"),
        name="sqrelu_ffn",
    )(x, gpre, wu, wd, gpost)


def _ple_kernel(x_ref, p_ref, gg_ref, wg_ref, wp_ref, gpost_ref, o_ref):
    x = x_ref[...]
    gate = jax.nn.sigmoid(_dot(_rms(x, gg_ref[...]).astype(BF16), wg_ref[...]))
    pe = _dot(p_ref[...].astype(BF16), wp_ref[...])
    o_ref[...] = x + _rms(pe * gate, gpost_ref[...])


def _ple(x, p, gg, wg, wp, gpost, tm):
    t, d = x.shape
    row = lambda i: (i, 0)
    const = lambda i: (0, 0)
    return pl.pallas_call(
        _ple_kernel,
        grid=(t // tm,),
        in_specs=[pl.BlockSpec((tm, d), row), pl.BlockSpec((tm, PLE_DIM), row),
                  pl.BlockSpec((1, d), const), pl.BlockSpec((d, d), const),
                  pl.BlockSpec((PLE_DIM, d), const), pl.BlockSpec((1, d), const)],
        out_specs=pl.BlockSpec((tm, d), row),
        out_shape=jax.ShapeDtypeStruct((t, d), F32),
        compiler_params=_params("parallel"),
        name="ple_embed",
    )(x, p, gg, wg, wp, gpost)


def _sb_step(qs, ks, vs, tri, carry, mask):
    rows = qs[0].shape[0]
    z = jnp.concatenate([_dot_nt(q, k) for q, k in zip(qs, ks)], axis=0)
    sp = jnp.maximum(z, 0.0) + jnp.log(1.0 + jnp.exp(-jnp.abs(z)))
    spm = sp if mask is None else jnp.where(mask, sp, 0.0)
    hi = spm.astype(BF16)
    lo = (spm - hi.astype(F32)).astype(BF16)
    cs = _dot(jnp.concatenate([hi, lo], axis=1), tri)
    after = cs[:, :KEY_BLOCK] if carry is None else cs[:, :KEY_BLOCK] + carry
    w = jnp.exp((z - sp) + after)
    if mask is not None:
        w = jnp.where(mask, w, 0.0)
    w = w.astype(BF16)
    pvs = [_dot(w[i * rows:(i + 1) * rows], v) for i, v in enumerate(vs)]
    carry = cs[:, KEY_BLOCK:] if carry is None else carry + cs[:, KEY_BLOCK:]
    return carry, pvs


def _sb_walk(first_blocks, later_blocks, tri_ref, carry_scr, acc_scr, o_ref):
    def run(chains, carry, mask, first):
        carry, pvs = _sb_step([c[2] for c in chains], [c[3] for c in chains], [c[4] for c in chains],
                              tri_ref[...], carry, mask)
        carry_scr[...] = carry
        for (rows, cols, _, _, _), pv in zip(chains, pvs):
            if first:
                acc_scr[rows, cols] = pv
            else:
                acc_scr[rows, cols] += pv
        return jnp.max(carry)

    def step0():
        chains, mask = first_blocks()
        return run(chains, None, mask, True)

    def step(j):
        chains, valid = later_blocks(j)
        rows = chains[0][2].shape[0]
        carry = jnp.concatenate(
            [jnp.where(ok, carry_scr[i * rows:(i + 1) * rows, :], MASKED) for i, ok in enumerate(valid)],
            axis=0)
        return run(chains, carry, None, False)

    lax.while_loop(lambda st: st[1] > SB_EXIT,
                   lambda st: (st[0] + 1, step(st[0])),
                   (jnp.int32(1), step0()))
    o_ref[...] = acc_scr[...].astype(BF16)


def _head_cols(h):
    return slice(h * HEAD_DIM, (h + 1) * HEAD_DIM)


def _sb_prompt_kernel(q_ref, k_ref, v_ref, tri_ref, o_ref, carry_scr, acc_scr, *, rsub):
    bk = KEY_BLOCK
    base = pl.program_id(1) * rsub
    n_chain = rsub * HEADS
    r = lax.broadcasted_iota(jnp.int32, (n_chain * bk, bk), 0)
    c = lax.broadcasted_iota(jnp.int32, (n_chain * bk, bk), 1)
    causal = c < jnp.bitwise_and(r, bk - 1)

    def blocks(j):
        chains, valid = [], []
        for s in range(rsub):
            kb = base + s - j
            start = pl.multiple_of(jnp.maximum(kb, 0) * bk, bk)
            rows = slice(s * bk, (s + 1) * bk)
            for h in range(HEADS):
                cols = _head_cols(h)
                chains.append((rows, cols, q_ref[rows, cols],
                               k_ref[pl.ds(start, bk), cols], v_ref[pl.ds(start, bk), cols]))
                valid.append(kb >= 0)
        return chains, valid

    _sb_walk(lambda: (blocks(0)[0], causal), blocks, tri_ref, carry_scr, acc_scr, o_ref)


def _sb_prompt(qkv, tri, rsub):
    b, n, _ = qkv.shape
    bq = rsub * KEY_BLOCK
    full = (None, n, SB_WIDTH)
    return pl.pallas_call(
        functools.partial(_sb_prompt_kernel, rsub=rsub),
        grid=(b, n // bq),
        in_specs=[pl.BlockSpec((None, bq, SB_WIDTH), lambda i, j: (i, j, 0)),
                  pl.BlockSpec(full, lambda i, j: (i, 0, 1)),
                  pl.BlockSpec(full, lambda i, j: (i, 0, 2)),
                  pl.BlockSpec(tri.shape, lambda i, j: (0, 0))],
        out_specs=pl.BlockSpec((None, bq, SB_WIDTH), lambda i, j: (i, j, 0)),
        out_shape=jax.ShapeDtypeStruct((b, n, SB_WIDTH), BF16),
        scratch_shapes=[pltpu.VMEM((rsub * HEADS * KEY_BLOCK, KEY_BLOCK), F32),
                        pltpu.VMEM((bq, SB_WIDTH), F32)],
        compiler_params=_params("parallel", "arbitrary"),
        name="stickbreak_attn",
    )(qkv, qkv, qkv, tri)


def _sb_decode_kernel(q_ref, kn_ref, vn_ref, kp_ref, vp_ref, tri_ref, o_ref, carry_scr, acc_scr,
                      *, n, past):
    bk = KEY_BLOCK
    nb = past // bk
    assert n & (n - 1) == 0
    r = lax.broadcasted_iota(jnp.int32, (HEADS * n, bk), 0)
    c = lax.broadcasted_iota(jnp.int32, (HEADS * n, bk), 1)
    causal = c < jnp.bitwise_and(r, n - 1)
    zpad = jnp.zeros((bk - n, HEAD_DIM), BF16)
    rows = slice(0, n)

    def first():
        return [(rows, _head_cols(h), q_ref[:, _head_cols(h)],
                 jnp.concatenate([kn_ref[:, _head_cols(h)], zpad], axis=0),
                 jnp.concatenate([vn_ref[:, _head_cols(h)], zpad], axis=0))
                for h in range(HEADS)], causal

    def later(j):
        kb = nb - j
        start = jnp.maximum(kb, 0) * (bk * HEADS)
        chains = []
        for h in range(HEADS):
            sl = pl.ds(start + h, bk, stride=HEADS)
            chains.append((rows, _head_cols(h), q_ref[:, _head_cols(h)],
                           kp_ref[sl, :].astype(BF16), vp_ref[sl, :].astype(BF16)))
        return chains, [kb >= 0] * HEADS

    _sb_walk(first, later, tri_ref, carry_scr, acc_scr, o_ref)


def _sb_decode(qkv, k_past, v_past, tri):
    b, n, _ = qkv.shape
    past = k_past.shape[1]
    assert past % KEY_BLOCK == 0 and n <= KEY_BLOCK and n % 16 == 0
    new = (None, n, SB_WIDTH)
    old = (None, past * HEADS, HEAD_DIM)
    return pl.pallas_call(
        functools.partial(_sb_decode_kernel, n=n, past=past),
        grid=(b,),
        in_specs=[pl.BlockSpec(new, lambda i: (i, 0, 0)),
                  pl.BlockSpec(new, lambda i: (i, 0, 1)),
                  pl.BlockSpec(new, lambda i: (i, 0, 2)),
                  pl.BlockSpec(old, lambda i: (i, 0, 0)),
                  pl.BlockSpec(old, lambda i: (i, 0, 0)),
                  pl.BlockSpec(tri.shape, lambda i: (0, 0))],
        out_specs=pl.BlockSpec(new, lambda i: (i, 0, 0)),
        out_shape=jax.ShapeDtypeStruct((b, n, SB_WIDTH), BF16),
        scratch_shapes=[pltpu.VMEM((HEADS * n, KEY_BLOCK), F32),
                        pltpu.VMEM((n, SB_WIDTH), F32)],
        compiler_params=_params("parallel"),
        name="stickbreak_attn_decode",
    )(qkv, qkv, qkv, k_past.reshape(b, past * HEADS, HEAD_DIM),
      v_past.reshape(b, past * HEADS, HEAD_DIM), tri)


CHUNK_SHIFT = CHUNK.bit_length() - 1
assert 1 << CHUNK_SHIFT == CHUNK


def _chunk(pos):
    return lax.shift_right_arithmetic(pos, CHUNK_SHIFT)


def _mla_prompt_kernel(qt_ref, k_ref, vt_ref, o_ref, s_scr, *, bk):
    bq = 2 * bk
    qi = pl.program_id(2)

    def scores(kb, slot):
        s0 = pl.multiple_of(kb * bk, bk)
        s_scr[slot] = _dot(k_ref[pl.ds(s0, bk), :], qt_ref[...])

    def absorb(st, s, kb):
        m, l, acc = st
        m_new = jnp.maximum(m, jnp.max(s, axis=0, keepdims=True))
        alpha = jnp.exp2(m - m_new)
        p = jnp.exp2(s - m_new)
        l = alpha * l + jnp.sum(p, axis=0, keepdims=True)
        acc = alpha * acc + _dot(vt_ref[kb], p.astype(BF16))
        return m_new, l, acc

    def pair(i, st):
        kb = 2 * i
        scores(kb + 1, 1)
        st = absorb(st, s_scr[0], kb)
        scores(kb + 2, 0)
        return absorb(st, s_scr[1], kb + 1)

    init = (jnp.full((1, bq), MASKED, F32), jnp.zeros((1, bq), F32), jnp.zeros((HEAD_DIM, bq), F32))
    scores(0, 0)
    st = lax.fori_loop(0, qi, pair, init)
    late = (slice(None), slice(bk, bq))
    s_late = _dot(k_ref[pl.ds(pl.multiple_of((2 * qi + 1) * bk, bk), bk), :], qt_ref[late])
    vis = (_chunk(lax.broadcasted_iota(jnp.int32, (bk, bq), 0))
           <= _chunk(lax.broadcasted_iota(jnp.int32, (bk, bq), 1)))
    m, l, acc = absorb(st, jnp.where(vis, s_scr[0], MASKED), 2 * qi)
    _, l2, acc2 = absorb((m[late], l[late], acc[late]), jnp.where(vis[:, :bk], s_late, MASKED), 2 * qi + 1)
    l = jnp.concatenate([l[:, :bk], l2], axis=1)
    acc = jnp.concatenate([acc[:, :bk], acc2], axis=1)
    o_ref[...] = (acc / l).T.astype(BF16)


def _mla_prompt(qt, k, vt, b):
    n = k.shape[1]
    bk = vt.shape[-1]
    bq = 2 * bk
    assert bk % CHUNK == 0 and n % bq == 0
    nq = n // bq
    return pl.pallas_call(
        functools.partial(_mla_prompt_kernel, bk=bk),
        grid=(b, HEADS, nq),
        in_specs=[pl.BlockSpec((QK_PAD, bq), lambda i, h, j: (h, i * nq + j)),
                  pl.BlockSpec((None, n, QK_PAD), lambda i, h, j: (i, 0, h)),
                  pl.BlockSpec((None, n // bk, HEAD_DIM, bk), lambda i, h, j: (i, 0, h, 0))],
        out_specs=pl.BlockSpec((None, bq, HEAD_DIM), lambda i, h, j: (i, j, h)),
        out_shape=jax.ShapeDtypeStruct((b, n, SB_WIDTH), BF16),
        scratch_shapes=[pltpu.VMEM((2, bk, bq), F32)],
        compiler_params=_params("parallel", "parallel", "arbitrary"),
        name="mla_attn",
    )(qt, k, vt)


def _mla_decode_kernel(q_ref, kp_ref, vp_ref, kn_ref, vn_ref, o_ref, *, n, past):
    r = lax.broadcasted_iota(jnp.int32, (n, LANES), 0)
    c = lax.broadcasted_iota(jnp.int32, (n, LANES), 1)
    vis = jnp.logical_and(c < n, _chunk(past + c) <= _chunk(past + r))
    kpad = jnp.zeros((LANES - n, QK_PAD), BF16)
    vpad = jnp.zeros((LANES - n, HEAD_DIM), BF16)
    for h in range(HEADS):
        qk = slice(h * QK_PAD, (h + 1) * QK_PAD)
        q = q_ref[:, qk]
        s_old = _dot_nt(q, kp_ref[:, qk])
        s_new = jnp.where(vis, _dot_nt(q, jnp.concatenate([kn_ref[:, qk], kpad], axis=0)), MASKED)
        m = jnp.maximum(jnp.max(s_old, axis=-1, keepdims=True), jnp.max(s_new, axis=-1, keepdims=True))
        p_old = jnp.exp(s_old - m)
        p_new = jnp.exp(s_new - m)
        l = jnp.sum(p_old, axis=-1, keepdims=True) + jnp.sum(p_new, axis=-1, keepdims=True)
        acc = (_dot(p_old.astype(BF16), vp_ref[:, _head_cols(h)])
               + _dot(p_new.astype(BF16), jnp.concatenate([vn_ref[:, _head_cols(h)], vpad], axis=0)))
        o_ref[:, _head_cols(h)] = (acc / l).astype(BF16)


def _mla_decode(q, k_past, v_past, k_new, v_new):
    b, n, _ = q.shape
    past = k_past.shape[1]
    assert n <= LANES and n % 16 == 0
    spec = lambda rows, width: pl.BlockSpec((None, rows, width), lambda i: (i, 0, 0))
    return pl.pallas_call(
        functools.partial(_mla_decode_kernel, n=n, past=past),
        grid=(b,),
        in_specs=[spec(n, HEADS * QK_PAD), spec(past, HEADS * QK_PAD), spec(past, SB_WIDTH),
                  spec(n, HEADS * QK_PAD), spec(n, SB_WIDTH)],
        out_specs=spec(n, SB_WIDTH),
        out_shape=jax.ShapeDtypeStruct((b, n, SB_WIDTH), BF16),
        compiler_params=_params("parallel"),
        name="mla_attn_decode",
    )(q, k_past, v_past, k_new, v_new)


def _rope_tables(pos):
    half = ROPE_DIM // 2
    freqs = ROPE_THETA ** (-jnp.arange(half, dtype=F32) / half)
    ang = pos.astype(F32)[:, None] * freqs[None, :]
    cos, sin = jnp.cos(ang), jnp.sin(ang)
    z = jnp.zeros((pos.shape[0], LANES - ROPE_DIM), F32)
    return jnp.concatenate([cos, cos, z], axis=1), jnp.concatenate([-sin, sin, z], axis=1)


def _swap_halves(w):
    half = w.shape[-1] // 2
    return jnp.concatenate([w[..., half:], w[..., :half]], axis=-1)


def _prep_weights(w_in, w_uq, w_uk, w_uv, w_branch, w_out, w_up, w_down, w_ple_gate, w_ple):
    d = w_in.shape[0]
    o = 3 * SB_WIDTH
    w_kr = w_in[:, o + Q_LORA + KV_LORA:o + Q_LORA + KV_LORA + ROPE_DIM]
    z64 = jnp.zeros((d, LANES - ROPE_DIM), w_in.dtype)
    w_lat = jnp.concatenate([w_in[:, o:o + Q_LORA + KV_LORA], w_kr, z64, _swap_halves(w_kr), z64], axis=1)
    wq3 = w_uq.reshape(Q_LORA, HEADS, HEAD_DIM + ROPE_DIM)
    rp = wq3[:, :, HEAD_DIM:]
    zq = jnp.zeros((Q_LORA, HEADS, LANES - ROPE_DIM), w_uq.dtype)
    w_q = jnp.concatenate([
        jnp.concatenate([wq3[:, :, :HEAD_DIM], rp, zq], axis=-1).reshape(Q_LORA, HEADS * QK_PAD),
        jnp.concatenate([_swap_halves(rp), zq], axis=-1).reshape(Q_LORA, HEADS * LANES)], axis=1)
    return dict(
        qkv=w_in.astype(BF16),
        lat=w_lat.astype(BF16),
        gate=w_in[:, o + Q_LORA + KV_LORA + ROPE_DIM:].astype(BF16),
        q=w_q.astype(BF16),
        qt=w_q.T.astype(BF16),
        kv=jnp.concatenate([w_uk, w_uv], axis=1).astype(BF16),
        uk=w_uk.astype(BF16),
        uvt=w_uv.T.astype(BF16),
        branch=w_branch.astype(BF16),
        out=w_out.astype(BF16),
        up=w_up.astype(BF16),
        down=w_down.astype(BF16),
        ple_gate=w_ple_gate.astype(BF16),
        ple=w_ple.astype(BF16),
    )


def _tile(t, want):
    return want if t % want == 0 else t


def _layer(x, ple, past, w, g, tri):
    b, n, d = x.shape
    t = b * n
    xf = x.reshape(t, d)
    tm = _tile(t, 1024)
    tm_s = _tile(t, 512)
    past_len = 0 if past is None else past[0].shape[1]
    if n >= tm:
        pos = jnp.arange(n) + past_len
    else:
        pos = jnp.tile(jnp.arange(n) + past_len, tm // n)
    cos2, sin2 = _rope_tables(pos)

    h = _norm(xf, g['mix_pre'], tm)
    qkv_b, k_f, v_f = _qkv(h, w['qkv'], tm)
    cq, ckv_f, ckv_b, kr_f, kr_b = _latent(h, w['lat'], g['q'], g['kv'], cos2, sin2, tm)
    gates = _gates(h, w['gate'], tm, 1024)
    qkv_b = qkv_b.reshape(b, n, -1)

    if past is None:
        blk = _tile(n, 512)
        qt = _mla_qt(cq, w['qt'], cos2.T, sin2.T, blk)
        k_mla, vt = _mla_kvt(ckv_b, kr_b, w['uk'], w['uvt'], blk)
        o_sb = _sb_prompt(qkv_b, tri, 2 if n % (2 * KEY_BLOCK) == 0 else 1)
        o_mla = _mla_prompt(qt, k_mla.reshape(b, n, -1), vt.reshape(b, n // blk, SB_WIDTH, blk), b)
    else:
        q_mla = _mla_q(cq, w['q'], cos2, sin2, tm_s).reshape(b, n, -1)
        k_mla, v_mla = _mla_kv(ckv_b, kr_b, w['kv'], tm)
        o_sb = _sb_decode(qkv_b, past[0], past[1], tri)
        k_old, v_old = _mla_kv(past[2].reshape(b * past_len, -1), past[3].reshape(b * past_len, -1),
                               w['kv'], _tile(b * past_len, 1024))
        o_mla = _mla_decode(q_mla, k_old.reshape(b, past_len, -1), v_old.reshape(b, past_len, -1),
                            k_mla.reshape(b, n, -1), v_mla.reshape(b, n, -1))

    merged = _merge(o_sb.reshape(t, -1), o_mla.reshape(t, -1), w['branch'], gates, tm, 1024)
    x1 = _out_proj(merged, w['out'], xf, g['mix_post'], tm_s)
    x2 = _ffn(x1, g['ffn_pre'], w['up'], w['down'], g['ffn_post'], tm_s, 1024)
    x3 = _ple(x2, ple.reshape(t, -1), g['ple_gate'], w['ple_gate'], w['ple'], g['ple_post'], tm_s)
    state = (k_f.reshape(1, b, n, HEADS, HEAD_DIM), v_f.reshape(1, b, n, HEADS, HEAD_DIM),
             ckv_f.reshape(1, b, n, KV_LORA), kr_f.reshape(1, b, n, ROPE_DIM))
    return x3.reshape(b, n, d), state


def kernel(x_prompt, x_sample, cache_sb_k, cache_sb_v, cache_mla_ckv, cache_mla_krope, p_prompt, p_sample, g_mix_pre, w_in, g_q, w_uq, g_kv, w_uk, w_uv, w_branch, w_out, g_mix_post, g_ffn_pre, w_up, w_down, g_ffn_post, g_ple_gate, w_ple_gate, w_ple, g_ple_post):
    assert w_in.shape[0] == 1, "single layer"
    w = _prep_weights(w_in[0], w_uq[0], w_uk[0], w_uv[0], w_branch[0], w_out[0], w_up[0], w_down[0],
                      w_ple_gate[0], w_ple[0])
    g = dict(mix_pre=g_mix_pre, q=g_q, kv=g_kv, mix_post=g_mix_post, ffn_pre=g_ffn_pre,
             ffn_post=g_ffn_post, ple_gate=g_ple_gate, ple_post=g_ple_post)
    idx = jnp.arange(KEY_BLOCK)
    lower = -(idx[:, None] > idx[None, :]).astype(BF16)
    tri = jnp.concatenate([lower, jnp.full((KEY_BLOCK, KEY_BLOCK), -1, BF16)], axis=1)
    tri = jnp.concatenate([tri, tri], axis=0)

    yp, sp = _layer(x_prompt, p_prompt[0], None, w, g, tri)
    past = (cache_sb_k[0], cache_sb_v[0], cache_mla_ckv[0], cache_mla_krope[0])
    ys, ss = _layer(x_sample, p_sample[0], past, w, g, tri)
    return (yp, ys) + sp + ss
```

```python
import functools

import jax
import jax.numpy as jnp
from jax import lax
from jax.experimental import pallas as pl
from jax.experimental.pallas import tpu as pltpu

F32 = jnp.float32
BF16 = jnp.bfloat16

D_MODEL = 2048
CHUNK = 64
PLE_DIM = 256
HEADS = 8
HEAD_DIM = 128
SB_WIDTH = HEADS * HEAD_DIM
ROPE_DIM = 64
Q_LORA = 512
KV_LORA = 512
QK_PAD = 256
D_FF = 4 * D_MODEL
ROPE_THETA = 10000.0
EPS = 1e-6
SB_SCALE = HEAD_DIM ** -0.5
MLA_SCALE = (HEAD_DIM + ROPE_DIM) ** -0.5
LOG2E = 1.4426950408889634

LANES = 128
KEY_BLOCK = 128
SB_EXIT = -104.0
MASKED = -1e30
VMEM_LIMIT = 56 * 1024 * 1024


def _params(*sem):
    return pltpu.CompilerParams(dimension_semantics=sem, vmem_limit_bytes=VMEM_LIMIT)


def _rms(xf, g):
    ms = jnp.mean(xf * xf, axis=-1, keepdims=True)
    return xf * lax.rsqrt(ms + EPS) * g


def _dot(a, b):
    return jnp.dot(a, b, preferred_element_type=F32)


def _dot_nt(a, b):
    return lax.dot_general(a, b, (((1,), (1,)), ((), ())), preferred_element_type=F32)


def _qkv_kernel(x_ref, g_ref, w_ref, h_ref, qkv_ref, kf_ref, vf_ref, h_scr):
    j = pl.program_id(1)

    @pl.when(j == 0)
    def _():
        h = _rms(x_ref[...], g_ref[...]).astype(BF16)
        h_scr[...] = h
        h_ref[...] = h

    acc = _dot(h_scr[...], w_ref[...])
    qkv_ref[...] = (acc * jnp.where(j == 0, SB_SCALE, 1.0)).astype(BF16)

    def store_state(ref):
        for h in range(HEADS):
            ref[pl.ds(h, acc.shape[0], stride=HEADS), :] = acc[:, _head_cols(h)]

    pl.when(j == 1)(lambda: store_state(kf_ref))
    pl.when(j == 2)(lambda: store_state(vf_ref))


def _qkv(x, g, w, tm):
    t, d = x.shape
    row = lambda i, j: (i, 0)
    state = pl.BlockSpec((tm * HEADS, HEAD_DIM), row)
    return pl.pallas_call(
        _qkv_kernel,
        grid=(t // tm, 3),
        in_specs=[pl.BlockSpec((tm, d), row), pl.BlockSpec((1, d), lambda i, j: (0, 0)),
                  pl.BlockSpec((d, SB_WIDTH), lambda i, j: (0, j))],
        out_specs=[pl.BlockSpec((tm, d), row), pl.BlockSpec((tm, SB_WIDTH), lambda i, j: (i, j)),
                   state, state],
        out_shape=[jax.ShapeDtypeStruct((t, d), BF16),
                   jax.ShapeDtypeStruct((t, 3 * SB_WIDTH), BF16),
                   jax.ShapeDtypeStruct((t * HEADS, HEAD_DIM), F32),
                   jax.ShapeDtypeStruct((t * HEADS, HEAD_DIM), F32)],
        scratch_shapes=[pltpu.VMEM((tm, d), BF16)],
        compiler_params=_params("parallel", "arbitrary"),
        name="norm_qkv_proj",
    )(x, g, w)


def _latent_kernel(h_ref, w_ref, gq_ref, gkv_ref, cos_ref, sin_ref,
                   cq_ref, ckvf_ref, ckvb_ref, krf_ref, krb_ref):
    acc = _dot(h_ref[...], w_ref[...])
    cq_ref[...] = _rms(acc[:, :Q_LORA], gq_ref[...]).astype(BF16)
    ckv = _rms(acc[:, Q_LORA:Q_LORA + KV_LORA], gkv_ref[...])
    ckvf_ref[...] = ckv
    ckvb_ref[...] = ckv.astype(BF16)
    base = Q_LORA + KV_LORA
    kr = acc[:, base:base + LANES] * cos_ref[...] + acc[:, base + LANES:base + 2 * LANES] * sin_ref[...]
    krf_ref[...] = kr[:, :ROPE_DIM]
    krb_ref[...] = kr[:, :ROPE_DIM].astype(BF16)


def _latent(h, w, gq, gkv, cos2, sin2, tm):
    t, d = h.shape
    n_tab = cos2.shape[0] // tm
    row = lambda i: (i, 0)
    const = lambda i: (0, 0)
    tab = lambda i: (i % n_tab, 0)
    wn = w.shape[1]
    return pl.pallas_call(
        _latent_kernel,
        grid=(t // tm,),
        in_specs=[pl.BlockSpec((tm, d), row), pl.BlockSpec((d, wn), const),
                  pl.BlockSpec((1, Q_LORA), const), pl.BlockSpec((1, KV_LORA), const),
                  pl.BlockSpec((tm, LANES), tab), pl.BlockSpec((tm, LANES), tab)],
        out_specs=[pl.BlockSpec((tm, Q_LORA), row), pl.BlockSpec((tm, KV_LORA), row),
                   pl.BlockSpec((tm, KV_LORA), row), pl.BlockSpec((tm, ROPE_DIM), row),
                   pl.BlockSpec((tm, ROPE_DIM), row)],
        out_shape=[jax.ShapeDtypeStruct((t, Q_LORA), BF16),
                   jax.ShapeDtypeStruct((t, KV_LORA), F32),
                   jax.ShapeDtypeStruct((t, KV_LORA), BF16),
                   jax.ShapeDtypeStruct((t, ROPE_DIM), F32),
                   jax.ShapeDtypeStruct((t, ROPE_DIM), BF16)],
        compiler_params=_params("parallel"),
        name="mla_latent_proj",
    )(h, w, gq, gkv, cos2, sin2)


def _mla_q_kernel(cq_ref, w_ref, cos_ref, sin_ref, q_ref):
    acc = _dot(cq_ref[...], w_ref[...])
    cos, sin = cos_ref[...], sin_ref[...]
    for h in range(HEADS):
        b = h * QK_PAD
        q_ref[:, b:b + LANES] = (acc[:, b:b + LANES] * MLA_SCALE).astype(BF16)
        sw = HEADS * QK_PAD + h * LANES
        rot = acc[:, b + LANES:b + QK_PAD] * cos + acc[:, sw:sw + LANES] * sin
        q_ref[:, b + LANES:b + QK_PAD] = (rot * MLA_SCALE).astype(BF16)


def _mla_q(cq, w, cos2, sin2, tm):
    t = cq.shape[0]
    n_tab = cos2.shape[0] // tm
    row = lambda i: (i, 0)
    tab = lambda i: (i % n_tab, 0)
    return pl.pallas_call(
        _mla_q_kernel,
        grid=(t // tm,),
        in_specs=[pl.BlockSpec((tm, Q_LORA), row),
                  pl.BlockSpec(w.shape, lambda i: (0, 0)),
                  pl.BlockSpec((tm, LANES), tab), pl.BlockSpec((tm, LANES), tab)],
        out_specs=pl.BlockSpec((tm, HEADS * QK_PAD), row),
        out_shape=jax.ShapeDtypeStruct((t, HEADS * QK_PAD), BF16),
        compiler_params=_params("parallel"),
        name="mla_q_proj",
    )(cq, w, cos2, sin2)


def _mla_qt_kernel(cq_ref, wt_ref, cos_ref, sin_ref, qt_ref):
    acc = _dot_nt(wt_ref[...], cq_ref[...])
    cos, sin = cos_ref[...], sin_ref[...]
    half = ROPE_DIM // 2
    scale = MLA_SCALE * LOG2E
    for h in range(HEADS):
        b = h * QK_PAD
        r = b + HEAD_DIM
        qt_ref[b:r, :] = (acc[b:r] * scale).astype(BF16)
        x1, x2 = acc[r:r + half], acc[r + half:r + ROPE_DIM]
        qt_ref[r:r + half, :] = ((x1 * cos - x2 * sin) * scale).astype(BF16)
        qt_ref[r + half:r + ROPE_DIM, :] = ((x1 * sin + x2 * cos) * scale).astype(BF16)
        qt_ref[r + ROPE_DIM:b + QK_PAD, :] = jnp.zeros((QK_PAD - HEAD_DIM - ROPE_DIM, acc.shape[1]), BF16)


def _mla_qt(cq, wt, cos_t, sin_t, tm):
    t = cq.shape[0]
    n_tab = cos_t.shape[1] // tm
    tab = lambda i: (0, i % n_tab)
    half = ROPE_DIM // 2
    return pl.pallas_call(
        _mla_qt_kernel,
        grid=(t // tm,),
        in_specs=[pl.BlockSpec((tm, Q_LORA), lambda i: (i, 0)),
                  pl.BlockSpec(wt.shape, lambda i: (0, 0)),
                  pl.BlockSpec((half, tm), tab), pl.BlockSpec((half, tm), tab)],
        out_specs=pl.BlockSpec((HEADS * QK_PAD, tm), lambda i: (0, i)),
        out_shape=jax.ShapeDtypeStruct((HEADS * QK_PAD, t), BF16),
        compiler_params=_params("parallel"),
        name="mla_q_proj_t",
    )(cq, wt, cos_t, sin_t)


def _mla_kv_kernel(ckv_ref, kr_ref, w_ref, k_ref, v_ref):
    acc = _dot(ckv_ref[...].astype(BF16), w_ref[...])
    kr = kr_ref[...].astype(BF16)
    zero = jnp.zeros((kr.shape[0], QK_PAD - LANES - ROPE_DIM), BF16)
    for h in range(HEADS):
        b = h * QK_PAD
        k_ref[:, b:b + LANES] = acc[:, h * LANES:(h + 1) * LANES].astype(BF16)
        k_ref[:, b + LANES:b + LANES + ROPE_DIM] = kr
        k_ref[:, b + LANES + ROPE_DIM:b + QK_PAD] = zero
    v_ref[...] = acc[:, SB_WIDTH:].astype(BF16)


def _mla_kv(ckv, kr, w, tm):
    t = ckv.shape[0]
    row = lambda i: (i, 0)
    return pl.pallas_call(
        _mla_kv_kernel,
        grid=(t // tm,),
        in_specs=[pl.BlockSpec((tm, KV_LORA), row), pl.BlockSpec((tm, ROPE_DIM), row),
                  pl.BlockSpec(w.shape, lambda i: (0, 0))],
        out_specs=[pl.BlockSpec((tm, HEADS * QK_PAD), row), pl.BlockSpec((tm, SB_WIDTH), row)],
        out_shape=[jax.ShapeDtypeStruct((t, HEADS * QK_PAD), BF16),
                   jax.ShapeDtypeStruct((t, SB_WIDTH), BF16)],
        compiler_params=_params("parallel"),
        name="mla_kv_proj",
    )(ckv, kr, w)


def _mla_kvt_kernel(ckv_ref, kr_ref, wk_ref, wvt_ref, k_ref, vt_ref):
    ckv = ckv_ref[...]
    acc = _dot(ckv, wk_ref[...])
    kr = kr_ref[...]
    zero = jnp.zeros((kr.shape[0], QK_PAD - LANES - ROPE_DIM), BF16)
    for h in range(HEADS):
        b = h * QK_PAD
        k_ref[:, b:b + LANES] = acc[:, h * LANES:(h + 1) * LANES].astype(BF16)
        k_ref[:, b + LANES:b + LANES + ROPE_DIM] = kr
        k_ref[:, b + LANES + ROPE_DIM:b + QK_PAD] = zero
    vt_ref[...] = _dot_nt(wvt_ref[...], ckv).astype(BF16)


def _mla_kvt(ckv, kr, wk, wvt, tm):
    t = ckv.shape[0]
    row = lambda i: (i, 0)
    const = lambda i: (0, 0)
    return pl.pallas_call(
        _mla_kvt_kernel,
        grid=(t // tm,),
        in_specs=[pl.BlockSpec((tm, KV_LORA), row), pl.BlockSpec((tm, ROPE_DIM), row),
                  pl.BlockSpec(wk.shape, const), pl.BlockSpec(wvt.shape, const)],
        out_specs=[pl.BlockSpec((tm, HEADS * QK_PAD), row),
                   pl.BlockSpec((None, SB_WIDTH, tm), lambda i: (i, 0, 0))],
        out_shape=[jax.ShapeDtypeStruct((t, HEADS * QK_PAD), BF16),
                   jax.ShapeDtypeStruct((t // tm, SB_WIDTH, tm), BF16)],
        compiler_params=_params("parallel"),
        name="mla_kv_proj_t",
    )(ckv, kr, wk, wvt)


def _gate_kernel(h_ref, w_ref, o_ref):
    o_ref[...] = jax.nn.sigmoid(_dot(h_ref[...], w_ref[...])).astype(BF16)


def _gates(h, w, tm, tn):
    t, d = h.shape
    n = w.shape[1]
    return pl.pallas_call(
        _gate_kernel,
        grid=(t // tm, n // tn),
        in_specs=[pl.BlockSpec((tm, d), lambda i, j: (i, 0)),
                  pl.BlockSpec((d, tn), lambda i, j: (0, j))],
        out_specs=pl.BlockSpec((tm, tn), lambda i, j: (i, j)),
        out_shape=jax.ShapeDtypeStruct((t, n), BF16),
        compiler_params=_params("parallel", "arbitrary"),
        name="branch_gates",
    )(h, w)


def _merge_kernel(osb_ref, omla_ref, wb_ref, g0_ref, g1_ref, o_ref):
    m = (g0_ref[...].astype(F32) * _dot(osb_ref[...], wb_ref[0])
         + g1_ref[...].astype(F32) * _dot(omla_ref[...], wb_ref[1]))
    o_ref[...] = m.astype(BF16)


def _merge(osb, omla, wb, gates, tm, tn):
    t = osb.shape[0]
    nj = D_MODEL // tn
    row = lambda i, j: (i, 0)
    return pl.pallas_call(
        _merge_kernel,
        grid=(t // tm, nj),
        in_specs=[pl.BlockSpec((tm, SB_WIDTH), row), pl.BlockSpec((tm, SB_WIDTH), row),
                  pl.BlockSpec((2, SB_WIDTH, tn), lambda i, j: (0, 0, j)),
                  pl.BlockSpec((tm, tn), lambda i, j: (i, j)),
                  pl.BlockSpec((tm, tn), lambda i, j: (i, j + nj))],
        out_specs=pl.BlockSpec((tm, tn), lambda i, j: (i, j)),
        out_shape=jax.ShapeDtypeStruct((t, D_MODEL), BF16),
        compiler_params=_params("parallel", "arbitrary"),
        name="branch_merge",
    )(osb, omla, wb, gates, gates)


def _out_proj_kernel(m_ref, w_ref, x_ref, g_ref, o_ref):
    o_ref[...] = x_ref[...] + _rms(_dot(m_ref[...], w_ref[...]), g_ref[...])


def _out_proj(m, w, x, g, tm):
    t, d = x.shape
    row = lambda i: (i, 0)
    const = lambda i: (0, 0)
    return pl.pallas_call(
        _out_proj_kernel,
        grid=(t // tm,),
        in_specs=[pl.BlockSpec((tm, d), row), pl.BlockSpec((d, d), const),
                  pl.BlockSpec((tm, d), row), pl.BlockSpec((1, d), const)],
        out_specs=pl.BlockSpec((tm, d), row),
        out_shape=jax.ShapeDtypeStruct((t, d), F32),
        compiler_params=_params("parallel"),
        name="mix_out_proj",
    )(m, w, x, g)


def _ffn_kernel(x_ref, gpre_ref, wu_ref, wd_ref, gpost_ref, o_ref, h_scr, acc_scr):
    j = pl.program_id(1)

    @pl.when(j == 0)
    def _():
        h_scr[...] = _rms(x_ref[...], gpre_ref[...]).astype(BF16)
        acc_scr[...] = jnp.zeros_like(acc_scr)

    u = jnp.maximum(_dot(h_scr[...], wu_ref[...]), 0.0)
    acc_scr[...] += _dot((u * u).astype(BF16), wd_ref[...])

    @pl.when(j == pl.num_programs(1) - 1)
    def _():
        o_ref[...] = x_ref[...] + _rms(acc_scr[...], gpost_ref[...])


def _ffn(x, gpre, wu, wd, gpost, tm, tf):
    t, d = x.shape
    row = lambda i, j: (i, 0)
    const = lambda i, j: (0, 0)
    return pl.pallas_call(
        _ffn_kernel,
        grid=(t // tm, D_FF // tf),
        in_specs=[pl.BlockSpec((tm, d), row), pl.BlockSpec((1, d), const),
                  pl.BlockSpec((d, tf), lambda i, j: (0, j)),
                  pl.BlockSpec((tf, d), lambda i, j: (j, 0)),
                  pl.BlockSpec((1, d), const)],
        out_specs=pl.BlockSpec((tm, d), row),
        out_shape=jax.ShapeDtypeStruct((t, d), F32),
        scratch_shapes=[pltpu.VMEM((tm, d), BF16), pltpu.VMEM((tm, d), F32)],
        compiler_params=_params("parallel", "arbitrary"),
        name="sqrelu_ffn",
    )(x, gpre, wu, wd, gpost)


def _ple_kernel(x_ref, p_ref, gg_ref, wg_ref, wp_ref, gpost_ref, o_ref):
    x = x_ref[...]
    gate = jax.nn.sigmoid(_dot(_rms(x, gg_ref[...]).astype(BF16), wg_ref[...]))
    pe = _dot(p_ref[...].astype(BF16), wp_ref[...])
    o_ref[...] = x + _rms(pe * gate, gpost_ref[...])


def _ple(x, p, gg, wg, wp, gpost, tm):
    t, d = x.shape
    row = lambda i: (i, 0)
    const = lambda i: (0, 0)
    return pl.pallas_call(
        _ple_kernel,
        grid=(t // tm,),
        in_specs=[pl.BlockSpec((tm, d), row), pl.BlockSpec((tm, PLE_DIM), row),
                  pl.BlockSpec((1, d), const), pl.BlockSpec((d, d), const),
                  pl.BlockSpec((PLE_DIM, d), const), pl.BlockSpec((1, d), const)],
        out_specs=pl.BlockSpec((tm, d), row),
        out_shape=jax.ShapeDtypeStruct((t, d), F32),
        compiler_params=_params("parallel"),
        name="ple_embed",
    )(x, p, gg, wg, wp, gpost)


def _sb_step(qs, ks, vs, tri, carry, mask):
    rows = qs[0].shape[0]
    z = jnp.concatenate([_dot_nt(q, k) for q, k in zip(qs, ks)], axis=0)
    sp = jnp.maximum(z, 0.0) + jnp.log(1.0 + jnp.exp(-jnp.abs(z)))
    spm = sp if mask is None else jnp.where(mask, sp, 0.0)
    hi = spm.astype(BF16)
    lo = (spm - hi.astype(F32)).astype(BF16)
    cs = _dot(jnp.concatenate([hi, lo], axis=1), tri)
    after = cs[:, :KEY_BLOCK] if carry is None else cs[:, :KEY_BLOCK] + carry
    w = jnp.exp((z - sp) + after)
    if mask is not None:
        w = jnp.where(mask, w, 0.0)
    w = w.astype(BF16)
    pvs = [_dot(w[i * rows:(i + 1) * rows], v) for i, v in enumerate(vs)]
    carry = cs[:, KEY_BLOCK:] if carry is None else carry + cs[:, KEY_BLOCK:]
    return carry, pvs


def _sb_walk(first_blocks, later_blocks, tri_ref, carry_scr, acc_scr, o_ref, alongside=None):
    def run(chains, carry, mask, first):
        carry, pvs = _sb_step([c[2] for c in chains], [c[3] for c in chains], [c[4] for c in chains],
                              tri_ref[...], carry, mask)
        carry_scr[...] = carry
        for (rows, cols, _, _, _), pv in zip(chains, pvs):
            if first:
                acc_scr[rows, cols] = pv
            else:
                acc_scr[rows, cols] += pv
        return jnp.max(carry)

    def step0():
        chains, mask = first_blocks()
        return run(chains, None, mask, True)

    def step(j):
        chains, valid = later_blocks(j)
        rows = chains[0][2].shape[0]
        carry = jnp.concatenate(
            [jnp.where(ok, carry_scr[i * rows:(i + 1) * rows, :], MASKED) for i, ok in enumerate(valid)],
            axis=0)
        return run(chains, carry, None, False)

    if alongside is None:
        start = (jnp.int32(1), step0())
    else:
        finish = alongside()
        step0()
        start = (jnp.int32(2), step(1))
        finish()
    lax.while_loop(lambda st: st[1] > SB_EXIT, lambda st: (st[0] + 1, step(st[0])), start)
    o_ref[...] = acc_scr[...].astype(BF16)


def _head_cols(h):
    return slice(h * HEAD_DIM, (h + 1) * HEAD_DIM)


def _sb_prompt_kernel(q_ref, k_ref, v_ref, tri_ref, h_ref, wg_ref, o_ref, gate_ref, carry_scr, acc_scr,
                      *, rsub):
    bk = KEY_BLOCK
    base = pl.program_id(1) * rsub
    n_chain = rsub * HEADS
    r = lax.broadcasted_iota(jnp.int32, (n_chain * bk, bk), 0)
    c = lax.broadcasted_iota(jnp.int32, (n_chain * bk, bk), 1)
    causal = c < jnp.bitwise_and(r, bk - 1)

    def blocks(j):
        chains, valid = [], []
        for s in range(rsub):
            kb = base + s - j
            start = pl.multiple_of(jnp.maximum(kb, 0) * bk, bk)
            rows = slice(s * bk, (s + 1) * bk)
            for h in range(HEADS):
                cols = _head_cols(h)
                chains.append((rows, cols, q_ref[rows, cols],
                               k_ref[pl.ds(start, bk), cols], v_ref[pl.ds(start, bk), cols]))
                valid.append(kb >= 0)
        return chains, valid

    def gates():
        logits = _dot(h_ref[...], wg_ref[...])

        def finish():
            gate_ref[...] = jax.nn.sigmoid(logits).astype(BF16)
        return finish

    _sb_walk(lambda: (blocks(0)[0], causal), blocks, tri_ref, carry_scr, acc_scr, o_ref, alongside=gates)


GATE_TILE = 1024


def _sb_prompt_gates(qkv, tri, h, wg):
    b, n, _ = qkv.shape
    t, d = h.shape
    gw = wg.shape[1]
    bq = GATE_TILE * GATE_TILE // gw
    rsub = bq // KEY_BLOCK
    assert bq % KEY_BLOCK == 0 and n % bq == 0 and t % GATE_TILE == 0 and gw % GATE_TILE == 0
    nq = n // bq
    ncol = gw // GATE_TILE
    full = (None, n, SB_WIDTH)
    once = pl.Buffered(1)
    return pl.pallas_call(
        functools.partial(_sb_prompt_kernel, rsub=rsub),
        grid=(b, nq),
        in_specs=[pl.BlockSpec((None, bq, SB_WIDTH), lambda i, j: (i, j, 0)),
                  pl.BlockSpec(full, lambda i, j: (i, 0, 1), pipeline_mode=once),
                  pl.BlockSpec(full, lambda i, j: (i, 0, 2), pipeline_mode=once),
                  pl.BlockSpec(tri.shape, lambda i, j: (0, 0)),
                  pl.BlockSpec((GATE_TILE, d), lambda i, j: ((i * nq + j) // ncol, 0)),
                  pl.BlockSpec((d, GATE_TILE), lambda i, j: (0, (i * nq + j) % ncol))],
        out_specs=[pl.BlockSpec((None, bq, SB_WIDTH), lambda i, j: (i, j, 0)),
                   pl.BlockSpec((GATE_TILE, GATE_TILE), lambda i, j: ((i * nq + j) // ncol, (i * nq + j) % ncol))],
        out_shape=[jax.ShapeDtypeStruct((b, n, SB_WIDTH), BF16),
                   jax.ShapeDtypeStruct((t, gw), BF16)],
        scratch_shapes=[pltpu.VMEM((rsub * HEADS * KEY_BLOCK, KEY_BLOCK), F32),
                        pltpu.VMEM((bq, SB_WIDTH), F32)],
        compiler_params=_params("arbitrary", "arbitrary"),
        name="stickbreak_attn_gates",
    )(qkv, qkv, qkv, tri, h, wg)


def _sb_decode_kernel(q_ref, kn_ref, vn_ref, kp_ref, vp_ref, tri_ref, o_ref, carry_scr, acc_scr,
                      *, n, past):
    bk = KEY_BLOCK
    nb = past // bk
    assert n & (n - 1) == 0
    r = lax.broadcasted_iota(jnp.int32, (HEADS * n, bk), 0)
    c = lax.broadcasted_iota(jnp.int32, (HEADS * n, bk), 1)
    causal = c < jnp.bitwise_and(r, n - 1)
    zpad = jnp.zeros((bk - n, HEAD_DIM), BF16)
    rows = slice(0, n)

    def first():
        return [(rows, _head_cols(h), q_ref[:, _head_cols(h)],
                 jnp.concatenate([kn_ref[:, _head_cols(h)], zpad], axis=0),
                 jnp.concatenate([vn_ref[:, _head_cols(h)], zpad], axis=0))
                for h in range(HEADS)], causal

    def later(j):
        kb = nb - j
        start = jnp.maximum(kb, 0) * (bk * HEADS)
        chains = []
        for h in range(HEADS):
            sl = pl.ds(start + h, bk, stride=HEADS)
            chains.append((rows, _head_cols(h), q_ref[:, _head_cols(h)],
                           kp_ref[sl, :].astype(BF16), vp_ref[sl, :].astype(BF16)))
        return chains, [kb >= 0] * HEADS

    _sb_walk(first, later, tri_ref, carry_scr, acc_scr, o_ref)


def _sb_decode(qkv, k_past, v_past, tri):
    b, n, _ = qkv.shape
    past = k_past.shape[1]
    assert past % KEY_BLOCK == 0 and n <= KEY_BLOCK and n % 16 == 0
    new = (None, n, SB_WIDTH)
    old = (None, past * HEADS, HEAD_DIM)
    return pl.pallas_call(
        functools.partial(_sb_decode_kernel, n=n, past=past),
        grid=(b,),
        in_specs=[pl.BlockSpec(new, lambda i: (i, 0, 0)),
                  pl.BlockSpec(new, lambda i: (i, 0, 1)),
                  pl.BlockSpec(new, lambda i: (i, 0, 2)),
                  pl.BlockSpec(old, lambda i: (i, 0, 0)),
                  pl.BlockSpec(old, lambda i: (i, 0, 0)),
                  pl.BlockSpec(tri.shape, lambda i: (0, 0))],
        out_specs=pl.BlockSpec(new, lambda i: (i, 0, 0)),
        out_shape=jax.ShapeDtypeStruct((b, n, SB_WIDTH), BF16),
        scratch_shapes=[pltpu.VMEM((HEADS * n, KEY_BLOCK), F32),
                        pltpu.VMEM((n, SB_WIDTH), F32)],
        compiler_params=_params("parallel"),
        name="stickbreak_attn_decode",
    )(qkv, qkv, qkv, k_past.reshape(b, past * HEADS, HEAD_DIM),
      v_past.reshape(b, past * HEADS, HEAD_DIM), tri)


CHUNK_SHIFT = CHUNK.bit_length() - 1
assert 1 << CHUNK_SHIFT == CHUNK


def _chunk(pos):
    return lax.shift_right_arithmetic(pos, CHUNK_SHIFT)


def _mla_prompt_kernel(qt_ref, k_ref, vt_ref, o_ref, s_scr, *, bk):
    bq = 2 * bk
    qi = pl.program_id(2)

    def scores(kb, slot):
        s0 = pl.multiple_of(kb * bk, bk)
        s_scr[slot] = _dot(k_ref[pl.ds(s0, bk), :], qt_ref[...])

    def absorb(st, s, kb):
        m, l, acc = st
        m_new = jnp.maximum(m, jnp.max(s, axis=0, keepdims=True))
        alpha = jnp.exp2(m - m_new)
        p = jnp.exp2(s - m_new)
        l = alpha * l + jnp.sum(p, axis=0, keepdims=True)
        acc = alpha * acc + _dot(vt_ref[kb], p.astype(BF16))
        return m_new, l, acc

    def pair(i, st):
        kb = 2 * i
        scores(kb + 1, 1)
        st = absorb(st, s_scr[0], kb)
        scores(kb + 2, 0)
        return absorb(st, s_scr[1], kb + 1)

    init = (jnp.full((1, bq), MASKED, F32), jnp.zeros((1, bq), F32), jnp.zeros((HEAD_DIM, bq), F32))
    scores(0, 0)
    st = lax.fori_loop(0, qi, pair, init)
    late = (slice(None), slice(bk, bq))
    s_late = _dot(k_ref[pl.ds(pl.multiple_of((2 * qi + 1) * bk, bk), bk), :], qt_ref[late])
    vis = (_chunk(lax.broadcasted_iota(jnp.int32, (bk, bq), 0))
           <= _chunk(lax.broadcasted_iota(jnp.int32, (bk, bq), 1)))
    m, l, acc = absorb(st, jnp.where(vis, s_scr[0], MASKED), 2 * qi)
    _, l2, acc2 = absorb((m[late], l[late], acc[late]), jnp.where(vis[:, :bk], s_late, MASKED), 2 * qi + 1)
    l = jnp.concatenate([l[:, :bk], l2], axis=1)
    acc = jnp.concatenate([acc[:, :bk], acc2], axis=1)
    o_ref[...] = (acc / l).T.astype(BF16)


def _mla_prompt(qt, k, vt, b):
    n = k.shape[1]
    bk = vt.shape[-1]
    bq = 2 * bk
    assert bk % CHUNK == 0 and n % bq == 0
    nq = n // bq
    return pl.pallas_call(
        functools.partial(_mla_prompt_kernel, bk=bk),
        grid=(b, HEADS, nq),
        in_specs=[pl.BlockSpec((QK_PAD, bq), lambda i, h, j: (h, i * nq + j)),
                  pl.BlockSpec((None, n, QK_PAD), lambda i, h, j: (i, 0, h)),
                  pl.BlockSpec((None, n // bk, HEAD_DIM, bk), lambda i, h, j: (i, 0, h, 0))],
        out_specs=pl.BlockSpec((None, bq, HEAD_DIM), lambda i, h, j: (i, j, h)),
        out_shape=jax.ShapeDtypeStruct((b, n, SB_WIDTH), BF16),
        scratch_shapes=[pltpu.VMEM((2, bk, bq), F32)],
        compiler_params=_params("parallel", "parallel", "arbitrary"),
        name="mla_attn",
    )(qt, k, vt)


def _mla_decode_kernel(q_ref, kp_ref, vp_ref, kn_ref, vn_ref, o_ref, *, n, past):
    r = lax.broadcasted_iota(jnp.int32, (n, LANES), 0)
    c = lax.broadcasted_iota(jnp.int32, (n, LANES), 1)
    vis = jnp.logical_and(c < n, _chunk(past + c) <= _chunk(past + r))
    kpad = jnp.zeros((LANES - n, QK_PAD), BF16)
    vpad = jnp.zeros((LANES - n, HEAD_DIM), BF16)
    for h in range(HEADS):
        qk = slice(h * QK_PAD, (h + 1) * QK_PAD)
        q = q_ref[:, qk]
        s_old = _dot_nt(q, kp_ref[:, qk])
        s_new = jnp.where(vis, _dot_nt(q, jnp.concatenate([kn_ref[:, qk], kpad], axis=0)), MASKED)
        m = jnp.maximum(jnp.max(s_old, axis=-1, keepdims=True), jnp.max(s_new, axis=-1, keepdims=True))
        p_old = jnp.exp(s_old - m)
        p_new = jnp.exp(s_new - m)
        l = jnp.sum(p_old, axis=-1, keepdims=True) + jnp.sum(p_new, axis=-1, keepdims=True)
        acc = (_dot(p_old.astype(BF16), vp_ref[:, _head_cols(h)])
               + _dot(p_new.astype(BF16), jnp.concatenate([vn_ref[:, _head_cols(h)], vpad], axis=0)))
        o_ref[:, _head_cols(h)] = (acc / l).astype(BF16)


def _mla_decode(q, k_past, v_past, k_new, v_new):
    b, n, _ = q.shape
    past = k_past.shape[1]
    assert n <= LANES and n % 16 == 0
    spec = lambda rows, width: pl.BlockSpec((None, rows, width), lambda i: (i, 0, 0))
    return pl.pallas_call(
        functools.partial(_mla_decode_kernel, n=n, past=past),
        grid=(b,),
        in_specs=[spec(n, HEADS * QK_PAD), spec(past, HEADS * QK_PAD), spec(past, SB_WIDTH),
                  spec(n, HEADS * QK_PAD), spec(n, SB_WIDTH)],
        out_specs=spec(n, SB_WIDTH),
        out_shape=jax.ShapeDtypeStruct((b, n, SB_WIDTH), BF16),
        compiler_params=_params("parallel"),
        name="mla_attn_decode",
    )(q, k_past, v_past, k_new, v_new)


def _rope_tables(pos):
    half = ROPE_DIM // 2
    freqs = ROPE_THETA ** (-jnp.arange(half, dtype=F32) / half)
    ang = pos.astype(F32)[:, None] * freqs[None, :]
    cos, sin = jnp.cos(ang), jnp.sin(ang)
    z = jnp.zeros((pos.shape[0], LANES - ROPE_DIM), F32)
    return jnp.concatenate([cos, cos, z], axis=1), jnp.concatenate([-sin, sin, z], axis=1)


def _swap_halves(w):
    half = w.shape[-1] // 2
    return jnp.concatenate([w[..., half:], w[..., :half]], axis=-1)


def _prep_weights(w_in, w_uq, w_uk, w_uv, w_branch, w_out, w_up, w_down, w_ple_gate, w_ple):
    d = w_in.shape[0]
    o = 3 * SB_WIDTH
    w_kr = w_in[:, o + Q_LORA + KV_LORA:o + Q_LORA + KV_LORA + ROPE_DIM]
    z64 = jnp.zeros((d, LANES - ROPE_DIM), w_in.dtype)
    w_lat = jnp.concatenate([w_in[:, o:o + Q_LORA + KV_LORA], w_kr, z64, _swap_halves(w_kr), z64], axis=1)
    wq3 = w_uq.reshape(Q_LORA, HEADS, HEAD_DIM + ROPE_DIM)
    rp = wq3[:, :, HEAD_DIM:]
    zq = jnp.zeros((Q_LORA, HEADS, LANES - ROPE_DIM), w_uq.dtype)
    w_q = jnp.concatenate([
        jnp.concatenate([wq3[:, :, :HEAD_DIM], rp, zq], axis=-1).reshape(Q_LORA, HEADS * QK_PAD),
        jnp.concatenate([_swap_halves(rp), zq], axis=-1).reshape(Q_LORA, HEADS * LANES)], axis=1)
    return dict(
        qkv=w_in.astype(BF16),
        lat=w_lat.astype(BF16),
        gate=w_in[:, o + Q_LORA + KV_LORA + ROPE_DIM:].astype(BF16),
        q=w_q.astype(BF16),
        qt=w_q[:, :HEADS * QK_PAD].T.astype(BF16),
        kv=jnp.concatenate([w_uk, w_uv], axis=1).astype(BF16),
        uk=w_uk.astype(BF16),
        uvt=w_uv.T.astype(BF16),
        branch=w_branch.astype(BF16),
        out=w_out.astype(BF16),
        up=w_up.astype(BF16),
        down=w_down.astype(BF16),
        ple_gate=w_ple_gate.astype(BF16),
        ple=w_ple.astype(BF16),
    )


def _tile(t, want):
    return want if t % want == 0 else t


def _layer(x, ple, past, w, g, tri):
    b, n, d = x.shape
    t = b * n
    xf = x.reshape(t, d)
    tm = _tile(t, 1024)
    tm_s = _tile(t, 512)
    past_len = 0 if past is None else past[0].shape[1]
    if n >= tm:
        pos = jnp.arange(n) + past_len
    else:
        pos = jnp.tile(jnp.arange(n) + past_len, tm // n)
    cos2, sin2 = _rope_tables(pos)

    h, qkv_b, k_f, v_f = _qkv(xf, g['mix_pre'], w['qkv'], tm_s)
    cq, ckv_f, ckv_b, kr_f, kr_b = _latent(h, w['lat'], g['q'], g['kv'], cos2, sin2, tm)
    qkv_b = qkv_b.reshape(b, n, -1)

    if past is None:
        blk = _tile(n, 512)
        qt = _mla_qt(cq, w['qt'], cos2[:, :ROPE_DIM // 2].T, sin2[:, ROPE_DIM // 2:ROPE_DIM].T, blk)
        k_mla, vt = _mla_kvt(ckv_b, kr_b, w['uk'], w['uvt'], blk)
        o_sb, gates = _sb_prompt_gates(qkv_b, tri, h, w['gate'])
        o_mla = _mla_prompt(qt, k_mla.reshape(b, n, -1), vt.reshape(b, n // blk, SB_WIDTH, blk), b)
    else:
        gates = _gates(h, w['gate'], tm, 1024)
        q_mla = _mla_q(cq, w['q'], cos2, sin2, tm_s).reshape(b, n, -1)
        k_mla, v_mla = _mla_kv(ckv_b, kr_b, w['kv'], tm)
        o_sb = _sb_decode(qkv_b, past[0], past[1], tri)
        k_old, v_old = _mla_kv(past[2].reshape(b * past_len, -1), past[3].reshape(b * past_len, -1),
                               w['kv'], _tile(b * past_len, 1024))
        o_mla = _mla_decode(q_mla, k_old.reshape(b, past_len, -1), v_old.reshape(b, past_len, -1),
                            k_mla.reshape(b, n, -1), v_mla.reshape(b, n, -1))

    merged = _merge(o_sb.reshape(t, -1), o_mla.reshape(t, -1), w['branch'], gates, tm, 1024)
    x1 = _out_proj(merged, w['out'], xf, g['mix_post'], tm_s)
    x2 = _ffn(x1, g['ffn_pre'], w['up'], w['down'], g['ffn_post'], tm_s, 1024)
    x3 = _ple(x2, ple.reshape(t, -1), g['ple_gate'], w['ple_gate'], w['ple'], g['ple_post'], tm_s)
    state = (k_f.reshape(1, b, n, HEADS, HEAD_DIM), v_f.reshape(1, b, n, HEADS, HEAD_DIM),
             ckv_f.reshape(1, b, n, KV_LORA), kr_f.reshape(1, b, n, ROPE_DIM))
    return x3.reshape(b, n, d), state


def kernel(x_prompt, x_sample, cache_sb_k, cache_sb_v, cache_mla_ckv, cache_mla_krope, p_prompt, p_sample, g_mix_pre, w_in, g_q, w_uq, g_kv, w_uk, w_uv, w_branch, w_out, g_mix_post, g_ffn_pre, w_up, w_down, g_ffn_post, g_ple_gate, w_ple_gate, w_ple, g_ple_post):
    assert w_in.shape[0] == 1, "single layer"
    w = _prep_weights(w_in[0], w_uq[0], w_uk[0], w_uv[0], w_branch[0], w_out[0], w_up[0], w_down[0],
                      w_ple_gate[0], w_ple[0])
    g = dict(mix_pre=g_mix_pre, q=g_q, kv=g_kv, mix_post=g_mix_post, ffn_pre=g_ffn_pre,
             ffn_post=g_ffn_post, ple_gate=g_ple_gate, ple_post=g_ple_post)
    idx = jnp.arange(KEY_BLOCK)
    lower = -(idx[:, None] > idx[None, :]).astype(BF16)
    tri = jnp.concatenate([lower, jnp.full((KEY_BLOCK, KEY_BLOCK), -1, BF16)], axis=1)
    tri = jnp.concatenate([tri, tri], axis=0)

    yp, sp = _layer(x_prompt, p_prompt[0], None, w, g, tri)
    past = (cache_sb_k[0], cache_sb_v[0], cache_mla_ckv[0], cache_mla_krope[0])
    ys, ss = _layer(x_sample, p_sample[0], past, w, g, tri)
    return (yp, ys) + sp + ss
```

```python
import functools

import jax
import jax.numpy as jnp
from jax import lax
from jax.experimental import pallas as pl
from jax.experimental.pallas import tpu as pltpu

F32 = jnp.float32
BF16 = jnp.bfloat16

D_MODEL = 2048
CHUNK = 64
PLE_DIM = 256
HEADS = 8
HEAD_DIM = 128
SB_WIDTH = HEADS * HEAD_DIM
ROPE_DIM = 64
Q_LORA = 512
KV_LORA = 512
QK_PAD = 256
D_FF = 4 * D_MODEL
ROPE_THETA = 10000.0
EPS = 1e-6
SB_SCALE = HEAD_DIM ** -0.5
MLA_SCALE = (HEAD_DIM + ROPE_DIM) ** -0.5
LOG2E = 1.4426950408889634

LANES = 128
KEY_BLOCK = 128
SB_EXIT = -104.0
MASKED = -1e30
VMEM_LIMIT = 56 * 1024 * 1024


def _params(*sem):
    return pltpu.CompilerParams(dimension_semantics=sem, vmem_limit_bytes=VMEM_LIMIT)


def _rms(xf, g):
    ms = jnp.mean(xf * xf, axis=-1, keepdims=True)
    return xf * lax.rsqrt(ms + EPS) * g


def _dot(a, b):
    return jnp.dot(a, b, preferred_element_type=F32)


def _dot_nt(a, b):
    return lax.dot_general(a, b, (((1,), (1,)), ((), ())), preferred_element_type=F32)


def _qkv_kernel(x_ref, g_ref, w_ref, h_ref, qkv_ref, kf_ref, vf_ref, h_scr):
    j = pl.program_id(1)

    @pl.when(j == 0)
    def _():
        h = _rms(x_ref[...], g_ref[...]).astype(BF16)
        h_scr[...] = h
        h_ref[...] = h

    acc = _dot(h_scr[...], w_ref[...])
    qkv_ref[...] = (acc * jnp.where(j == 0, SB_SCALE, 1.0)).astype(BF16)

    def store_state(ref):
        for h in range(HEADS):
            ref[pl.ds(h, acc.shape[0], stride=HEADS), :] = acc[:, _head_cols(h)]

    pl.when(j == 1)(lambda: store_state(kf_ref))
    pl.when(j == 2)(lambda: store_state(vf_ref))


def _qkv(x, g, w, tm):
    t, d = x.shape
    row = lambda i, j: (i, 0)
    state = pl.BlockSpec((tm * HEADS, HEAD_DIM), row)
    return pl.pallas_call(
        _qkv_kernel,
        grid=(t // tm, 3),
        in_specs=[pl.BlockSpec((tm, d), row), pl.BlockSpec((1, d), lambda i, j: (0, 0)),
                  pl.BlockSpec((d, SB_WIDTH), lambda i, j: (0, j))],
        out_specs=[pl.BlockSpec((tm, d), row), pl.BlockSpec((tm, SB_WIDTH), lambda i, j: (i, j)),
                   state, state],
        out_shape=[jax.ShapeDtypeStruct((t, d), BF16),
                   jax.ShapeDtypeStruct((t, 3 * SB_WIDTH), BF16),
                   jax.ShapeDtypeStruct((t * HEADS, HEAD_DIM), F32),
                   jax.ShapeDtypeStruct((t * HEADS, HEAD_DIM), F32)],
        scratch_shapes=[pltpu.VMEM((tm, d), BF16)],
        compiler_params=_params("parallel", "arbitrary"),
        name="norm_qkv_proj",
    )(x, g, w)


def _latent_kernel(h_ref, w_ref, gq_ref, gkv_ref, cos_ref, sin_ref,
                   cq_ref, ckvf_ref, ckvb_ref, krf_ref, krb_ref):
    acc = _dot(h_ref[...], w_ref[...])
    cq_ref[...] = _rms(acc[:, :Q_LORA], gq_ref[...]).astype(BF16)
    ckv = _rms(acc[:, Q_LORA:Q_LORA + KV_LORA], gkv_ref[...])
    ckvf_ref[...] = ckv
    ckvb_ref[...] = ckv.astype(BF16)
    base = Q_LORA + KV_LORA
    kr = acc[:, base:base + LANES] * cos_ref[...] + acc[:, base + LANES:base + 2 * LANES] * sin_ref[...]
    krf_ref[...] = kr[:, :ROPE_DIM]
    krb_ref[...] = kr[:, :ROPE_DIM].astype(BF16)


def _latent(h, w, gq, gkv, cos2, sin2, tm):
    t, d = h.shape
    n_tab = cos2.shape[0] // tm
    row = lambda i: (i, 0)
    const = lambda i: (0, 0)
    tab = lambda i: (i % n_tab, 0)
    wn = w.shape[1]
    return pl.pallas_call(
        _latent_kernel,
        grid=(t // tm,),
        in_specs=[pl.BlockSpec((tm, d), row), pl.BlockSpec((d, wn), const),
                  pl.BlockSpec((1, Q_LORA), const), pl.BlockSpec((1, KV_LORA), const),
                  pl.BlockSpec((tm, LANES), tab), pl.BlockSpec((tm, LANES), tab)],
        out_specs=[pl.BlockSpec((tm, Q_LORA), row), pl.BlockSpec((tm, KV_LORA), row),
                   pl.BlockSpec((tm, KV_LORA), row), pl.BlockSpec((tm, ROPE_DIM), row),
                   pl.BlockSpec((tm, ROPE_DIM), row)],
        out_shape=[jax.ShapeDtypeStruct((t, Q_LORA), BF16),
                   jax.ShapeDtypeStruct((t, KV_LORA), F32),
                   jax.ShapeDtypeStruct((t, KV_LORA), BF16),
                   jax.ShapeDtypeStruct((t, ROPE_DIM), F32),
                   jax.ShapeDtypeStruct((t, ROPE_DIM), BF16)],
        compiler_params=_params("parallel"),
        name="mla_latent_proj",
    )(h, w, gq, gkv, cos2, sin2)


def _mla_q_kernel(cq_ref, w_ref, cos_ref, sin_ref, q_ref):
    acc = _dot(cq_ref[...], w_ref[...])
    cos, sin = cos_ref[...], sin_ref[...]
    for h in range(HEADS):
        b = h * QK_PAD
        q_ref[:, b:b + LANES] = (acc[:, b:b + LANES] * MLA_SCALE).astype(BF16)
        sw = HEADS * QK_PAD + h * LANES
        rot = acc[:, b + LANES:b + QK_PAD] * cos + acc[:, sw:sw + LANES] * sin
        q_ref[:, b + LANES:b + QK_PAD] = (rot * MLA_SCALE).astype(BF16)


def _mla_q(cq, w, cos2, sin2, tm):
    t = cq.shape[0]
    n_tab = cos2.shape[0] // tm
    row = lambda i: (i, 0)
    tab = lambda i: (i % n_tab, 0)
    return pl.pallas_call(
        _mla_q_kernel,
        grid=(t // tm,),
        in_specs=[pl.BlockSpec((tm, Q_LORA), row),
                  pl.BlockSpec(w.shape, lambda i: (0, 0)),
                  pl.BlockSpec((tm, LANES), tab), pl.BlockSpec((tm, LANES), tab)],
        out_specs=pl.BlockSpec((tm, HEADS * QK_PAD), row),
        out_shape=jax.ShapeDtypeStruct((t, HEADS * QK_PAD), BF16),
        compiler_params=_params("parallel"),
        name="mla_q_proj",
    )(cq, w, cos2, sin2)


def _mla_qt_kernel(cq_ref, wt_ref, cos_ref, sin_ref, qt_ref):
    acc = _dot_nt(wt_ref[...], cq_ref[...])
    cos, sin = cos_ref[...], sin_ref[...]
    half = ROPE_DIM // 2
    scale = MLA_SCALE * LOG2E
    for h in range(HEADS):
        b = h * QK_PAD
        r = b + HEAD_DIM
        qt_ref[b:r, :] = (acc[b:r] * scale).astype(BF16)
        x1, x2 = acc[r:r + half], acc[r + half:r + ROPE_DIM]
        qt_ref[r:r + half, :] = ((x1 * cos - x2 * sin) * scale).astype(BF16)
        qt_ref[r + half:r + ROPE_DIM, :] = ((x1 * sin + x2 * cos) * scale).astype(BF16)
        qt_ref[r + ROPE_DIM:b + QK_PAD, :] = jnp.zeros((QK_PAD - HEAD_DIM - ROPE_DIM, acc.shape[1]), BF16)


def _mla_qt(cq, wt, cos_t, sin_t, tm):
    t = cq.shape[0]
    n_tab = cos_t.shape[1] // tm
    tab = lambda i: (0, i % n_tab)
    half = ROPE_DIM // 2
    return pl.pallas_call(
        _mla_qt_kernel,
        grid=(t // tm,),
        in_specs=[pl.BlockSpec((tm, Q_LORA), lambda i: (i, 0)),
                  pl.BlockSpec(wt.shape, lambda i: (0, 0)),
                  pl.BlockSpec((half, tm), tab), pl.BlockSpec((half, tm), tab)],
        out_specs=pl.BlockSpec((HEADS * QK_PAD, tm), lambda i: (0, i)),
        out_shape=jax.ShapeDtypeStruct((HEADS * QK_PAD, t), BF16),
        compiler_params=_params("parallel"),
        name="mla_q_proj_t",
    )(cq, wt, cos_t, sin_t)


def _mla_kvt_kernel(ckv_ref, kr_ref, wk_ref, wvt_ref, k_ref, vt_ref):
    ckv = ckv_ref[...]
    acc = _dot(ckv, wk_ref[...])
    kr = kr_ref[...]
    zero = jnp.zeros((kr.shape[0], QK_PAD - LANES - ROPE_DIM), BF16)
    for h in range(HEADS):
        b = h * QK_PAD
        k_ref[:, b:b + LANES] = acc[:, h * LANES:(h + 1) * LANES].astype(BF16)
        k_ref[:, b + LANES:b + LANES + ROPE_DIM] = kr
        k_ref[:, b + LANES + ROPE_DIM:b + QK_PAD] = zero
    vt_ref[...] = _dot_nt(wvt_ref[...], ckv).astype(BF16)


def _mla_kvt(ckv, kr, wk, wvt, tm):
    t = ckv.shape[0]
    row = lambda i: (i, 0)
    const = lambda i: (0, 0)
    return pl.pallas_call(
        _mla_kvt_kernel,
        grid=(t // tm,),
        in_specs=[pl.BlockSpec((tm, KV_LORA), row), pl.BlockSpec((tm, ROPE_DIM), row),
                  pl.BlockSpec(wk.shape, const), pl.BlockSpec(wvt.shape, const)],
        out_specs=[pl.BlockSpec((tm, HEADS * QK_PAD), row),
                   pl.BlockSpec((None, SB_WIDTH, tm), lambda i: (i, 0, 0))],
        out_shape=[jax.ShapeDtypeStruct((t, HEADS * QK_PAD), BF16),
                   jax.ShapeDtypeStruct((t // tm, SB_WIDTH, tm), BF16)],
        compiler_params=_params("parallel"),
        name="mla_kv_proj_t",
    )(ckv, kr, wk, wvt)


def _gate_kernel(h_ref, w_ref, o_ref):
    o_ref[...] = jax.nn.sigmoid(_dot(h_ref[...], w_ref[...])).astype(BF16)


def _gates(h, w, tm, tn):
    t, d = h.shape
    n = w.shape[1]
    return pl.pallas_call(
        _gate_kernel,
        grid=(t // tm, n // tn),
        in_specs=[pl.BlockSpec((tm, d), lambda i, j: (i, 0)),
                  pl.BlockSpec((d, tn), lambda i, j: (0, j))],
        out_specs=pl.BlockSpec((tm, tn), lambda i, j: (i, j)),
        out_shape=jax.ShapeDtypeStruct((t, n), BF16),
        compiler_params=_params("parallel", "arbitrary"),
        name="branch_gates",
    )(h, w)


def _merge_kernel(osb_ref, omla_ref, wb_ref, g0_ref, g1_ref, o_ref):
    m = (g0_ref[...].astype(F32) * _dot(osb_ref[...], wb_ref[0])
         + g1_ref[...].astype(F32) * _dot(omla_ref[...], wb_ref[1]))
    o_ref[...] = m.astype(BF16)


def _merge(osb, omla, wb, gates, tm, tn):
    t = osb.shape[0]
    nj = D_MODEL // tn
    row = lambda i, j: (i, 0)
    return pl.pallas_call(
        _merge_kernel,
        grid=(t // tm, nj),
        in_specs=[pl.BlockSpec((tm, SB_WIDTH), row), pl.BlockSpec((tm, SB_WIDTH), row),
                  pl.BlockSpec((2, SB_WIDTH, tn), lambda i, j: (0, 0, j)),
                  pl.BlockSpec((tm, tn), lambda i, j: (i, j)),
                  pl.BlockSpec((tm, tn), lambda i, j: (i, j + nj))],
        out_specs=pl.BlockSpec((tm, tn), lambda i, j: (i, j)),
        out_shape=jax.ShapeDtypeStruct((t, D_MODEL), BF16),
        compiler_params=_params("parallel", "arbitrary"),
        name="branch_merge",
    )(osb, omla, wb, gates, gates)


def _out_proj_kernel(m_ref, w_ref, x_ref, g_ref, o_ref):
    o_ref[...] = x_ref[...] + _rms(_dot(m_ref[...], w_ref[...]), g_ref[...])


def _out_proj(m, w, x, g, tm):
    t, d = x.shape
    row = lambda i: (i, 0)
    const = lambda i: (0, 0)
    return pl.pallas_call(
        _out_proj_kernel,
        grid=(t // tm,),
        in_specs=[pl.BlockSpec((tm, d), row), pl.BlockSpec((d, d), const),
                  pl.BlockSpec((tm, d), row), pl.BlockSpec((1, d), const)],
        out_specs=pl.BlockSpec((tm, d), row),
        out_shape=jax.ShapeDtypeStruct((t, d), F32),
        compiler_params=_params("parallel"),
        name="mix_out_proj",
    )(m, w, x, g)


def _ffn_kernel(x_ref, gpre_ref, wu_ref, wd_ref, gpost_ref, o_ref, h_scr, acc_scr):
    j = pl.program_id(1)

    @pl.when(j == 0)
    def _():
        h_scr[...] = _rms(x_ref[...], gpre_ref[...]).astype(BF16)
        acc_scr[...] = jnp.zeros_like(acc_scr)

    u = jnp.maximum(_dot(h_scr[...], wu_ref[...]), 0.0)
    acc_scr[...] += _dot((u * u).astype(BF16), wd_ref[...])

    @pl.when(j == pl.num_programs(1) - 1)
    def _():
        o_ref[...] = x_ref[...] + _rms(acc_scr[...], gpost_ref[...])


def _ffn(x, gpre, wu, wd, gpost, tm, tf):
    t, d = x.shape
    row = lambda i, j: (i, 0)
    const = lambda i, j: (0, 0)
    return pl.pallas_call(
        _ffn_kernel,
        grid=(t // tm, D_FF // tf),
        in_specs=[pl.BlockSpec((tm, d), row), pl.BlockSpec((1, d), const),
                  pl.BlockSpec((d, tf), lambda i, j: (0, j)),
                  pl.BlockSpec((tf, d), lambda i, j: (j, 0)),
                  pl.BlockSpec((1, d), const)],
        out_specs=pl.BlockSpec((tm, d), row),
        out_shape=jax.ShapeDtypeStruct((t, d), F32),
        scratch_shapes=[pltpu.VMEM((tm, d), BF16), pltpu.VMEM((tm, d), F32)],
        compiler_params=_params("parallel", "arbitrary"),
        name="sqrelu_ffn",
    )(x, gpre, wu, wd, gpost)


def _ple_kernel(x_ref, p_ref, gg_ref, wg_ref, wp_ref, gpost_ref, o_ref):
    x = x_ref[...]
    gate = jax.nn.sigmoid(_dot(_rms(x, gg_ref[...]).astype(BF16), wg_ref[...]))
    pe = _dot(p_ref[...].astype(BF16), wp_ref[...])
    o_ref[...] = x + _rms(pe * gate, gpost_ref[...])


def _ple(x, p, gg, wg, wp, gpost, tm):
    t, d = x.shape
    row = lambda i: (i, 0)
    const = lambda i: (0, 0)
    return pl.pallas_call(
        _ple_kernel,
        grid=(t // tm,),
        in_specs=[pl.BlockSpec((tm, d), row), pl.BlockSpec((tm, PLE_DIM), row),
                  pl.BlockSpec((1, d), const), pl.BlockSpec((d, d), const),
                  pl.BlockSpec((PLE_DIM, d), const), pl.BlockSpec((1, d), const)],
        out_specs=pl.BlockSpec((tm, d), row),
        out_shape=jax.ShapeDtypeStruct((t, d), F32),
        compiler_params=_params("parallel"),
        name="ple_embed",
    )(x, p, gg, wg, wp, gpost)


def _sb_step(qs, ks, vs, tri, carry, mask):
    rows = qs[0].shape[0]
    z = jnp.concatenate([_dot_nt(q, k) for q, k in zip(qs, ks)], axis=0)
    sp = jnp.maximum(z, 0.0) + jnp.log(1.0 + jnp.exp(-jnp.abs(z)))
    spm = sp if mask is None else jnp.where(mask, sp, 0.0)
    hi = spm.astype(BF16)
    lo = (spm - hi.astype(F32)).astype(BF16)
    cs = _dot(jnp.concatenate([hi, lo], axis=1), tri)
    after = cs[:, :KEY_BLOCK] if carry is None else cs[:, :KEY_BLOCK] + carry
    w = jnp.exp((z - sp) + after)
    if mask is not None:
        w = jnp.where(mask, w, 0.0)
    w = w.astype(BF16)
    pvs = [_dot(w[i * rows:(i + 1) * rows], v) for i, v in enumerate(vs)]
    carry = cs[:, KEY_BLOCK:] if carry is None else carry + cs[:, KEY_BLOCK:]
    return carry, pvs


def _sb_walk(first_blocks, later_blocks, tri_ref, carry_scr, acc_scr, o_ref, alongside=None):
    def run(chains, carry, mask, first):
        carry, pvs = _sb_step([c[2] for c in chains], [c[3] for c in chains], [c[4] for c in chains],
                              tri_ref[...], carry, mask)
        carry_scr[...] = carry
        for (rows, cols, _, _, _), pv in zip(chains, pvs):
            if first:
                acc_scr[rows, cols] = pv
            else:
                acc_scr[rows, cols] += pv
        return jnp.max(carry)

    def step0():
        chains, mask = first_blocks()
        return run(chains, None, mask, True)

    def step(j):
        chains, valid = later_blocks(j)
        rows = chains[0][2].shape[0]
        carry = jnp.concatenate(
            [jnp.where(ok, carry_scr[i * rows:(i + 1) * rows, :], MASKED) for i, ok in enumerate(valid)],
            axis=0)
        return run(chains, carry, None, False)

    if alongside is None:
        start = (jnp.int32(1), step0())
    else:
        finish = alongside()
        step0()
        start = (jnp.int32(2), step(1))
        finish()
    lax.while_loop(lambda st: st[1] > SB_EXIT, lambda st: (st[0] + 1, step(st[0])), start)
    o_ref[...] = acc_scr[...].astype(BF16)


def _head_cols(h):
    return slice(h * HEAD_DIM, (h + 1) * HEAD_DIM)


def _sb_prompt_kernel(q_ref, k_ref, v_ref, tri_ref, h_ref, wg_ref, o_ref, gate_ref, carry_scr, acc_scr,
                      *, rsub):
    bk = KEY_BLOCK
    base = pl.program_id(1) * rsub
    n_chain = rsub * HEADS
    r = lax.broadcasted_iota(jnp.int32, (n_chain * bk, bk), 0)
    c = lax.broadcasted_iota(jnp.int32, (n_chain * bk, bk), 1)
    causal = c < jnp.bitwise_and(r, bk - 1)

    def blocks(j):
        chains, valid = [], []
        for s in range(rsub):
            kb = base + s - j
            start = pl.multiple_of(jnp.maximum(kb, 0) * bk, bk)
            rows = slice(s * bk, (s + 1) * bk)
            for h in range(HEADS):
                cols = _head_cols(h)
                chains.append((rows, cols, q_ref[rows, cols],
                               k_ref[pl.ds(start, bk), cols], v_ref[pl.ds(start, bk), cols]))
                valid.append(kb >= 0)
        return chains, valid

    def gates():
        logits = _dot(h_ref[...], wg_ref[...])

        def finish():
            gate_ref[...] = jax.nn.sigmoid(logits).astype(BF16)
        return finish

    _sb_walk(lambda: (blocks(0)[0], causal), blocks, tri_ref, carry_scr, acc_scr, o_ref, alongside=gates)


GATE_TILE = 1024


def _sb_prompt_gates(qkv, tri, h, wg):
    b, n, _ = qkv.shape
    t, d = h.shape
    gw = wg.shape[1]
    bq = GATE_TILE * GATE_TILE // gw
    rsub = bq // KEY_BLOCK
    assert bq % KEY_BLOCK == 0 and n % bq == 0 and t % GATE_TILE == 0 and gw % GATE_TILE == 0
    nq = n // bq
    ncol = gw // GATE_TILE
    full = (None, n, SB_WIDTH)
    once = pl.Buffered(1)
    return pl.pallas_call(
        functools.partial(_sb_prompt_kernel, rsub=rsub),
        grid=(b, nq),
        in_specs=[pl.BlockSpec((None, bq, SB_WIDTH), lambda i, j: (i, j, 0)),
                  pl.BlockSpec(full, lambda i, j: (i, 0, 1), pipeline_mode=once),
                  pl.BlockSpec(full, lambda i, j: (i, 0, 2), pipeline_mode=once),
                  pl.BlockSpec(tri.shape, lambda i, j: (0, 0)),
                  pl.BlockSpec((GATE_TILE, d), lambda i, j: ((i * nq + j) // ncol, 0)),
                  pl.BlockSpec((d, GATE_TILE), lambda i, j: (0, (i * nq + j) % ncol))],
        out_specs=[pl.BlockSpec((None, bq, SB_WIDTH), lambda i, j: (i, j, 0)),
                   pl.BlockSpec((GATE_TILE, GATE_TILE), lambda i, j: ((i * nq + j) // ncol, (i * nq + j) % ncol))],
        out_shape=[jax.ShapeDtypeStruct((b, n, SB_WIDTH), BF16),
                   jax.ShapeDtypeStruct((t, gw), BF16)],
        scratch_shapes=[pltpu.VMEM((rsub * HEADS * KEY_BLOCK, KEY_BLOCK), F32),
                        pltpu.VMEM((bq, SB_WIDTH), F32)],
        compiler_params=_params("arbitrary", "arbitrary"),
        name="stickbreak_attn_gates",
    )(qkv, qkv, qkv, tri, h, wg)


def _sb_decode_kernel(q_ref, kn_ref, vn_ref, kp_ref, vp_ref, tri_ref, o_ref, carry_scr, acc_scr,
                      *, n, past):
    bk = KEY_BLOCK
    nb = past // bk
    assert n & (n - 1) == 0
    r = lax.broadcasted_iota(jnp.int32, (HEADS * n, bk), 0)
    c = lax.broadcasted_iota(jnp.int32, (HEADS * n, bk), 1)
    causal = c < jnp.bitwise_and(r, n - 1)
    zpad = jnp.zeros((bk - n, HEAD_DIM), BF16)
    rows = slice(0, n)

    def first():
        return [(rows, _head_cols(h), q_ref[:, _head_cols(h)],
                 jnp.concatenate([kn_ref[:, _head_cols(h)], zpad], axis=0),
                 jnp.concatenate([vn_ref[:, _head_cols(h)], zpad], axis=0))
                for h in range(HEADS)], causal

    def later(j):
        kb = nb - j
        start = jnp.maximum(kb, 0) * (bk * HEADS)
        chains = []
        for h in range(HEADS):
            sl = pl.ds(start + h, bk, stride=HEADS)
            chains.append((rows, _head_cols(h), q_ref[:, _head_cols(h)],
                           kp_ref[sl, :].astype(BF16), vp_ref[sl, :].astype(BF16)))
        return chains, [kb >= 0] * HEADS

    _sb_walk(first, later, tri_ref, carry_scr, acc_scr, o_ref)


def _sb_decode(qkv, k_past, v_past, tri):
    b, n, _ = qkv.shape
    past = k_past.shape[1]
    assert past % KEY_BLOCK == 0 and n <= KEY_BLOCK and n % 16 == 0
    new = (None, n, SB_WIDTH)
    old = (None, past * HEADS, HEAD_DIM)
    return pl.pallas_call(
        functools.partial(_sb_decode_kernel, n=n, past=past),
        grid=(b,),
        in_specs=[pl.BlockSpec(new, lambda i: (i, 0, 0)),
                  pl.BlockSpec(new, lambda i: (i, 0, 1)),
                  pl.BlockSpec(new, lambda i: (i, 0, 2)),
                  pl.BlockSpec(old, lambda i: (i, 0, 0)),
                  pl.BlockSpec(old, lambda i: (i, 0, 0)),
                  pl.BlockSpec(tri.shape, lambda i: (0, 0))],
        out_specs=pl.BlockSpec(new, lambda i: (i, 0, 0)),
        out_shape=jax.ShapeDtypeStruct((b, n, SB_WIDTH), BF16),
        scratch_shapes=[pltpu.VMEM((HEADS * n, KEY_BLOCK), F32),
                        pltpu.VMEM((n, SB_WIDTH), F32)],
        compiler_params=_params("parallel"),
        name="stickbreak_attn_decode",
    )(qkv, qkv, qkv, k_past.reshape(b, past * HEADS, HEAD_DIM),
      v_past.reshape(b, past * HEADS, HEAD_DIM), tri)


CHUNK_SHIFT = CHUNK.bit_length() - 1
assert 1 << CHUNK_SHIFT == CHUNK


def _chunk(pos):
    return lax.shift_right_arithmetic(pos, CHUNK_SHIFT)


def _mla_prompt_kernel(qt_ref, k_ref, vt_ref, o_ref, s_scr, *, bk):
    bq = 2 * bk
    qi = pl.program_id(2)

    def scores(kb, slot):
        s0 = pl.multiple_of(kb * bk, bk)
        s_scr[slot] = _dot(k_ref[pl.ds(s0, bk), :], qt_ref[...])

    def absorb(st, s, kb):
        m, l, acc = st
        m_new = jnp.maximum(m, jnp.max(s, axis=0, keepdims=True))
        alpha = jnp.exp2(m - m_new)
        p = jnp.exp2(s - m_new)
        l = alpha * l + jnp.sum(p, axis=0, keepdims=True)
        acc = alpha * acc + _dot(vt_ref[kb], p.astype(BF16))
        return m_new, l, acc

    def pair(i, st):
        kb = 2 * i
        scores(kb + 1, 1)
        st = absorb(st, s_scr[0], kb)
        scores(kb + 2, 0)
        return absorb(st, s_scr[1], kb + 1)

    init = (jnp.full((1, bq), MASKED, F32), jnp.zeros((1, bq), F32), jnp.zeros((HEAD_DIM, bq), F32))
    scores(0, 0)
    st = lax.fori_loop(0, qi, pair, init)
    late = (slice(None), slice(bk, bq))
    s_late = _dot(k_ref[pl.ds(pl.multiple_of((2 * qi + 1) * bk, bk), bk), :], qt_ref[late])
    vis = (_chunk(lax.broadcasted_iota(jnp.int32, (bk, bq), 0))
           <= _chunk(lax.broadcasted_iota(jnp.int32, (bk, bq), 1)))
    m, l, acc = absorb(st, jnp.where(vis, s_scr[0], MASKED), 2 * qi)
    _, l2, acc2 = absorb((m[late], l[late], acc[late]), jnp.where(vis[:, :bk], s_late, MASKED), 2 * qi + 1)
    l = jnp.concatenate([l[:, :bk], l2], axis=1)
    acc = jnp.concatenate([acc[:, :bk], acc2], axis=1)
    o_ref[...] = (acc / l).T.astype(BF16)


def _mla_prompt(qt, k, vt, b):
    n = k.shape[1]
    bk = vt.shape[-1]
    bq = 2 * bk
    assert bk % CHUNK == 0 and n % bq == 0
    nq = n // bq
    return pl.pallas_call(
        functools.partial(_mla_prompt_kernel, bk=bk),
        grid=(b, HEADS, nq),
        in_specs=[pl.BlockSpec((QK_PAD, bq), lambda i, h, j: (h, i * nq + j)),
                  pl.BlockSpec((None, n, QK_PAD), lambda i, h, j: (i, 0, h)),
                  pl.BlockSpec((None, n // bk, HEAD_DIM, bk), lambda i, h, j: (i, 0, h, 0))],
        out_specs=pl.BlockSpec((None, bq, HEAD_DIM), lambda i, h, j: (i, j, h)),
        out_shape=jax.ShapeDtypeStruct((b, n, SB_WIDTH), BF16),
        scratch_shapes=[pltpu.VMEM((2, bk, bq), F32)],
        compiler_params=_params("parallel", "parallel", "arbitrary"),
        name="mla_attn",
    )(qt, k, vt)


def _mla_decode_kernel(q_ref, ckv_old_ref, kr_old_ref, ckv_new_ref, kr_new_ref, wuk_ref, wuv_ref, o_ref,
                       *, n, past):
    rows = HEADS * n
    ck_old = ckv_old_ref[...].astype(BF16)
    kr_old = kr_old_ref[...].astype(BF16)
    ck_new = jnp.concatenate([ckv_new_ref[...], jnp.zeros((LANES - n, KV_LORA), BF16)], axis=0)
    kr_new = jnp.concatenate([kr_new_ref[...], jnp.zeros((LANES - n, ROPE_DIM), BF16)], axis=0)
    qa = jnp.concatenate(
        [_dot(q_ref[:, h * QK_PAD:h * QK_PAD + HEAD_DIM], wuk_ref[h]) for h in range(HEADS)],
        axis=0).astype(BF16)
    qr = jnp.concatenate(
        [q_ref[:, h * QK_PAD + HEAD_DIM:h * QK_PAD + HEAD_DIM + ROPE_DIM] for h in range(HEADS)], axis=0)
    s_old = _dot_nt(qa, ck_old) + _dot_nt(qr, kr_old)
    r = jnp.bitwise_and(lax.broadcasted_iota(jnp.int32, (rows, LANES), 0), n - 1)
    c = lax.broadcasted_iota(jnp.int32, (rows, LANES), 1)
    vis = jnp.logical_and(c < n, _chunk(past + c) <= _chunk(past + r))
    s_new = jnp.where(vis, _dot_nt(qa, ck_new) + _dot_nt(qr, kr_new), MASKED)
    m = jnp.maximum(jnp.max(s_old, axis=-1, keepdims=True), jnp.max(s_new, axis=-1, keepdims=True))
    p_old = jnp.exp(s_old - m)
    p_new = jnp.exp(s_new - m)
    l = jnp.sum(p_old, axis=-1, keepdims=True) + jnp.sum(p_new, axis=-1, keepdims=True)
    ctx = ((_dot(p_old.astype(BF16), ck_old) + _dot(p_new.astype(BF16), ck_new)) / l).astype(BF16)
    for h in range(HEADS):
        o_ref[:, _head_cols(h)] = _dot(ctx[h * n:(h + 1) * n], wuv_ref[:, _head_cols(h)]).astype(BF16)


def _mla_decode(q, ckv_old, kr_old, ckv_new, kr_new, wuk_t, wuv):
    b, n, _ = q.shape
    past = ckv_old.shape[1]
    assert n <= LANES and n % 16 == 0 and n & (n - 1) == 0
    spec = lambda rows, width: pl.BlockSpec((None, rows, width), lambda i: (i, 0, 0))
    return pl.pallas_call(
        functools.partial(_mla_decode_kernel, n=n, past=past),
        grid=(b,),
        in_specs=[spec(n, HEADS * QK_PAD), spec(past, KV_LORA), spec(past, ROPE_DIM),
                  spec(n, KV_LORA), spec(n, ROPE_DIM),
                  pl.BlockSpec(wuk_t.shape, lambda i: (0, 0, 0)), pl.BlockSpec(wuv.shape, lambda i: (0, 0))],
        out_specs=spec(n, SB_WIDTH),
        out_shape=jax.ShapeDtypeStruct((b, n, SB_WIDTH), BF16),
        compiler_params=_params("parallel"),
        name="mla_attn_decode",
    )(q, ckv_old, kr_old, ckv_new, kr_new, wuk_t, wuv)


def _rope_tables(pos):
    half = ROPE_DIM // 2
    freqs = ROPE_THETA ** (-jnp.arange(half, dtype=F32) / half)
    ang = pos.astype(F32)[:, None] * freqs[None, :]
    cos, sin = jnp.cos(ang), jnp.sin(ang)
    z = jnp.zeros((pos.shape[0], LANES - ROPE_DIM), F32)
    return jnp.concatenate([cos, cos, z], axis=1), jnp.concatenate([-sin, sin, z], axis=1)


def _swap_halves(w):
    half = w.shape[-1] // 2
    return jnp.concatenate([w[..., half:], w[..., :half]], axis=-1)


def _prep_weights(w_in, w_uq, w_uk, w_uv, w_branch, w_out, w_up, w_down, w_ple_gate, w_ple):
    d = w_in.shape[0]
    o = 3 * SB_WIDTH
    w_kr = w_in[:, o + Q_LORA + KV_LORA:o + Q_LORA + KV_LORA + ROPE_DIM]
    z64 = jnp.zeros((d, LANES - ROPE_DIM), w_in.dtype)
    w_lat = jnp.concatenate([w_in[:, o:o + Q_LORA + KV_LORA], w_kr, z64, _swap_halves(w_kr), z64], axis=1)
    wq3 = w_uq.reshape(Q_LORA, HEADS, HEAD_DIM + ROPE_DIM)
    rp = wq3[:, :, HEAD_DIM:]
    zq = jnp.zeros((Q_LORA, HEADS, LANES - ROPE_DIM), w_uq.dtype)
    w_q = jnp.concatenate([
        jnp.concatenate([wq3[:, :, :HEAD_DIM], rp, zq], axis=-1).reshape(Q_LORA, HEADS * QK_PAD),
        jnp.concatenate([_swap_halves(rp), zq], axis=-1).reshape(Q_LORA, HEADS * LANES)], axis=1)
    return dict(
        qkv=w_in.astype(BF16),
        lat=w_lat.astype(BF16),
        gate=w_in[:, o + Q_LORA + KV_LORA + ROPE_DIM:].astype(BF16),
        q=w_q.astype(BF16),
        qt=w_q[:, :HEADS * QK_PAD].T.astype(BF16),
        uk=w_uk.astype(BF16),
        uk_t=w_uk.reshape(KV_LORA, HEADS, HEAD_DIM).transpose(1, 2, 0).astype(BF16),
        uv=w_uv.astype(BF16),
        uvt=w_uv.T.astype(BF16),
        branch=w_branch.astype(BF16),
        out=w_out.astype(BF16),
        up=w_up.astype(BF16),
        down=w_down.astype(BF16),
        ple_gate=w_ple_gate.astype(BF16),
        ple=w_ple.astype(BF16),
    )


def _tile(t, want):
    return want if t % want == 0 else t


def _layer(x, ple, past, w, g, tri):
    b, n, d = x.shape
    t = b * n
    xf = x.reshape(t, d)
    tm = _tile(t, 1024)
    tm_s = _tile(t, 512)
    past_len = 0 if past is None else past[0].shape[1]
    if n >= tm:
        pos = jnp.arange(n) + past_len
    else:
        pos = jnp.tile(jnp.arange(n) + past_len, tm // n)
    cos2, sin2 = _rope_tables(pos)

    h, qkv_b, k_f, v_f = _qkv(xf, g['mix_pre'], w['qkv'], tm_s)
    cq, ckv_f, ckv_b, kr_f, kr_b = _latent(h, w['lat'], g['q'], g['kv'], cos2, sin2, tm)
    qkv_b = qkv_b.reshape(b, n, -1)

    if past is None:
        blk = _tile(n, 512)
        qt = _mla_qt(cq, w['qt'], cos2[:, :ROPE_DIM // 2].T, sin2[:, ROPE_DIM // 2:ROPE_DIM].T, blk)
        k_mla, vt = _mla_kvt(ckv_b, kr_b, w['uk'], w['uvt'], blk)
        o_sb, gates = _sb_prompt_gates(qkv_b, tri, h, w['gate'])
        o_mla = _mla_prompt(qt, k_mla.reshape(b, n, -1), vt.reshape(b, n // blk, SB_WIDTH, blk), b)
    else:
        gates = _gates(h, w['gate'], tm, 1024)
        q_mla = _mla_q(cq, w['q'], cos2, sin2, tm_s).reshape(b, n, -1)
        o_sb = _sb_decode(qkv_b, past[0], past[1], tri)
        o_mla = _mla_decode(q_mla, past[2], past[3], ckv_b.reshape(b, n, -1), kr_b.reshape(b, n, -1),
                            w['uk_t'], w['uv'])

    merged = _merge(o_sb.reshape(t, -1), o_mla.reshape(t, -1), w['branch'], gates, tm, 1024)
    x1 = _out_proj(merged, w['out'], xf, g['mix_post'], tm_s)
    x2 = _ffn(x1, g['ffn_pre'], w['up'], w['down'], g['ffn_post'], tm_s, 1024)
    x3 = _ple(x2, ple.reshape(t, -1), g['ple_gate'], w['ple_gate'], w['ple'], g['ple_post'], tm_s)
    state = (k_f.reshape(1, b, n, HEADS, HEAD_DIM), v_f.reshape(1, b, n, HEADS, HEAD_DIM),
             ckv_f.reshape(1, b, n, KV_LORA), kr_f.reshape(1, b, n, ROPE_DIM))
    return x3.reshape(b, n, d), state


def kernel(x_prompt, x_sample, cache_sb_k, cache_sb_v, cache_mla_ckv, cache_mla_krope, p_prompt, p_sample, g_mix_pre, w_in, g_q, w_uq, g_kv, w_uk, w_uv, w_branch, w_out, g_mix_post, g_ffn_pre, w_up, w_down, g_ffn_post, g_ple_gate, w_ple_gate, w_ple, g_ple_post):
    assert w_in.shape[0] == 1, "single layer"
    w = _prep_weights(w_in[0], w_uq[0], w_uk[0], w_uv[0], w_branch[0], w_out[0], w_up[0], w_down[0],
                      w_ple_gate[0], w_ple[0])
    g = dict(mix_pre=g_mix_pre, q=g_q, kv=g_kv, mix_post=g_mix_post, ffn_pre=g_ffn_pre,
             ffn_post=g_ffn_post, ple_gate=g_ple_gate, ple_post=g_ple_post)
    idx = jnp.arange(KEY_BLOCK)
    lower = -(idx[:, None] > idx[None, :]).astype(BF16)
    tri = jnp.concatenate([lower, jnp.full((KEY_BLOCK, KEY_BLOCK), -1, BF16)], axis=1)
    tri = jnp.concatenate([tri, tri], axis=0)

    yp, sp = _layer(x_prompt, p_prompt[0], None, w, g, tri)
    past = (cache_sb_k[0], cache_sb_v[0], cache_mla_ckv[0], cache_mla_krope[0])
    ys, ss = _layer(x_sample, p_sample[0], past, w, g, tri)
    return (yp, ys) + sp + ss
```

```python
import functools

import jax
import jax.numpy as jnp
from jax import lax
from jax.experimental import pallas as pl
from jax.experimental.pallas import tpu as pltpu

F32 = jnp.float32
BF16 = jnp.bfloat16

D_MODEL = 2048
CHUNK = 64
PLE_DIM = 256
HEADS = 8
HEAD_DIM = 128
SB_WIDTH = HEADS * HEAD_DIM
ROPE_DIM = 64
Q_LORA = 512
KV_LORA = 512
QK_PAD = 256
D_FF = 4 * D_MODEL
ROPE_THETA = 10000.0
EPS = 1e-6
SB_SCALE = HEAD_DIM ** -0.5
MLA_SCALE = (HEAD_DIM + ROPE_DIM) ** -0.5
LOG2E = 1.4426950408889634

LANES = 128
KEY_BLOCK = 128
SB_EXIT = -104.0
MASKED = -1e30
VMEM_LIMIT = 56 * 1024 * 1024


def _params(*sem):
    return pltpu.CompilerParams(dimension_semantics=sem, vmem_limit_bytes=VMEM_LIMIT)


def _rms(xf, g):
    ms = jnp.mean(xf * xf, axis=-1, keepdims=True)
    return xf * lax.rsqrt(ms + EPS) * g


def _dot(a, b):
    return jnp.dot(a, b, preferred_element_type=F32)


def _dot_nt(a, b):
    return lax.dot_general(a, b, (((1,), (1,)), ((), ())), preferred_element_type=F32)


def _qkv_kernel(x_ref, g_ref, w_ref, h_ref, qkv_ref, kf_ref, vf_ref, h_scr):
    j = pl.program_id(1)

    @pl.when(j == 0)
    def _():
        h = _rms(x_ref[...], g_ref[...]).astype(BF16)
        h_scr[...] = h
        h_ref[...] = h

    acc = _dot(h_scr[...], w_ref[...])
    qkv_ref[...] = (acc * jnp.where(j == 0, SB_SCALE, 1.0)).astype(BF16)

    def store_state(ref):
        for h in range(HEADS):
            ref[pl.ds(h, acc.shape[0], stride=HEADS), :] = acc[:, _head_cols(h)]

    pl.when(j == 1)(lambda: store_state(kf_ref))
    pl.when(j == 2)(lambda: store_state(vf_ref))


def _qkv(x, g, w, tm):
    t, d = x.shape
    row = lambda i, j: (i, 0)
    state = pl.BlockSpec((tm * HEADS, HEAD_DIM), row)
    return pl.pallas_call(
        _qkv_kernel,
        grid=(t // tm, 3),
        in_specs=[pl.BlockSpec((tm, d), row), pl.BlockSpec((1, d), lambda i, j: (0, 0)),
                  pl.BlockSpec((d, SB_WIDTH), lambda i, j: (0, j))],
        out_specs=[pl.BlockSpec((tm, d), row), pl.BlockSpec((tm, SB_WIDTH), lambda i, j: (i, j)),
                   state, state],
        out_shape=[jax.ShapeDtypeStruct((t, d), BF16),
                   jax.ShapeDtypeStruct((t, 3 * SB_WIDTH), BF16),
                   jax.ShapeDtypeStruct((t * HEADS, HEAD_DIM), F32),
                   jax.ShapeDtypeStruct((t * HEADS, HEAD_DIM), F32)],
        scratch_shapes=[pltpu.VMEM((tm, d), BF16)],
        compiler_params=_params("parallel", "arbitrary"),
        name="norm_qkv_proj",
    )(x, g, w)


def _latent_kernel(h_ref, w_ref, gq_ref, gkv_ref, cos_ref, sin_ref,
                   cq_ref, ckvf_ref, ckvb_ref, krf_ref, krb_ref):
    acc = _dot(h_ref[...], w_ref[...])
    cq_ref[...] = _rms(acc[:, :Q_LORA], gq_ref[...]).astype(BF16)
    ckv = _rms(acc[:, Q_LORA:Q_LORA + KV_LORA], gkv_ref[...])
    ckvf_ref[...] = ckv
    ckvb_ref[...] = ckv.astype(BF16)
    base = Q_LORA + KV_LORA
    kr = acc[:, base:base + LANES] * cos_ref[...] + acc[:, base + LANES:base + 2 * LANES] * sin_ref[...]
    krf_ref[...] = kr[:, :ROPE_DIM]
    krb_ref[...] = kr[:, :ROPE_DIM].astype(BF16)


def _latent(h, w, gq, gkv, cos2, sin2, tm):
    t, d = h.shape
    n_tab = cos2.shape[0] // tm
    row = lambda i: (i, 0)
    const = lambda i: (0, 0)
    tab = lambda i: (i % n_tab, 0)
    wn = w.shape[1]
    return pl.pallas_call(
        _latent_kernel,
        grid=(t // tm,),
        in_specs=[pl.BlockSpec((tm, d), row), pl.BlockSpec((d, wn), const),
                  pl.BlockSpec((1, Q_LORA), const), pl.BlockSpec((1, KV_LORA), const),
                  pl.BlockSpec((tm, LANES), tab), pl.BlockSpec((tm, LANES), tab)],
        out_specs=[pl.BlockSpec((tm, Q_LORA), row), pl.BlockSpec((tm, KV_LORA), row),
                   pl.BlockSpec((tm, KV_LORA), row), pl.BlockSpec((tm, ROPE_DIM), row),
                   pl.BlockSpec((tm, ROPE_DIM), row)],
        out_shape=[jax.ShapeDtypeStruct((t, Q_LORA), BF16),
                   jax.ShapeDtypeStruct((t, KV_LORA), F32),
                   jax.ShapeDtypeStruct((t, KV_LORA), BF16),
                   jax.ShapeDtypeStruct((t, ROPE_DIM), F32),
                   jax.ShapeDtypeStruct((t, ROPE_DIM), BF16)],
        compiler_params=_params("parallel"),
        name="mla_latent_proj",
    )(h, w, gq, gkv, cos2, sin2)


def _mla_q_kernel(cq_ref, w_ref, cos_ref, sin_ref, q_ref):
    acc = _dot(cq_ref[...], w_ref[...])
    cos, sin = cos_ref[...], sin_ref[...]
    for h in range(HEADS):
        b = h * QK_PAD
        q_ref[:, b:b + LANES] = (acc[:, b:b + LANES] * MLA_SCALE).astype(BF16)
        sw = HEADS * QK_PAD + h * LANES
        rot = acc[:, b + LANES:b + QK_PAD] * cos + acc[:, sw:sw + LANES] * sin
        q_ref[:, b + LANES:b + QK_PAD] = (rot * MLA_SCALE).astype(BF16)


def _mla_q(cq, w, cos2, sin2, tm):
    t = cq.shape[0]
    n_tab = cos2.shape[0] // tm
    row = lambda i: (i, 0)
    tab = lambda i: (i % n_tab, 0)
    return pl.pallas_call(
        _mla_q_kernel,
        grid=(t // tm,),
        in_specs=[pl.BlockSpec((tm, Q_LORA), row),
                  pl.BlockSpec(w.shape, lambda i: (0, 0)),
                  pl.BlockSpec((tm, LANES), tab), pl.BlockSpec((tm, LANES), tab)],
        out_specs=pl.BlockSpec((tm, HEADS * QK_PAD), row),
        out_shape=jax.ShapeDtypeStruct((t, HEADS * QK_PAD), BF16),
        compiler_params=_params("parallel"),
        name="mla_q_proj",
    )(cq, w, cos2, sin2)


def _mla_qt_kernel(cq_ref, wt_ref, cos_ref, sin_ref, qt_ref):
    acc = _dot_nt(wt_ref[...], cq_ref[...])
    cos, sin = cos_ref[...], sin_ref[...]
    half = ROPE_DIM // 2
    scale = MLA_SCALE * LOG2E
    for h in range(HEADS):
        b = h * QK_PAD
        r = b + HEAD_DIM
        qt_ref[b:r, :] = (acc[b:r] * scale).astype(BF16)
        x1, x2 = acc[r:r + half], acc[r + half:r + ROPE_DIM]
        qt_ref[r:r + half, :] = ((x1 * cos - x2 * sin) * scale).astype(BF16)
        qt_ref[r + half:r + ROPE_DIM, :] = ((x1 * sin + x2 * cos) * scale).astype(BF16)
        qt_ref[r + ROPE_DIM:b + QK_PAD, :] = jnp.zeros((QK_PAD - HEAD_DIM - ROPE_DIM, acc.shape[1]), BF16)


def _mla_qt(cq, wt, cos_t, sin_t, tm):
    t = cq.shape[0]
    n_tab = cos_t.shape[1] // tm
    tab = lambda i: (0, i % n_tab)
    half = ROPE_DIM // 2
    return pl.pallas_call(
        _mla_qt_kernel,
        grid=(t // tm,),
        in_specs=[pl.BlockSpec((tm, Q_LORA), lambda i: (i, 0)),
                  pl.BlockSpec(wt.shape, lambda i: (0, 0)),
                  pl.BlockSpec((half, tm), tab), pl.BlockSpec((half, tm), tab)],
        out_specs=pl.BlockSpec((HEADS * QK_PAD, tm), lambda i: (0, i)),
        out_shape=jax.ShapeDtypeStruct((HEADS * QK_PAD, t), BF16),
        compiler_params=_params("parallel"),
        name="mla_q_proj_t",
    )(cq, wt, cos_t, sin_t)


def _mla_kvt_kernel(ckv_ref, kr_ref, wk_ref, wvt_ref, k_ref, vt_ref):
    ckv = ckv_ref[...]
    acc = _dot(ckv, wk_ref[...])
    kr = kr_ref[...]
    zero = jnp.zeros((kr.shape[0], QK_PAD - LANES - ROPE_DIM), BF16)
    for h in range(HEADS):
        b = h * QK_PAD
        k_ref[:, b:b + LANES] = acc[:, h * LANES:(h + 1) * LANES].astype(BF16)
        k_ref[:, b + LANES:b + LANES + ROPE_DIM] = kr
        k_ref[:, b + LANES + ROPE_DIM:b + QK_PAD] = zero
    vt_ref[...] = _dot_nt(wvt_ref[...], ckv).astype(BF16)


def _mla_kvt(ckv, kr, wk, wvt, tm):
    t = ckv.shape[0]
    row = lambda i: (i, 0)
    const = lambda i: (0, 0)
    return pl.pallas_call(
        _mla_kvt_kernel,
        grid=(t // tm,),
        in_specs=[pl.BlockSpec((tm, KV_LORA), row), pl.BlockSpec((tm, ROPE_DIM), row),
                  pl.BlockSpec(wk.shape, const), pl.BlockSpec(wvt.shape, const)],
        out_specs=[pl.BlockSpec((tm, HEADS * QK_PAD), row),
                   pl.BlockSpec((None, SB_WIDTH, tm), lambda i: (i, 0, 0))],
        out_shape=[jax.ShapeDtypeStruct((t, HEADS * QK_PAD), BF16),
                   jax.ShapeDtypeStruct((t // tm, SB_WIDTH, tm), BF16)],
        compiler_params=_params("parallel"),
        name="mla_kv_proj_t",
    )(ckv, kr, wk, wvt)


def _gate_kernel(h_ref, w_ref, o_ref):
    o_ref[...] = jax.nn.sigmoid(_dot(h_ref[...], w_ref[...])).astype(BF16)


def _gates(h, w, tm, tn):
    t, d = h.shape
    n = w.shape[1]
    return pl.pallas_call(
        _gate_kernel,
        grid=(t // tm, n // tn),
        in_specs=[pl.BlockSpec((tm, d), lambda i, j: (i, 0)),
                  pl.BlockSpec((d, tn), lambda i, j: (0, j))],
        out_specs=pl.BlockSpec((tm, tn), lambda i, j: (i, j)),
        out_shape=jax.ShapeDtypeStruct((t, n), BF16),
        compiler_params=_params("parallel", "arbitrary"),
        name="branch_gates",
    )(h, w)


def _mix_kernel(osb_ref, omla_ref, g_ref, wb_ref, wo_ref, x_ref, gn_ref, o_ref):
    d = x_ref.shape[1]
    merged = (g_ref[:, :d].astype(F32) * _dot(osb_ref[...], wb_ref[0])
              + g_ref[:, d:].astype(F32) * _dot(omla_ref[...], wb_ref[1]))
    o_ref[...] = x_ref[...] + _rms(_dot(merged.astype(BF16), wo_ref[...]), gn_ref[...])


def _mix(osb, omla, gates, wb, wo, x, gn, tm):
    t, d = x.shape
    row = lambda i: (i, 0)
    once = pl.Buffered(1)
    return pl.pallas_call(
        _mix_kernel,
        grid=(t // tm,),
        in_specs=[pl.BlockSpec((tm, SB_WIDTH), row), pl.BlockSpec((tm, SB_WIDTH), row),
                  pl.BlockSpec((tm, 2 * d), row),
                  pl.BlockSpec(wb.shape, lambda i: (0, 0, 0), pipeline_mode=once),
                  pl.BlockSpec(wo.shape, lambda i: (0, 0), pipeline_mode=once),
                  pl.BlockSpec((tm, d), row), pl.BlockSpec((1, d), lambda i: (0, 0))],
        out_specs=pl.BlockSpec((tm, d), row),
        out_shape=jax.ShapeDtypeStruct((t, d), F32),
        compiler_params=_params("parallel"),
        name="branch_mix_out",
    )(osb, omla, gates, wb, wo, x, gn)


def _ffn_kernel(x_ref, gpre_ref, wu_ref, wd_ref, gpost_ref, o_ref, h_scr, acc_scr):
    j = pl.program_id(1)

    @pl.when(j == 0)
    def _():
        h_scr[...] = _rms(x_ref[...], gpre_ref[...]).astype(BF16)
        acc_scr[...] = jnp.zeros_like(acc_scr)

    u = jnp.maximum(_dot(h_scr[...], wu_ref[...]), 0.0)
    acc_scr[...] += _dot((u * u).astype(BF16), wd_ref[...])

    @pl.when(j == pl.num_programs(1) - 1)
    def _():
        o_ref[...] = x_ref[...] + _rms(acc_scr[...], gpost_ref[...])


def _ffn(x, gpre, wu, wd, gpost, tm, tf):
    t, d = x.shape
    row = lambda i, j: (i, 0)
    const = lambda i, j: (0, 0)
    return pl.pallas_call(
        _ffn_kernel,
        grid=(t // tm, D_FF // tf),
        in_specs=[pl.BlockSpec((tm, d), row), pl.BlockSpec((1, d), const),
                  pl.BlockSpec((d, tf), lambda i, j: (0, j)),
                  pl.BlockSpec((tf, d), lambda i, j: (j, 0)),
                  pl.BlockSpec((1, d), const)],
        out_specs=pl.BlockSpec((tm, d), row),
        out_shape=jax.ShapeDtypeStruct((t, d), F32),
        scratch_shapes=[pltpu.VMEM((tm, d), BF16), pltpu.VMEM((tm, d), F32)],
        compiler_params=_params("parallel", "arbitrary"),
        name="sqrelu_ffn",
    )(x, gpre, wu, wd, gpost)


def _ple_kernel(x_ref, p_ref, gg_ref, wg_ref, wp_ref, gpost_ref, o_ref):
    x = x_ref[...]
    gate = jax.nn.sigmoid(_dot(_rms(x, gg_ref[...]).astype(BF16), wg_ref[...]))
    pe = _dot(p_ref[...].astype(BF16), wp_ref[...])
    o_ref[...] = x + _rms(pe * gate, gpost_ref[...])


def _ple(x, p, gg, wg, wp, gpost, tm):
    t, d = x.shape
    row = lambda i: (i, 0)
    const = lambda i: (0, 0)
    return pl.pallas_call(
        _ple_kernel,
        grid=(t // tm,),
        in_specs=[pl.BlockSpec((tm, d), row), pl.BlockSpec((tm, PLE_DIM), row),
                  pl.BlockSpec((1, d), const), pl.BlockSpec((d, d), const),
                  pl.BlockSpec((PLE_DIM, d), const), pl.BlockSpec((1, d), const)],
        out_specs=pl.BlockSpec((tm, d), row),
        out_shape=jax.ShapeDtypeStruct((t, d), F32),
        compiler_params=_params("parallel"),
        name="ple_embed",
    )(x, p, gg, wg, wp, gpost)


def _sb_step(qs, ks, vs, tri, carry, mask):
    rows = qs[0].shape[0]
    z = jnp.concatenate([_dot_nt(q, k) for q, k in zip(qs, ks)], axis=0)
    sp = jnp.maximum(z, 0.0) + jnp.log(1.0 + jnp.exp(-jnp.abs(z)))
    spm = sp if mask is None else jnp.where(mask, sp, 0.0)
    hi = spm.astype(BF16)
    lo = (spm - hi.astype(F32)).astype(BF16)
    cs = _dot(jnp.concatenate([hi, lo], axis=1), tri)
    after = cs[:, :KEY_BLOCK] if carry is None else cs[:, :KEY_BLOCK] + carry
    w = jnp.exp((z - sp) + after)
    if mask is not None:
        w = jnp.where(mask, w, 0.0)
    w = w.astype(BF16)
    pvs = [_dot(w[i * rows:(i + 1) * rows], v) for i, v in enumerate(vs)]
    carry = cs[:, KEY_BLOCK:] if carry is None else carry + cs[:, KEY_BLOCK:]
    return carry, pvs


def _sb_walk(first_blocks, later_blocks, tri_ref, carry_scr, acc_scr, o_ref, alongside=None):
    def run(chains, carry, mask, first):
        carry, pvs = _sb_step([c[2] for c in chains], [c[3] for c in chains], [c[4] for c in chains],
                              tri_ref[...], carry, mask)
        carry_scr[...] = carry
        for (rows, cols, _, _, _), pv in zip(chains, pvs):
            if first:
                acc_scr[rows, cols] = pv
            else:
                acc_scr[rows, cols] += pv
        return jnp.max(carry)

    def step0():
        chains, mask = first_blocks()
        return run(chains, None, mask, True)

    def step(j):
        chains, valid = later_blocks(j)
        rows = chains[0][2].shape[0]
        carry = jnp.concatenate(
            [jnp.where(ok, carry_scr[i * rows:(i + 1) * rows, :], MASKED) for i, ok in enumerate(valid)],
            axis=0)
        return run(chains, carry, None, False)

    if alongside is None:
        start = (jnp.int32(1), step0())
    else:
        finish = alongside()
        step0()
        step(1)
        start = (jnp.int32(3), step(2))
        finish()
    lax.while_loop(lambda st: st[1] > SB_EXIT, lambda st: (st[0] + 1, step(st[0])), start)
    o_ref[...] = acc_scr[...].astype(BF16)


def _head_cols(h):
    return slice(h * HEAD_DIM, (h + 1) * HEAD_DIM)


def _sb_prompt_kernel(q_ref, k_ref, v_ref, tri_ref, h_ref, wg_ref, o_ref, gate_ref, carry_scr, acc_scr,
                      *, rsub):
    bk = KEY_BLOCK
    base = pl.program_id(1) * rsub
    n_chain = rsub * HEADS
    r = lax.broadcasted_iota(jnp.int32, (n_chain * bk, bk), 0)
    c = lax.broadcasted_iota(jnp.int32, (n_chain * bk, bk), 1)
    causal = c < jnp.bitwise_and(r, bk - 1)

    def blocks(j):
        chains, valid = [], []
        for s in range(rsub):
            kb = base + s - j
            start = pl.multiple_of(jnp.maximum(kb, 0) * bk, bk)
            rows = slice(s * bk, (s + 1) * bk)
            for h in range(HEADS):
                cols = _head_cols(h)
                chains.append((rows, cols, q_ref[rows, cols],
                               k_ref[pl.ds(start, bk), cols], v_ref[pl.ds(start, bk), cols]))
                valid.append(kb >= 0)
        return chains, valid

    def gates():
        logits = _dot(h_ref[...], wg_ref[...])

        def finish():
            gate_ref[...] = jax.nn.sigmoid(logits).astype(BF16)
        return finish

    _sb_walk(lambda: (blocks(0)[0], causal), blocks, tri_ref, carry_scr, acc_scr, o_ref, alongside=gates)


GATE_TILE = 1024


def _sb_prompt_gates(qkv, tri, h, wg):
    b, n, _ = qkv.shape
    t, d = h.shape
    gw = wg.shape[1]
    bq = GATE_TILE * GATE_TILE // gw
    rsub = bq // KEY_BLOCK
    assert bq % KEY_BLOCK == 0 and n % bq == 0 and t % GATE_TILE == 0 and gw % GATE_TILE == 0
    nq = n // bq
    ncol = gw // GATE_TILE
    full = (None, n, SB_WIDTH)
    once = pl.Buffered(1)
    return pl.pallas_call(
        functools.partial(_sb_prompt_kernel, rsub=rsub),
        grid=(b, nq),
        in_specs=[pl.BlockSpec((None, bq, SB_WIDTH), lambda i, j: (i, j, 0)),
                  pl.BlockSpec(full, lambda i, j: (i, 0, 1), pipeline_mode=once),
                  pl.BlockSpec(full, lambda i, j: (i, 0, 2), pipeline_mode=once),
                  pl.BlockSpec(tri.shape, lambda i, j: (0, 0)),
                  pl.BlockSpec((GATE_TILE, d), lambda i, j: ((i * nq + j) // ncol, 0)),
                  pl.BlockSpec((d, GATE_TILE), lambda i, j: (0, (i * nq + j) % ncol))],
        out_specs=[pl.BlockSpec((None, bq, SB_WIDTH), lambda i, j: (i, j, 0)),
                   pl.BlockSpec((GATE_TILE, GATE_TILE), lambda i, j: ((i * nq + j) // ncol, (i * nq + j) % ncol))],
        out_shape=[jax.ShapeDtypeStruct((b, n, SB_WIDTH), BF16),
                   jax.ShapeDtypeStruct((t, gw), BF16)],
        scratch_shapes=[pltpu.VMEM((rsub * HEADS * KEY_BLOCK, KEY_BLOCK), F32),
                        pltpu.VMEM((bq, SB_WIDTH), F32)],
        compiler_params=_params("arbitrary", "arbitrary"),
        name="stickbreak_attn_gates",
    )(qkv, qkv, qkv, tri, h, wg)


def _sb_decode_kernel(q_ref, kn_ref, vn_ref, kp_ref, vp_ref, tri_ref, o_ref, carry_scr, acc_scr,
                      *, n, past):
    bk = KEY_BLOCK
    nb = past // bk
    assert n & (n - 1) == 0
    r = lax.broadcasted_iota(jnp.int32, (HEADS * n, bk), 0)
    c = lax.broadcasted_iota(jnp.int32, (HEADS * n, bk), 1)
    causal = c < jnp.bitwise_and(r, n - 1)
    zpad = jnp.zeros((bk - n, HEAD_DIM), BF16)
    rows = slice(0, n)

    def first():
        return [(rows, _head_cols(h), q_ref[:, _head_cols(h)],
                 jnp.concatenate([kn_ref[:, _head_cols(h)], zpad], axis=0),
                 jnp.concatenate([vn_ref[:, _head_cols(h)], zpad], axis=0))
                for h in range(HEADS)], causal

    def later(j):
        kb = nb - j
        start = jnp.maximum(kb, 0) * (bk * HEADS)
        chains = []
        for h in range(HEADS):
            sl = pl.ds(start + h, bk, stride=HEADS)
            chains.append((rows, _head_cols(h), q_ref[:, _head_cols(h)],
                           kp_ref[sl, :].astype(BF16), vp_ref[sl, :].astype(BF16)))
        return chains, [kb >= 0] * HEADS

    _sb_walk(first, later, tri_ref, carry_scr, acc_scr, o_ref)


def _sb_decode(qkv, k_past, v_past, tri):
    b, n, _ = qkv.shape
    past = k_past.shape[1]
    assert past % KEY_BLOCK == 0 and n <= KEY_BLOCK and n % 16 == 0
    new = (None, n, SB_WIDTH)
    old = (None, past * HEADS, HEAD_DIM)
    return pl.pallas_call(
        functools.partial(_sb_decode_kernel, n=n, past=past),
        grid=(b,),
        in_specs=[pl.BlockSpec(new, lambda i: (i, 0, 0)),
                  pl.BlockSpec(new, lambda i: (i, 0, 1)),
                  pl.BlockSpec(new, lambda i: (i, 0, 2)),
                  pl.BlockSpec(old, lambda i: (i, 0, 0)),
                  pl.BlockSpec(old, lambda i: (i, 0, 0)),
                  pl.BlockSpec(tri.shape, lambda i: (0, 0))],
        out_specs=pl.BlockSpec(new, lambda i: (i, 0, 0)),
        out_shape=jax.ShapeDtypeStruct((b, n, SB_WIDTH), BF16),
        scratch_shapes=[pltpu.VMEM((HEADS * n, KEY_BLOCK), F32),
                        pltpu.VMEM((n, SB_WIDTH), F32)],
        compiler_params=_params("parallel"),
        name="stickbreak_attn_decode",
    )(qkv, qkv, qkv, k_past.reshape(b, past * HEADS, HEAD_DIM),
      v_past.reshape(b, past * HEADS, HEAD_DIM), tri)


CHUNK_SHIFT = CHUNK.bit_length() - 1
assert 1 << CHUNK_SHIFT == CHUNK


def _chunk(pos):
    return lax.shift_right_arithmetic(pos, CHUNK_SHIFT)


def _mla_prompt_kernel(qt_ref, k_ref, vt_ref, o_ref, s_scr, *, bk):
    bq = 2 * bk
    qi = pl.program_id(2)

    def scores(kb, slot):
        s0 = pl.multiple_of(kb * bk, bk)
        s_scr[slot] = _dot(k_ref[pl.ds(s0, bk), :], qt_ref[...])

    def absorb(st, s, kb):
        m, l, acc = st
        m_new = jnp.maximum(m, jnp.max(s, axis=0, keepdims=True))
        alpha = jnp.exp2(m - m_new)
        p = jnp.exp2(s - m_new)
        l = alpha * l + jnp.sum(p, axis=0, keepdims=True)
        acc = alpha * acc + _dot(vt_ref[kb], p.astype(BF16))
        return m_new, l, acc

    def pair(i, st):
        kb = 2 * i
        scores(kb + 1, 1)
        st = absorb(st, s_scr[0], kb)
        scores(kb + 2, 0)
        return absorb(st, s_scr[1], kb + 1)

    init = (jnp.full((1, bq), MASKED, F32), jnp.zeros((1, bq), F32), jnp.zeros((HEAD_DIM, bq), F32))
    scores(0, 0)
    st = lax.fori_loop(0, qi, pair, init)
    late = (slice(None), slice(bk, bq))
    s_late = _dot(k_ref[pl.ds(pl.multiple_of((2 * qi + 1) * bk, bk), bk), :], qt_ref[late])
    vis = (_chunk(lax.broadcasted_iota(jnp.int32, (bk, bq), 0))
           <= _chunk(lax.broadcasted_iota(jnp.int32, (bk, bq), 1)))
    m, l, acc = absorb(st, jnp.where(vis, s_scr[0], MASKED), 2 * qi)
    _, l2, acc2 = absorb((m[late], l[late], acc[late]), jnp.where(vis[:, :bk], s_late, MASKED), 2 * qi + 1)
    l = jnp.concatenate([l[:, :bk], l2], axis=1)
    acc = jnp.concatenate([acc[:, :bk], acc2], axis=1)
    o_ref[...] = (acc / l).T.astype(BF16)


def _mla_prompt(qt, k, vt, b):
    n = k.shape[1]
    bk = vt.shape[-1]
    bq = 2 * bk
    assert bk % CHUNK == 0 and n % bq == 0
    nq = n // bq
    return pl.pallas_call(
        functools.partial(_mla_prompt_kernel, bk=bk),
        grid=(b, HEADS, nq),
        in_specs=[pl.BlockSpec((QK_PAD, bq), lambda i, h, j: (h, i * nq + j)),
                  pl.BlockSpec((None, n, QK_PAD), lambda i, h, j: (i, 0, h)),
                  pl.BlockSpec((None, n // bk, HEAD_DIM, bk), lambda i, h, j: (i, 0, h, 0))],
        out_specs=pl.BlockSpec((None, bq, HEAD_DIM), lambda i, h, j: (i, j, h)),
        out_shape=jax.ShapeDtypeStruct((b, n, SB_WIDTH), BF16),
        scratch_shapes=[pltpu.VMEM((2, bk, bq), F32)],
        compiler_params=_params("parallel", "parallel", "arbitrary"),
        name="mla_attn",
    )(qt, k, vt)


def _mla_decode_kernel(q_ref, ckv_old_ref, kr_old_ref, ckv_new_ref, kr_new_ref, wuk_ref, wuv_ref, o_ref,
                       *, n, past):
    rows = HEADS * n
    ck_old = ckv_old_ref[...].astype(BF16)
    kr_old = kr_old_ref[...].astype(BF16)
    ck_new = jnp.concatenate([ckv_new_ref[...], jnp.zeros((LANES - n, KV_LORA), BF16)], axis=0)
    kr_new = jnp.concatenate([kr_new_ref[...], jnp.zeros((LANES - n, ROPE_DIM), BF16)], axis=0)
    qa = jnp.concatenate(
        [_dot(q_ref[:, h * QK_PAD:h * QK_PAD + HEAD_DIM], wuk_ref[h]) for h in range(HEADS)],
        axis=0).astype(BF16)
    qr = jnp.concatenate(
        [q_ref[:, h * QK_PAD + HEAD_DIM:h * QK_PAD + HEAD_DIM + ROPE_DIM] for h in range(HEADS)], axis=0)
    s_old = _dot_nt(qa, ck_old) + _dot_nt(qr, kr_old)
    r = jnp.bitwise_and(lax.broadcasted_iota(jnp.int32, (rows, LANES), 0), n - 1)
    c = lax.broadcasted_iota(jnp.int32, (rows, LANES), 1)
    vis = jnp.logical_and(c < n, _chunk(past + c) <= _chunk(past + r))
    s_new = jnp.where(vis, _dot_nt(qa, ck_new) + _dot_nt(qr, kr_new), MASKED)
    m = jnp.maximum(jnp.max(s_old, axis=-1, keepdims=True), jnp.max(s_new, axis=-1, keepdims=True))
    p_old = jnp.exp(s_old - m)
    p_new = jnp.exp(s_new - m)
    l = jnp.sum(p_old, axis=-1, keepdims=True) + jnp.sum(p_new, axis=-1, keepdims=True)
    ctx = ((_dot(p_old.astype(BF16), ck_old) + _dot(p_new.astype(BF16), ck_new)) / l).astype(BF16)
    for h in range(HEADS):
        o_ref[:, _head_cols(h)] = _dot(ctx[h * n:(h + 1) * n], wuv_ref[:, _head_cols(h)]).astype(BF16)


def _mla_decode(q, ckv_old, kr_old, ckv_new, kr_new, wuk_t, wuv):
    b, n, _ = q.shape
    past = ckv_old.shape[1]
    assert n <= LANES and n % 16 == 0 and n & (n - 1) == 0
    spec = lambda rows, width: pl.BlockSpec((None, rows, width), lambda i: (i, 0, 0))
    return pl.pallas_call(
        functools.partial(_mla_decode_kernel, n=n, past=past),
        grid=(b,),
        in_specs=[spec(n, HEADS * QK_PAD), spec(past, KV_LORA), spec(past, ROPE_DIM),
                  spec(n, KV_LORA), spec(n, ROPE_DIM),
                  pl.BlockSpec(wuk_t.shape, lambda i: (0, 0, 0)), pl.BlockSpec(wuv.shape, lambda i: (0, 0))],
        out_specs=spec(n, SB_WIDTH),
        out_shape=jax.ShapeDtypeStruct((b, n, SB_WIDTH), BF16),
        compiler_params=_params("parallel"),
        name="mla_attn_decode",
    )(q, ckv_old, kr_old, ckv_new, kr_new, wuk_t, wuv)


def _rope_tables(pos):
    half = ROPE_DIM // 2
    freqs = ROPE_THETA ** (-jnp.arange(half, dtype=F32) / half)
    ang = pos.astype(F32)[:, None] * freqs[None, :]
    cos, sin = jnp.cos(ang), jnp.sin(ang)
    z = jnp.zeros((pos.shape[0], LANES - ROPE_DIM), F32)
    return jnp.concatenate([cos, cos, z], axis=1), jnp.concatenate([-sin, sin, z], axis=1)


def _swap_halves(w):
    half = w.shape[-1] // 2
    return jnp.concatenate([w[..., half:], w[..., :half]], axis=-1)


def _prep_weights(w_in, w_uq, w_uk, w_uv, w_branch, w_out, w_up, w_down, w_ple_gate, w_ple):
    d = w_in.shape[0]
    o = 3 * SB_WIDTH
    w_kr = w_in[:, o + Q_LORA + KV_LORA:o + Q_LORA + KV_LORA + ROPE_DIM]
    z64 = jnp.zeros((d, LANES - ROPE_DIM), w_in.dtype)
    w_lat = jnp.concatenate([w_in[:, o:o + Q_LORA + KV_LORA], w_kr, z64, _swap_halves(w_kr), z64], axis=1)
    wq3 = w_uq.reshape(Q_LORA, HEADS, HEAD_DIM + ROPE_DIM)
    rp = wq3[:, :, HEAD_DIM:]
    zq = jnp.zeros((Q_LORA, HEADS, LANES - ROPE_DIM), w_uq.dtype)
    w_q = jnp.concatenate([
        jnp.concatenate([wq3[:, :, :HEAD_DIM], rp, zq], axis=-1).reshape(Q_LORA, HEADS * QK_PAD),
        jnp.concatenate([_swap_halves(rp), zq], axis=-1).reshape(Q_LORA, HEADS * LANES)], axis=1)
    return dict(
        qkv=w_in.astype(BF16),
        lat=w_lat.astype(BF16),
        gate=w_in[:, o + Q_LORA + KV_LORA + ROPE_DIM:].astype(BF16),
        q=w_q.astype(BF16),
        qt=w_q[:, :HEADS * QK_PAD].T.astype(BF16),
        uk=w_uk.astype(BF16),
        uk_t=w_uk.reshape(KV_LORA, HEADS, HEAD_DIM).transpose(1, 2, 0).astype(BF16),
        uv=w_uv.astype(BF16),
        uvt=w_uv.T.astype(BF16),
        branch=w_branch.astype(BF16),
        out=w_out.astype(BF16),
        up=w_up.astype(BF16),
        down=w_down.astype(BF16),
        ple_gate=w_ple_gate.astype(BF16),
        ple=w_ple.astype(BF16),
    )


def _tile(t, want):
    return want if t % want == 0 else t


def _layer(x, ple, past, w, g, tri):
    b, n, d = x.shape
    t = b * n
    xf = x.reshape(t, d)
    tm = _tile(t, 1024)
    tm_s = _tile(t, 512)
    past_len = 0 if past is None else past[0].shape[1]
    if n >= tm:
        pos = jnp.arange(n) + past_len
    else:
        pos = jnp.tile(jnp.arange(n) + past_len, tm // n)
    cos2, sin2 = _rope_tables(pos)

    h, qkv_b, k_f, v_f = _qkv(xf, g['mix_pre'], w['qkv'], tm_s)
    cq, ckv_f, ckv_b, kr_f, kr_b = _latent(h, w['lat'], g['q'], g['kv'], cos2, sin2, tm)
    qkv_b = qkv_b.reshape(b, n, -1)

    if past is None:
        blk = _tile(n, 512)
        qt = _mla_qt(cq, w['qt'], cos2[:, :ROPE_DIM // 2].T, sin2[:, ROPE_DIM // 2:ROPE_DIM].T, blk)
        k_mla, vt = _mla_kvt(ckv_b, kr_b, w['uk'], w['uvt'], blk)
        o_sb, gates = _sb_prompt_gates(qkv_b, tri, h, w['gate'])
        o_mla = _mla_prompt(qt, k_mla.reshape(b, n, -1), vt.reshape(b, n // blk, SB_WIDTH, blk), b)
    else:
        gates = _gates(h, w['gate'], tm, 1024)
        q_mla = _mla_q(cq, w['q'], cos2, sin2, tm_s).reshape(b, n, -1)
        o_sb = _sb_decode(qkv_b, past[0], past[1], tri)
        o_mla = _mla_decode(q_mla, past[2], past[3], ckv_b.reshape(b, n, -1), kr_b.reshape(b, n, -1),
                            w['uk_t'], w['uv'])

    x1 = _mix(o_sb.reshape(t, -1), o_mla.reshape(t, -1), gates, w['branch'], w['out'], xf, g['mix_post'], tm_s)
    x2 = _ffn(x1, g['ffn_pre'], w['up'], w['down'], g['ffn_post'], tm_s, 1024)
    x3 = _ple(x2, ple.reshape(t, -1), g['ple_gate'], w['ple_gate'], w['ple'], g['ple_post'], tm_s)
    state = (k_f.reshape(1, b, n, HEADS, HEAD_DIM), v_f.reshape(1, b, n, HEADS, HEAD_DIM),
             ckv_f.reshape(1, b, n, KV_LORA), kr_f.reshape(1, b, n, ROPE_DIM))
    return x3.reshape(b, n, d), state


def kernel(x_prompt, x_sample, cache_sb_k, cache_sb_v, cache_mla_ckv, cache_mla_krope, p_prompt, p_sample, g_mix_pre, w_in, g_q, w_uq, g_kv, w_uk, w_uv, w_branch, w_out, g_mix_post, g_ffn_pre, w_up, w_down, g_ffn_post, g_ple_gate, w_ple_gate, w_ple, g_ple_post):
    assert w_in.shape[0] == 1, "single layer"
    w = _prep_weights(w_in[0], w_uq[0], w_uk[0], w_uv[0], w_branch[0], w_out[0], w_up[0], w_down[0],
                      w_ple_gate[0], w_ple[0])
    g = dict(mix_pre=g_mix_pre, q=g_q, kv=g_kv, mix_post=g_mix_post, ffn_pre=g_ffn_pre,
             ffn_post=g_ffn_post, ple_gate=g_ple_gate, ple_post=g_ple_post)
    idx = jnp.arange(KEY_BLOCK)
    lower = -(idx[:, None] > idx[None, :]).astype(BF16)
    tri = jnp.concatenate([lower, jnp.full((KEY_BLOCK, KEY_BLOCK), -1, BF16)], axis=1)
    tri = jnp.concatenate([tri, tri], axis=0)

    yp, sp = _layer(x_prompt, p_prompt[0], None, w, g, tri)
    past = (cache_sb_k[0], cache_sb_v[0], cache_mla_ckv[0], cache_mla_krope[0])
    ys, ss = _layer(x_sample, p_sample[0], past, w, g, tri)
    return (yp, ys) + sp + ss
```

```python
import functools

import jax
import jax.numpy as jnp
from jax import lax
from jax.experimental import pallas as pl
from jax.experimental.pallas import tpu as pltpu

F32 = jnp.float32
BF16 = jnp.bfloat16

D_MODEL = 2048
CHUNK = 64
PLE_DIM = 256
HEADS = 8
HEAD_DIM = 128
SB_WIDTH = HEADS * HEAD_DIM
ROPE_DIM = 64
Q_LORA = 512
KV_LORA = 512
QK_PAD = 256
D_FF = 4 * D_MODEL
ROPE_THETA = 10000.0
EPS = 1e-6
SB_SCALE = HEAD_DIM ** -0.5
MLA_SCALE = (HEAD_DIM + ROPE_DIM) ** -0.5
LOG2E = 1.4426950408889634

LANES = 128
KEY_BLOCK = 128
SB_EXIT = -104.0
MASKED = -1e30
VMEM_LIMIT = 56 * 1024 * 1024
FFN_VMEM_LIMIT = 62 * 1024 * 1024
FFN_TILE = 2048


def _params(*sem, vmem=VMEM_LIMIT):
    return pltpu.CompilerParams(dimension_semantics=sem, vmem_limit_bytes=vmem)


def _rms(xf, g):
    ms = jnp.mean(xf * xf, axis=-1, keepdims=True)
    return xf * lax.rsqrt(ms + EPS) * g


def _dot(a, b):
    return jnp.dot(a, b, preferred_element_type=F32)


def _dot_nt(a, b):
    return lax.dot_general(a, b, (((1,), (1,)), ((), ())), preferred_element_type=F32)


def _norm_kernel(x_ref, g_ref, o_ref):
    o_ref[...] = _rms(x_ref[...], g_ref[...]).astype(BF16)


def _norm(x, g, tm):
    t, d = x.shape
    return pl.pallas_call(
        _norm_kernel,
        grid=(t // tm,),
        in_specs=[pl.BlockSpec((tm, d), lambda i: (i, 0)),
                  pl.BlockSpec((1, d), lambda i: (0, 0))],
        out_specs=pl.BlockSpec((tm, d), lambda i: (i, 0)),
        out_shape=jax.ShapeDtypeStruct((t, d), BF16),
        compiler_params=_params("parallel"),
        name="pre_norm",
    )(x, g)


def _qkv_kernel(h_ref, w_ref, qkv_ref, kf_ref, vf_ref):
    j = pl.program_id(1)
    acc = _dot(h_ref[...], w_ref[...])
    qkv_ref[...] = (acc * jnp.where(j == 0, SB_SCALE, 1.0)).astype(BF16)

    def store_state(ref):
        for h in range(HEADS):
            ref[pl.ds(h, acc.shape[0], stride=HEADS), :] = acc[:, _head_cols(h)]

    pl.when(j == 1)(lambda: store_state(kf_ref))
    pl.when(j == 2)(lambda: store_state(vf_ref))


def _qkv(h, w, tm):
    t, d = h.shape
    state = pl.BlockSpec((tm * HEADS, HEAD_DIM), lambda i, j: (i, 0))
    return pl.pallas_call(
        _qkv_kernel,
        grid=(t // tm, 3),
        in_specs=[pl.BlockSpec((tm, d), lambda i, j: (i, 0)),
                  pl.BlockSpec((d, SB_WIDTH), lambda i, j: (0, j))],
        out_specs=[pl.BlockSpec((tm, SB_WIDTH), lambda i, j: (i, j)), state, state],
        out_shape=[jax.ShapeDtypeStruct((t, 3 * SB_WIDTH), BF16),
                   jax.ShapeDtypeStruct((t * HEADS, HEAD_DIM), F32),
                   jax.ShapeDtypeStruct((t * HEADS, HEAD_DIM), F32)],
        compiler_params=_params("parallel", "arbitrary"),
        name="sb_qkv_proj",
    )(h, w)


def _latent_kernel(h_ref, w_ref, gq_ref, gkv_ref, cos_ref, sin_ref,
                   cq_ref, ckvf_ref, ckvb_ref, krf_ref, krb_ref):
    acc = _dot(h_ref[...], w_ref[...])
    cq_ref[...] = _rms(acc[:, :Q_LORA], gq_ref[...]).astype(BF16)
    ckv = _rms(acc[:, Q_LORA:Q_LORA + KV_LORA], gkv_ref[...])
    ckvf_ref[...] = ckv
    ckvb_ref[...] = ckv.astype(BF16)
    base = Q_LORA + KV_LORA
    kr = acc[:, base:base + LANES] * cos_ref[...] + acc[:, base + LANES:base + 2 * LANES] * sin_ref[...]
    krf_ref[...] = kr[:, :ROPE_DIM]
    krb_ref[...] = kr[:, :ROPE_DIM].astype(BF16)


def _latent(h, w, gq, gkv, cos2, sin2, tm):
    t, d = h.shape
    n_tab = cos2.shape[0] // tm
    row = lambda i: (i, 0)
    const = lambda i: (0, 0)
    tab = lambda i: (i % n_tab, 0)
    wn = w.shape[1]
    return pl.pallas_call(
        _latent_kernel,
        grid=(t // tm,),
        in_specs=[pl.BlockSpec((tm, d), row), pl.BlockSpec((d, wn), const),
                  pl.BlockSpec((1, Q_LORA), const), pl.BlockSpec((1, KV_LORA), const),
                  pl.BlockSpec((tm, LANES), tab), pl.BlockSpec((tm, LANES), tab)],
        out_specs=[pl.BlockSpec((tm, Q_LORA), row), pl.BlockSpec((tm, KV_LORA), row),
                   pl.BlockSpec((tm, KV_LORA), row), pl.BlockSpec((tm, ROPE_DIM), row),
                   pl.BlockSpec((tm, ROPE_DIM), row)],
        out_shape=[jax.ShapeDtypeStruct((t, Q_LORA), BF16),
                   jax.ShapeDtypeStruct((t, KV_LORA), F32),
                   jax.ShapeDtypeStruct((t, KV_LORA), BF16),
                   jax.ShapeDtypeStruct((t, ROPE_DIM), F32),
                   jax.ShapeDtypeStruct((t, ROPE_DIM), BF16)],
        compiler_params=_params("parallel"),
        name="mla_latent_proj",
    )(h, w, gq, gkv, cos2, sin2)


def _mla_q_kernel(cq_ref, w_ref, cos_ref, sin_ref, q_ref):
    acc = _dot(cq_ref[...], w_ref[...])
    cos, sin = cos_ref[...], sin_ref[...]
    for h in range(HEADS):
        b = h * QK_PAD
        q_ref[:, b:b + LANES] = (acc[:, b:b + LANES] * MLA_SCALE).astype(BF16)
        sw = HEADS * QK_PAD + h * LANES
        rot = acc[:, b + LANES:b + QK_PAD] * cos + acc[:, sw:sw + LANES] * sin
        q_ref[:, b + LANES:b + QK_PAD] = (rot * MLA_SCALE).astype(BF16)


def _mla_q(cq, w, cos2, sin2, tm):
    t = cq.shape[0]
    n_tab = cos2.shape[0] // tm
    row = lambda i: (i, 0)
    tab = lambda i: (i % n_tab, 0)
    return pl.pallas_call(
        _mla_q_kernel,
        grid=(t // tm,),
        in_specs=[pl.BlockSpec((tm, Q_LORA), row),
                  pl.BlockSpec(w.shape, lambda i: (0, 0)),
                  pl.BlockSpec((tm, LANES), tab), pl.BlockSpec((tm, LANES), tab)],
        out_specs=pl.BlockSpec((tm, HEADS * QK_PAD), row),
        out_shape=jax.ShapeDtypeStruct((t, HEADS * QK_PAD), BF16),
        compiler_params=_params("parallel"),
        name="mla_q_proj",
    )(cq, w, cos2, sin2)


def _mla_qt_kernel(cq_ref, wt_ref, cos_ref, sin_ref, qt_ref):
    acc = _dot_nt(wt_ref[...], cq_ref[...])
    cos, sin = cos_ref[...], sin_ref[...]
    half = ROPE_DIM // 2
    scale = MLA_SCALE * LOG2E
    for h in range(HEADS):
        b = h * QK_PAD
        r = b + HEAD_DIM
        qt_ref[b:r, :] = (acc[b:r] * scale).astype(BF16)
        x1, x2 = acc[r:r + half], acc[r + half:r + ROPE_DIM]
        qt_ref[r:r + half, :] = ((x1 * cos - x2 * sin) * scale).astype(BF16)
        qt_ref[r + half:r + ROPE_DIM, :] = ((x1 * sin + x2 * cos) * scale).astype(BF16)
        qt_ref[r + ROPE_DIM:b + QK_PAD, :] = jnp.zeros((QK_PAD - HEAD_DIM - ROPE_DIM, acc.shape[1]), BF16)


def _mla_qt(cq, wt, cos_t, sin_t, tm):
    t = cq.shape[0]
    n_tab = cos_t.shape[1] // tm
    tab = lambda i: (0, i % n_tab)
    half = ROPE_DIM // 2
    return pl.pallas_call(
        _mla_qt_kernel,
        grid=(t // tm,),
        in_specs=[pl.BlockSpec((tm, Q_LORA), lambda i: (i, 0)),
                  pl.BlockSpec(wt.shape, lambda i: (0, 0)),
                  pl.BlockSpec((half, tm), tab), pl.BlockSpec((half, tm), tab)],
        out_specs=pl.BlockSpec((HEADS * QK_PAD, tm), lambda i: (0, i)),
        out_shape=jax.ShapeDtypeStruct((HEADS * QK_PAD, t), BF16),
        compiler_params=_params("parallel"),
        name="mla_q_proj_t",
    )(cq, wt, cos_t, sin_t)


def _mla_kvt_kernel(ckv_ref, kr_ref, wk_ref, wvt_ref, k_ref, vt_ref):
    ckv = ckv_ref[...]
    acc = _dot(ckv, wk_ref[...])
    kr = kr_ref[...]
    zero = jnp.zeros((kr.shape[0], QK_PAD - LANES - ROPE_DIM), BF16)
    for h in range(HEADS):
        b = h * QK_PAD
        k_ref[:, b:b + LANES] = acc[:, h * LANES:(h + 1) * LANES].astype(BF16)
        k_ref[:, b + LANES:b + LANES + ROPE_DIM] = kr
        k_ref[:, b + LANES + ROPE_DIM:b + QK_PAD] = zero
    vt_ref[...] = _dot_nt(wvt_ref[...], ckv).astype(BF16)


def _mla_kvt(ckv, kr, wk, wvt, tm):
    t = ckv.shape[0]
    row = lambda i: (i, 0)
    const = lambda i: (0, 0)
    return pl.pallas_call(
        _mla_kvt_kernel,
        grid=(t // tm,),
        in_specs=[pl.BlockSpec((tm, KV_LORA), row), pl.BlockSpec((tm, ROPE_DIM), row),
                  pl.BlockSpec(wk.shape, const), pl.BlockSpec(wvt.shape, const)],
        out_specs=[pl.BlockSpec((tm, HEADS * QK_PAD), row),
                   pl.BlockSpec((None, SB_WIDTH, tm), lambda i: (i, 0, 0))],
        out_shape=[jax.ShapeDtypeStruct((t, HEADS * QK_PAD), BF16),
                   jax.ShapeDtypeStruct((t // tm, SB_WIDTH, tm), BF16)],
        compiler_params=_params("parallel"),
        name="mla_kv_proj_t",
    )(ckv, kr, wk, wvt)


def _gate_kernel(h_ref, w_ref, o_ref):
    o_ref[...] = jax.nn.sigmoid(_dot(h_ref[...], w_ref[...])).astype(BF16)


def _gates(h, w, tm, tn):
    t, d = h.shape
    n = w.shape[1]
    return pl.pallas_call(
        _gate_kernel,
        grid=(t // tm, n // tn),
        in_specs=[pl.BlockSpec((tm, d), lambda i, j: (i, 0)),
                  pl.BlockSpec((d, tn), lambda i, j: (0, j))],
        out_specs=pl.BlockSpec((tm, tn), lambda i, j: (i, j)),
        out_shape=jax.ShapeDtypeStruct((t, n), BF16),
        compiler_params=_params("parallel", "arbitrary"),
        name="branch_gates",
    )(h, w)


def _mix_kernel(osb_ref, omla_ref, g_ref, wb_ref, wo_ref, x_ref, gn_ref, o_ref):
    d = x_ref.shape[1]
    merged = (g_ref[:, :d].astype(F32) * _dot(osb_ref[...], wb_ref[0])
              + g_ref[:, d:].astype(F32) * _dot(omla_ref[...], wb_ref[1]))
    o_ref[...] = x_ref[...] + _rms(_dot(merged.astype(BF16), wo_ref[...]), gn_ref[...])


def _mix(osb, omla, gates, wb, wo, x, gn, tm):
    t, d = x.shape
    row = lambda i: (i, 0)
    once = pl.Buffered(1)
    return pl.pallas_call(
        _mix_kernel,
        grid=(t // tm,),
        in_specs=[pl.BlockSpec((tm, SB_WIDTH), row), pl.BlockSpec((tm, SB_WIDTH), row),
                  pl.BlockSpec((tm, 2 * d), row),
                  pl.BlockSpec(wb.shape, lambda i: (0, 0, 0), pipeline_mode=once),
                  pl.BlockSpec(wo.shape, lambda i: (0, 0), pipeline_mode=once),
                  pl.BlockSpec((tm, d), row), pl.BlockSpec((1, d), lambda i: (0, 0))],
        out_specs=pl.BlockSpec((tm, d), row),
        out_shape=jax.ShapeDtypeStruct((t, d), F32),
        compiler_params=_params("parallel"),
        name="branch_mix_out",
    )(osb, omla, gates, wb, wo, x, gn)


def _ffn_kernel(x_ref, gpre_ref, wu_ref, wd_ref, gpost_ref, o_ref, h_scr, acc_scr):
    j = pl.program_id(1)

    @pl.when(j == 0)
    def _():
        h_scr[...] = _rms(x_ref[...], gpre_ref[...]).astype(BF16)
        acc_scr[...] = jnp.zeros_like(acc_scr)

    u = jnp.maximum(_dot(h_scr[...], wu_ref[...]), 0.0)
    acc_scr[...] += _dot((u * u).astype(BF16), wd_ref[...])

    @pl.when(j == pl.num_programs(1) - 1)
    def _():
        o_ref[...] = x_ref[...] + _rms(acc_scr[...], gpost_ref[...])


def _ffn(x, gpre, wu, wd, gpost, tm, tf):
    t, d = x.shape
    row = lambda i, j: (i, 0)
    const = lambda i, j: (0, 0)
    return pl.pallas_call(
        _ffn_kernel,
        grid=(t // tm, D_FF // tf),
        in_specs=[pl.BlockSpec((tm, d), row), pl.BlockSpec((1, d), const),
                  pl.BlockSpec((d, tf), lambda i, j: (0, j)),
                  pl.BlockSpec((tf, d), lambda i, j: (j, 0)),
                  pl.BlockSpec((1, d), const)],
        out_specs=pl.BlockSpec((tm, d), row),
        out_shape=jax.ShapeDtypeStruct((t, d), F32),
        scratch_shapes=[pltpu.VMEM((tm, d), BF16), pltpu.VMEM((tm, d), F32)],
        compiler_params=_params("parallel", "arbitrary", vmem=FFN_VMEM_LIMIT),
        name="sqrelu_ffn",
    )(x, gpre, wu, wd, gpost)


def _ple_kernel(x_ref, p_ref, gg_ref, wg_ref, wp_ref, gpost_ref, o_ref):
    x = x_ref[...]
    gate = jax.nn.sigmoid(_dot(_rms(x, gg_ref[...]).astype(BF16), wg_ref[...]))
    pe = _dot(p_ref[...].astype(BF16), wp_ref[...])
    o_ref[...] = x + _rms(pe * gate, gpost_ref[...])


def _ple(x, p, gg, wg, wp, gpost, tm):
    t, d = x.shape
    row = lambda i: (i, 0)
    const = lambda i: (0, 0)
    return pl.pallas_call(
        _ple_kernel,
        grid=(t // tm,),
        in_specs=[pl.BlockSpec((tm, d), row), pl.BlockSpec((tm, PLE_DIM), row),
                  pl.BlockSpec((1, d), const), pl.BlockSpec((d, d), const),
                  pl.BlockSpec((PLE_DIM, d), const), pl.BlockSpec((1, d), const)],
        out_specs=pl.BlockSpec((tm, d), row),
        out_shape=jax.ShapeDtypeStruct((t, d), F32),
        compiler_params=_params("parallel"),
        name="ple_embed",
    )(x, p, gg, wg, wp, gpost)


def _sb_step(qs, ks, vs, tri, carry, mask):
    rows = qs[0].shape[0]
    z = jnp.concatenate([_dot_nt(q, k) for q, k in zip(qs, ks)], axis=0)
    sp = jnp.maximum(z, 0.0) + jnp.log(1.0 + jnp.exp(-jnp.abs(z)))
    spm = sp if mask is None else jnp.where(mask, sp, 0.0)
    hi = spm.astype(BF16)
    lo = (spm - hi.astype(F32)).astype(BF16)
    cs = _dot(jnp.concatenate([hi, lo], axis=1), tri)
    after = cs[:, :KEY_BLOCK] if carry is None else cs[:, :KEY_BLOCK] + carry
    w = jnp.exp((z - sp) + after)
    if mask is not None:
        w = jnp.where(mask, w, 0.0)
    w = w.astype(BF16)
    pvs = [_dot(w[i * rows:(i + 1) * rows], v) for i, v in enumerate(vs)]
    carry = cs[:, KEY_BLOCK:] if carry is None else carry + cs[:, KEY_BLOCK:]
    return carry, pvs


def _sb_walk(first_blocks, later_blocks, tri_ref, carry_scr, acc_scr, o_ref, alongside=None):
    def run(chains, carry, mask, first):
        carry, pvs = _sb_step([c[2] for c in chains], [c[3] for c in chains], [c[4] for c in chains],
                              tri_ref[...], carry, mask)
        carry_scr[...] = carry
        for (rows, cols, _, _, _), pv in zip(chains, pvs):
            if first:
                acc_scr[rows, cols] = pv
            else:
                acc_scr[rows, cols] += pv
        return jnp.max(carry)

    def step0():
        chains, mask = first_blocks()
        return run(chains, None, mask, True)

    def step(j):
        chains, valid = later_blocks(j)
        rows = chains[0][2].shape[0]
        carry = jnp.concatenate(
            [jnp.where(ok, carry_scr[i * rows:(i + 1) * rows, :], MASKED) for i, ok in enumerate(valid)],
            axis=0)
        return run(chains, carry, None, False)

    if alongside is None:
        start = (jnp.int32(1), step0())
    else:
        finish = alongside()
        step0()
        step(1)
        start = (jnp.int32(3), step(2))
        finish()
    lax.while_loop(lambda st: st[1] > SB_EXIT, lambda st: (st[0] + 1, step(st[0])), start)
    o_ref[...] = acc_scr[...].astype(BF16)


def _head_cols(h):
    return slice(h * HEAD_DIM, (h + 1) * HEAD_DIM)


def _sb_prompt_kernel(q_ref, k_ref, v_ref, tri_ref, h_ref, wg_ref, o_ref, gate_ref, carry_scr, acc_scr,
                      *, rsub):
    bk = KEY_BLOCK
    base = pl.program_id(1) * rsub
    n_chain = rsub * HEADS
    r = lax.broadcasted_iota(jnp.int32, (n_chain * bk, bk), 0)
    c = lax.broadcasted_iota(jnp.int32, (n_chain * bk, bk), 1)
    causal = c < jnp.bitwise_and(r, bk - 1)

    def blocks(j):
        chains, valid = [], []
        for s in range(rsub):
            kb = base + s - j
            start = pl.multiple_of(jnp.maximum(kb, 0) * bk, bk)
            rows = slice(s * bk, (s + 1) * bk)
            for h in range(HEADS):
                cols = _head_cols(h)
                chains.append((rows, cols, q_ref[rows, cols],
                               k_ref[pl.ds(start, bk), cols], v_ref[pl.ds(start, bk), cols]))
                valid.append(kb >= 0)
        return chains, valid

    def gates():
        logits = _dot(h_ref[...], wg_ref[...])

        def finish():
            gate_ref[...] = jax.nn.sigmoid(logits).astype(BF16)
        return finish

    _sb_walk(lambda: (blocks(0)[0], causal), blocks, tri_ref, carry_scr, acc_scr, o_ref, alongside=gates)


GATE_TILE = 1024


def _sb_prompt_gates(qkv, tri, h, wg):
    b, n, _ = qkv.shape
    t, d = h.shape
    gw = wg.shape[1]
    bq = GATE_TILE * GATE_TILE // gw
    rsub = bq // KEY_BLOCK
    assert bq % KEY_BLOCK == 0 and n % bq == 0 and t % GATE_TILE == 0 and gw % GATE_TILE == 0
    nq = n // bq
    ncol = gw // GATE_TILE
    full = (None, n, SB_WIDTH)
    once = pl.Buffered(1)
    return pl.pallas_call(
        functools.partial(_sb_prompt_kernel, rsub=rsub),
        grid=(b, nq),
        in_specs=[pl.BlockSpec((None, bq, SB_WIDTH), lambda i, j: (i, j, 0)),
                  pl.BlockSpec(full, lambda i, j: (i, 0, 1), pipeline_mode=once),
                  pl.BlockSpec(full, lambda i, j: (i, 0, 2), pipeline_mode=once),
                  pl.BlockSpec(tri.shape, lambda i, j: (0, 0)),
                  pl.BlockSpec((GATE_TILE, d), lambda i, j: ((i * nq + j) // ncol, 0)),
                  pl.BlockSpec((d, GATE_TILE), lambda i, j: (0, (i * nq + j) % ncol))],
        out_specs=[pl.BlockSpec((None, bq, SB_WIDTH), lambda i, j: (i, j, 0)),
                   pl.BlockSpec((GATE_TILE, GATE_TILE), lambda i, j: ((i * nq + j) // ncol, (i * nq + j) % ncol))],
        out_shape=[jax.ShapeDtypeStruct((b, n, SB_WIDTH), BF16),
                   jax.ShapeDtypeStruct((t, gw), BF16)],
        scratch_shapes=[pltpu.VMEM((rsub * HEADS * KEY_BLOCK, KEY_BLOCK), F32),
                        pltpu.VMEM((bq, SB_WIDTH), F32)],
        compiler_params=_params("arbitrary", "arbitrary"),
        name="stickbreak_attn_gates",
    )(qkv, qkv, qkv, tri, h, wg)


def _sb_decode_kernel(q_ref, kn_ref, vn_ref, kp_ref, vp_ref, tri_ref, o_ref, carry_scr, acc_scr,
                      *, n, past):
    bk = KEY_BLOCK
    nb = past // bk
    assert n & (n - 1) == 0
    r = lax.broadcasted_iota(jnp.int32, (HEADS * n, bk), 0)
    c = lax.broadcasted_iota(jnp.int32, (HEADS * n, bk), 1)
    causal = c < jnp.bitwise_and(r, n - 1)
    zpad = jnp.zeros((bk - n, HEAD_DIM), BF16)
    rows = slice(0, n)

    def first():
        return [(rows, _head_cols(h), q_ref[:, _head_cols(h)],
                 jnp.concatenate([kn_ref[:, _head_cols(h)], zpad], axis=0),
                 jnp.concatenate([vn_ref[:, _head_cols(h)], zpad], axis=0))
                for h in range(HEADS)], causal

    def later(j):
        kb = nb - j
        start = jnp.maximum(kb, 0) * (bk * HEADS)
        chains = []
        for h in range(HEADS):
            sl = pl.ds(start + h, bk, stride=HEADS)
            chains.append((rows, _head_cols(h), q_ref[:, _head_cols(h)],
                           kp_ref[sl, :].astype(BF16), vp_ref[sl, :].astype(BF16)))
        return chains, [kb >= 0] * HEADS

    _sb_walk(first, later, tri_ref, carry_scr, acc_scr, o_ref)


def _sb_decode(qkv, k_past, v_past, tri):
    b, n, _ = qkv.shape
    past = k_past.shape[1]
    assert past % KEY_BLOCK == 0 and n <= KEY_BLOCK and n % 16 == 0
    new = (None, n, SB_WIDTH)
    old = (None, past * HEADS, HEAD_DIM)
    return pl.pallas_call(
        functools.partial(_sb_decode_kernel, n=n, past=past),
        grid=(b,),
        in_specs=[pl.BlockSpec(new, lambda i: (i, 0, 0)),
                  pl.BlockSpec(new, lambda i: (i, 0, 1)),
                  pl.BlockSpec(new, lambda i: (i, 0, 2)),
                  pl.BlockSpec(old, lambda i: (i, 0, 0)),
                  pl.BlockSpec(old, lambda i: (i, 0, 0)),
                  pl.BlockSpec(tri.shape, lambda i: (0, 0))],
        out_specs=pl.BlockSpec(new, lambda i: (i, 0, 0)),
        out_shape=jax.ShapeDtypeStruct((b, n, SB_WIDTH), BF16),
        scratch_shapes=[pltpu.VMEM((HEADS * n, KEY_BLOCK), F32),
                        pltpu.VMEM((n, SB_WIDTH), F32)],
        compiler_params=_params("parallel"),
        name="stickbreak_attn_decode",
    )(qkv, qkv, qkv, k_past.reshape(b, past * HEADS, HEAD_DIM),
      v_past.reshape(b, past * HEADS, HEAD_DIM), tri)


CHUNK_SHIFT = CHUNK.bit_length() - 1
assert 1 << CHUNK_SHIFT == CHUNK


def _chunk(pos):
    return lax.shift_right_arithmetic(pos, CHUNK_SHIFT)


def _mla_prompt_kernel(qt_ref, k_ref, vt_ref, o_ref, s_scr, *, bk):
    bq = 2 * bk
    qi = pl.program_id(2)

    def scores(kb, slot):
        s0 = pl.multiple_of(kb * bk, bk)
        s_scr[slot] = _dot(k_ref[pl.ds(s0, bk), :], qt_ref[...])

    def absorb(st, s, kb):
        m, l, acc = st
        m_new = jnp.maximum(m, jnp.max(s, axis=0, keepdims=True))
        alpha = jnp.exp2(m - m_new)
        p = jnp.exp2(s - m_new)
        l = alpha * l + jnp.sum(p, axis=0, keepdims=True)
        acc = alpha * acc + _dot(vt_ref[kb], p.astype(BF16))
        return m_new, l, acc

    def pair(i, st):
        kb = 2 * i
        scores(kb + 1, 1)
        st = absorb(st, s_scr[0], kb)
        scores(kb + 2, 0)
        return absorb(st, s_scr[1], kb + 1)

    init = (jnp.full((1, bq), MASKED, F32), jnp.zeros((1, bq), F32), jnp.zeros((HEAD_DIM, bq), F32))
    scores(0, 0)
    st = lax.fori_loop(0, qi, pair, init)
    late = (slice(None), slice(bk, bq))
    s_late = _dot(k_ref[pl.ds(pl.multiple_of((2 * qi + 1) * bk, bk), bk), :], qt_ref[late])
    vis = (_chunk(lax.broadcasted_iota(jnp.int32, (bk, bq), 0))
           <= _chunk(lax.broadcasted_iota(jnp.int32, (bk, bq), 1)))
    m, l, acc = absorb(st, jnp.where(vis, s_scr[0], MASKED), 2 * qi)
    _, l2, acc2 = absorb((m[late], l[late], acc[late]), jnp.where(vis[:, :bk], s_late, MASKED), 2 * qi + 1)
    l = jnp.concatenate([l[:, :bk], l2], axis=1)
    acc = jnp.concatenate([acc[:, :bk], acc2], axis=1)
    o_ref[...] = (acc / l).T.astype(BF16)


def _mla_prompt(qt, k, vt, b):
    n = k.shape[1]
    bk = vt.shape[-1]
    bq = 2 * bk
    assert bk % CHUNK == 0 and n % bq == 0
    nq = n // bq
    return pl.pallas_call(
        functools.partial(_mla_prompt_kernel, bk=bk),
        grid=(b, HEADS, nq),
        in_specs=[pl.BlockSpec((QK_PAD, bq), lambda i, h, j: (h, i * nq + j)),
                  pl.BlockSpec((None, n, QK_PAD), lambda i, h, j: (i, 0, h)),
                  pl.BlockSpec((None, n // bk, HEAD_DIM, bk), lambda i, h, j: (i, 0, h, 0))],
        out_specs=pl.BlockSpec((None, bq, HEAD_DIM), lambda i, h, j: (i, j, h)),
        out_shape=jax.ShapeDtypeStruct((b, n, SB_WIDTH), BF16),
        scratch_shapes=[pltpu.VMEM((2, bk, bq), F32)],
        compiler_params=_params("parallel", "parallel", "arbitrary"),
        name="mla_attn",
    )(qt, k, vt)


def _mla_decode_kernel(q_ref, ckv_old_ref, kr_old_ref, ckv_new_ref, kr_new_ref, wuk_ref, wuv_ref, o_ref,
                       *, n, past):
    rows = HEADS * n
    ck_old = ckv_old_ref[...].astype(BF16)
    kr_old = kr_old_ref[...].astype(BF16)
    ck_new = jnp.concatenate([ckv_new_ref[...], jnp.zeros((LANES - n, KV_LORA), BF16)], axis=0)
    kr_new = jnp.concatenate([kr_new_ref[...], jnp.zeros((LANES - n, ROPE_DIM), BF16)], axis=0)
    qa = jnp.concatenate(
        [_dot(q_ref[:, h * QK_PAD:h * QK_PAD + HEAD_DIM], wuk_ref[h]) for h in range(HEADS)],
        axis=0).astype(BF16)
    qr = jnp.concatenate(
        [q_ref[:, h * QK_PAD + HEAD_DIM:h * QK_PAD + HEAD_DIM + ROPE_DIM] for h in range(HEADS)], axis=0)
    s_old = _dot_nt(qa, ck_old) + _dot_nt(qr, kr_old)
    r = jnp.bitwise_and(lax.broadcasted_iota(jnp.int32, (rows, LANES), 0), n - 1)
    c = lax.broadcasted_iota(jnp.int32, (rows, LANES), 1)
    vis = jnp.logical_and(c < n, _chunk(past + c) <= _chunk(past + r))
    s_new = jnp.where(vis, _dot_nt(qa, ck_new) + _dot_nt(qr, kr_new), MASKED)
    m = jnp.maximum(jnp.max(s_old, axis=-1, keepdims=True), jnp.max(s_new, axis=-1, keepdims=True))
    p_old = jnp.exp(s_old - m)
    p_new = jnp.exp(s_new - m)
    l = jnp.sum(p_old, axis=-1, keepdims=True) + jnp.sum(p_new, axis=-1, keepdims=True)
    ctx = ((_dot(p_old.astype(BF16), ck_old) + _dot(p_new.astype(BF16), ck_new)) / l).astype(BF16)
    for h in range(HEADS):
        o_ref[:, _head_cols(h)] = _dot(ctx[h * n:(h + 1) * n], wuv_ref[:, _head_cols(h)]).astype(BF16)


def _mla_decode(q, ckv_old, kr_old, ckv_new, kr_new, wuk_t, wuv):
    b, n, _ = q.shape
    past = ckv_old.shape[1]
    assert n <= LANES and n % 16 == 0 and n & (n - 1) == 0
    spec = lambda rows, width: pl.BlockSpec((None, rows, width), lambda i: (i, 0, 0))
    return pl.pallas_call(
        functools.partial(_mla_decode_kernel, n=n, past=past),
        grid=(b,),
        in_specs=[spec(n, HEADS * QK_PAD), spec(past, KV_LORA), spec(past, ROPE_DIM),
                  spec(n, KV_LORA), spec(n, ROPE_DIM),
                  pl.BlockSpec(wuk_t.shape, lambda i: (0, 0, 0)), pl.BlockSpec(wuv.shape, lambda i: (0, 0))],
        out_specs=spec(n, SB_WIDTH),
        out_shape=jax.ShapeDtypeStruct((b, n, SB_WIDTH), BF16),
        compiler_params=_params("parallel"),
        name="mla_attn_decode",
    )(q, ckv_old, kr_old, ckv_new, kr_new, wuk_t, wuv)


def _rope_tables(pos):
    half = ROPE_DIM // 2
    freqs = ROPE_THETA ** (-jnp.arange(half, dtype=F32) / half)
    ang = pos.astype(F32)[:, None] * freqs[None, :]
    cos, sin = jnp.cos(ang), jnp.sin(ang)
    z = jnp.zeros((pos.shape[0], LANES - ROPE_DIM), F32)
    return jnp.concatenate([cos, cos, z], axis=1), jnp.concatenate([-sin, sin, z], axis=1)


def _swap_halves(w):
    half = w.shape[-1] // 2
    return jnp.concatenate([w[..., half:], w[..., :half]], axis=-1)


def _prep_weights(w_in, w_uq, w_uk, w_uv, w_branch, w_out, w_up, w_down, w_ple_gate, w_ple):
    d = w_in.shape[0]
    o = 3 * SB_WIDTH
    w_kr = w_in[:, o + Q_LORA + KV_LORA:o + Q_LORA + KV_LORA + ROPE_DIM]
    z64 = jnp.zeros((d, LANES - ROPE_DIM), w_in.dtype)
    w_lat = jnp.concatenate([w_in[:, o:o + Q_LORA + KV_LORA], w_kr, z64, _swap_halves(w_kr), z64], axis=1)
    wq3 = w_uq.reshape(Q_LORA, HEADS, HEAD_DIM + ROPE_DIM)
    rp = wq3[:, :, HEAD_DIM:]
    zq = jnp.zeros((Q_LORA, HEADS, LANES - ROPE_DIM), w_uq.dtype)
    w_q = jnp.concatenate([
        jnp.concatenate([wq3[:, :, :HEAD_DIM], rp, zq], axis=-1).reshape(Q_LORA, HEADS * QK_PAD),
        jnp.concatenate([_swap_halves(rp), zq], axis=-1).reshape(Q_LORA, HEADS * LANES)], axis=1)
    return dict(
        qkv=w_in.astype(BF16),
        lat=w_lat.astype(BF16),
        gate=w_in[:, o + Q_LORA + KV_LORA + ROPE_DIM:].astype(BF16),
        q=w_q.astype(BF16),
        qt=w_q[:, :HEADS * QK_PAD].T.astype(BF16),
        uk=w_uk.astype(BF16),
        uk_t=w_uk.reshape(KV_LORA, HEADS, HEAD_DIM).transpose(1, 2, 0).astype(BF16),
        uv=w_uv.astype(BF16),
        uvt=w_uv.T.astype(BF16),
        branch=w_branch.astype(BF16),
        out=w_out.astype(BF16),
        up=w_up.astype(BF16),
        down=w_down.astype(BF16),
        ple_gate=w_ple_gate.astype(BF16),
        ple=w_ple.astype(BF16),
    )


def _tile(t, want):
    return want if t % want == 0 else t


def _layer(x, ple, past, w, g, tri):
    b, n, d = x.shape
    t = b * n
    xf = x.reshape(t, d)
    tm = _tile(t, 1024)
    tm_s = _tile(t, 512)
    past_len = 0 if past is None else past[0].shape[1]
    if n >= tm:
        pos = jnp.arange(n) + past_len
    else:
        pos = jnp.tile(jnp.arange(n) + past_len, tm // n)
    cos2, sin2 = _rope_tables(pos)

    h = _norm(xf, g['mix_pre'], tm)
    qkv_b, k_f, v_f = _qkv(h, w['qkv'], tm)
    cq, ckv_f, ckv_b, kr_f, kr_b = _latent(h, w['lat'], g['q'], g['kv'], cos2, sin2, tm)
    qkv_b = qkv_b.reshape(b, n, -1)

    if past is None:
        blk = _tile(n, 512)
        qt = _mla_qt(cq, w['qt'], cos2[:, :ROPE_DIM // 2].T, sin2[:, ROPE_DIM // 2:ROPE_DIM].T, blk)
        k_mla, vt = _mla_kvt(ckv_b, kr_b, w['uk'], w['uvt'], blk)
        o_sb, gates = _sb_prompt_gates(qkv_b, tri, h, w['gate'])
        o_mla = _mla_prompt(qt, k_mla.reshape(b, n, -1), vt.reshape(b, n // blk, SB_WIDTH, blk), b)
    else:
        gates = _gates(h, w['gate'], tm, 1024)
        q_mla = _mla_q(cq, w['q'], cos2, sin2, tm_s).reshape(b, n, -1)
        o_sb = _sb_decode(qkv_b, past[0], past[1], tri)
        o_mla = _mla_decode(q_mla, past[2], past[3], ckv_b.reshape(b, n, -1), kr_b.reshape(b, n, -1),
                            w['uk_t'], w['uv'])

    x1 = _mix(o_sb.reshape(t, -1), o_mla.reshape(t, -1), gates, w['branch'], w['out'], xf, g['mix_post'], tm_s)
    x2 = _ffn(x1, g['ffn_pre'], w['up'], w['down'], g['ffn_post'], tm_s, FFN_TILE)
    x3 = _ple(x2, ple.reshape(t, -1), g['ple_gate'], w['ple_gate'], w['ple'], g['ple_post'], tm_s)
    state = (k_f.reshape(1, b, n, HEADS, HEAD_DIM), v_f.reshape(1, b, n, HEADS, HEAD_DIM),
             ckv_f.reshape(1, b, n, KV_LORA), kr_f.reshape(1, b, n, ROPE_DIM))
    return x3.reshape(b, n, d), state


def kernel(x_prompt, x_sample, cache_sb_k, cache_sb_v, cache_mla_ckv, cache_mla_krope, p_prompt, p_sample, g_mix_pre, w_in, g_q, w_uq, g_kv, w_uk, w_uv, w_branch, w_out, g_mix_post, g_ffn_pre, w_up, w_down, g_ffn_post, g_ple_gate, w_ple_gate, w_ple, g_ple_post):
    assert w_in.shape[0] == 1, "single layer"
    w = _prep_weights(w_in[0], w_uq[0], w_uk[0], w_uv[0], w_branch[0], w_out[0], w_up[0], w_down[0],
                      w_ple_gate[0], w_ple[0])
    g = dict(mix_pre=g_mix_pre, q=g_q, kv=g_kv, mix_post=g_mix_post, ffn_pre=g_ffn_pre,
             ffn_post=g_ffn_post, ple_gate=g_ple_gate, ple_post=g_ple_post)
    idx = jnp.arange(KEY_BLOCK)
    lower = -(idx[:, None] > idx[None, :]).astype(BF16)
    tri = jnp.concatenate([lower, jnp.full((KEY_BLOCK, KEY_BLOCK), -1, BF16)], axis=1)
    tri = jnp.concatenate([tri, tri], axis=0)

    yp, sp = _layer(x_prompt, p_prompt[0], None, w, g, tri)
    past = (cache_sb_k[0], cache_sb_v[0], cache_mla_ckv[0], cache_mla_krope[0])
    ys, ss = _layer(x_sample, p_sample[0], past, w, g, tri)
    return (yp, ys) + sp + ss
```

```python
import functools

import jax
import jax.numpy as jnp
from jax import lax
from jax.experimental import pallas as pl
from jax.experimental.pallas import tpu as pltpu

F32 = jnp.float32
BF16 = jnp.bfloat16

D_MODEL = 2048
CHUNK = 64
PLE_DIM = 256
HEADS = 8
HEAD_DIM = 128
SB_WIDTH = HEADS * HEAD_DIM
ROPE_DIM = 64
Q_LORA = 512
KV_LORA = 512
QK_PAD = 256
D_FF = 4 * D_MODEL
ROPE_THETA = 10000.0
EPS = 1e-6
SB_SCALE = HEAD_DIM ** -0.5
MLA_SCALE = (HEAD_DIM + ROPE_DIM) ** -0.5
LOG2E = 1.4426950408889634

LANES = 128
KEY_BLOCK = 128
SB_EXIT = -104.0
MASKED = -1e30
VMEM_LIMIT = 56 * 1024 * 1024
FFN_VMEM_LIMIT = 62 * 1024 * 1024
FFN_TILE = 2048


def _params(*sem, vmem=VMEM_LIMIT):
    return pltpu.CompilerParams(dimension_semantics=sem, vmem_limit_bytes=vmem)


def _rms(xf, g):
    ms = jnp.mean(xf * xf, axis=-1, keepdims=True)
    return xf * lax.rsqrt(ms + EPS) * g


def _dot(a, b):
    return jnp.dot(a, b, preferred_element_type=F32)


def _dot_nt(a, b):
    return lax.dot_general(a, b, (((1,), (1,)), ((), ())), preferred_element_type=F32)


def _norm_kernel(x_ref, g_ref, o_ref):
    o_ref[...] = _rms(x_ref[...], g_ref[...]).astype(BF16)


def _norm(x, g, tm):
    t, d = x.shape
    return pl.pallas_call(
        _norm_kernel,
        grid=(t // tm,),
        in_specs=[pl.BlockSpec((tm, d), lambda i: (i, 0)),
                  pl.BlockSpec((1, d), lambda i: (0, 0))],
        out_specs=pl.BlockSpec((tm, d), lambda i: (i, 0)),
        out_shape=jax.ShapeDtypeStruct((t, d), BF16),
        compiler_params=_params("parallel"),
        name="pre_norm",
    )(x, g)


def _qkv_kernel(h_ref, w_ref, qkv_ref, kf_ref, vf_ref):
    j = pl.program_id(1)
    acc = _dot(h_ref[...], w_ref[...])
    qkv_ref[...] = (acc * jnp.where(j == 0, SB_SCALE, 1.0)).astype(BF16)

    def store_state(ref):
        for h in range(HEADS):
            ref[pl.ds(h, acc.shape[0], stride=HEADS), :] = acc[:, _head_cols(h)]

    pl.when(j == 1)(lambda: store_state(kf_ref))
    pl.when(j == 2)(lambda: store_state(vf_ref))


def _qkv(h, w, tm):
    t, d = h.shape
    state = pl.BlockSpec((tm * HEADS, HEAD_DIM), lambda i, j: (i, 0))
    return pl.pallas_call(
        _qkv_kernel,
        grid=(t // tm, 3),
        in_specs=[pl.BlockSpec((tm, d), lambda i, j: (i, 0)),
                  pl.BlockSpec((d, SB_WIDTH), lambda i, j: (0, j))],
        out_specs=[pl.BlockSpec((None, tm, SB_WIDTH), lambda i, j: (j, i, 0)), state, state],
        out_shape=[jax.ShapeDtypeStruct((3, t, SB_WIDTH), BF16),
                   jax.ShapeDtypeStruct((t * HEADS, HEAD_DIM), F32),
                   jax.ShapeDtypeStruct((t * HEADS, HEAD_DIM), F32)],
        compiler_params=_params("parallel", "arbitrary"),
        name="sb_qkv_proj",
    )(h, w)


def _latent_kernel(h_ref, w_ref, gq_ref, gkv_ref, cos_ref, sin_ref,
                   cq_ref, ckvf_ref, ckvb_ref, krf_ref, krb_ref):
    acc = _dot(h_ref[...], w_ref[...])
    cq_ref[...] = _rms(acc[:, :Q_LORA], gq_ref[...]).astype(BF16)
    ckv = _rms(acc[:, Q_LORA:Q_LORA + KV_LORA], gkv_ref[...])
    ckvf_ref[...] = ckv
    ckvb_ref[...] = ckv.astype(BF16)
    base = Q_LORA + KV_LORA
    kr = acc[:, base:base + LANES] * cos_ref[...] + acc[:, base + LANES:base + 2 * LANES] * sin_ref[...]
    krf_ref[...] = kr[:, :ROPE_DIM]
    krb_ref[...] = kr[:, :ROPE_DIM].astype(BF16)


def _latent(h, w, gq, gkv, cos2, sin2, tm):
    t, d = h.shape
    n_tab = cos2.shape[0] // tm
    row = lambda i: (i, 0)
    const = lambda i: (0, 0)
    tab = lambda i: (i % n_tab, 0)
    wn = w.shape[1]
    return pl.pallas_call(
        _latent_kernel,
        grid=(t // tm,),
        in_specs=[pl.BlockSpec((tm, d), row), pl.BlockSpec((d, wn), const),
                  pl.BlockSpec((1, Q_LORA), const), pl.BlockSpec((1, KV_LORA), const),
                  pl.BlockSpec((tm, LANES), tab), pl.BlockSpec((tm, LANES), tab)],
        out_specs=[pl.BlockSpec((tm, Q_LORA), row), pl.BlockSpec((tm, KV_LORA), row),
                   pl.BlockSpec((tm, KV_LORA), row), pl.BlockSpec((tm, ROPE_DIM), row),
                   pl.BlockSpec((tm, ROPE_DIM), row)],
        out_shape=[jax.ShapeDtypeStruct((t, Q_LORA), BF16),
                   jax.ShapeDtypeStruct((t, KV_LORA), F32),
                   jax.ShapeDtypeStruct((t, KV_LORA), BF16),
                   jax.ShapeDtypeStruct((t, ROPE_DIM), F32),
                   jax.ShapeDtypeStruct((t, ROPE_DIM), BF16)],
        compiler_params=_params("parallel"),
        name="mla_latent_proj",
    )(h, w, gq, gkv, cos2, sin2)


def _mla_q_kernel(cq_ref, w_ref, cos_ref, sin_ref, q_ref):
    acc = _dot(cq_ref[...], w_ref[...])
    cos, sin = cos_ref[...], sin_ref[...]
    for h in range(HEADS):
        b = h * QK_PAD
        q_ref[:, b:b + LANES] = (acc[:, b:b + LANES] * MLA_SCALE).astype(BF16)
        sw = HEADS * QK_PAD + h * LANES
        rot = acc[:, b + LANES:b + QK_PAD] * cos + acc[:, sw:sw + LANES] * sin
        q_ref[:, b + LANES:b + QK_PAD] = (rot * MLA_SCALE).astype(BF16)


def _mla_q(cq, w, cos2, sin2, tm):
    t = cq.shape[0]
    n_tab = cos2.shape[0] // tm
    row = lambda i: (i, 0)
    tab = lambda i: (i % n_tab, 0)
    return pl.pallas_call(
        _mla_q_kernel,
        grid=(t // tm,),
        in_specs=[pl.BlockSpec((tm, Q_LORA), row),
                  pl.BlockSpec(w.shape, lambda i: (0, 0)),
                  pl.BlockSpec((tm, LANES), tab), pl.BlockSpec((tm, LANES), tab)],
        out_specs=pl.BlockSpec((tm, HEADS * QK_PAD), row),
        out_shape=jax.ShapeDtypeStruct((t, HEADS * QK_PAD), BF16),
        compiler_params=_params("parallel"),
        name="mla_q_proj",
    )(cq, w, cos2, sin2)


def _mla_qt_kernel(cq_ref, wt_ref, cos_ref, sin_ref, qt_ref):
    acc = _dot_nt(wt_ref[...], cq_ref[...])
    cos, sin = cos_ref[...], sin_ref[...]
    half = ROPE_DIM // 2
    scale = MLA_SCALE * LOG2E
    for h in range(HEADS):
        b = h * QK_PAD
        r = b + HEAD_DIM
        qt_ref[b:r, :] = (acc[b:r] * scale).astype(BF16)
        x1, x2 = acc[r:r + half], acc[r + half:r + ROPE_DIM]
        qt_ref[r:r + half, :] = ((x1 * cos - x2 * sin) * scale).astype(BF16)
        qt_ref[r + half:r + ROPE_DIM, :] = ((x1 * sin + x2 * cos) * scale).astype(BF16)
        qt_ref[r + ROPE_DIM:b + QK_PAD, :] = jnp.zeros((QK_PAD - HEAD_DIM - ROPE_DIM, acc.shape[1]), BF16)


def _mla_qt(cq, wt, cos_t, sin_t, tm):
    t = cq.shape[0]
    n_tab = cos_t.shape[1] // tm
    tab = lambda i: (0, i % n_tab)
    half = ROPE_DIM // 2
    return pl.pallas_call(
        _mla_qt_kernel,
        grid=(t // tm,),
        in_specs=[pl.BlockSpec((tm, Q_LORA), lambda i: (i, 0)),
                  pl.BlockSpec(wt.shape, lambda i: (0, 0)),
                  pl.BlockSpec((half, tm), tab), pl.BlockSpec((half, tm), tab)],
        out_specs=pl.BlockSpec((HEADS * QK_PAD, tm), lambda i: (0, i)),
        out_shape=jax.ShapeDtypeStruct((HEADS * QK_PAD, t), BF16),
        compiler_params=_params("parallel"),
        name="mla_q_proj_t",
    )(cq, wt, cos_t, sin_t)


def _mla_kvt_kernel(ckv_ref, kr_ref, wk_ref, wvt_ref, k_ref, vt_ref):
    ckv = ckv_ref[...]
    acc = _dot(ckv, wk_ref[...])
    kr = kr_ref[...]
    zero = jnp.zeros((kr.shape[0], QK_PAD - LANES - ROPE_DIM), BF16)
    for h in range(HEADS):
        k_ref[h, :, :LANES] = acc[:, h * LANES:(h + 1) * LANES].astype(BF16)
        k_ref[h, :, LANES:LANES + ROPE_DIM] = kr
        k_ref[h, :, LANES + ROPE_DIM:] = zero
    vt_ref[...] = _dot_nt(wvt_ref[...], ckv).astype(BF16)


def _mla_kvt(ckv, kr, wk, wvt, b, tm):
    t = ckv.shape[0]
    per = t // b // tm
    row = lambda i: (i, 0)
    const = lambda i: (0, 0)
    return pl.pallas_call(
        _mla_kvt_kernel,
        grid=(t // tm,),
        in_specs=[pl.BlockSpec((tm, KV_LORA), row), pl.BlockSpec((tm, ROPE_DIM), row),
                  pl.BlockSpec(wk.shape, const), pl.BlockSpec(wvt.shape, const)],
        out_specs=[pl.BlockSpec((None, HEADS, tm, QK_PAD), lambda i: (i // per, 0, i % per, 0)),
                   pl.BlockSpec((None, SB_WIDTH, tm), lambda i: (i, 0, 0))],
        out_shape=[jax.ShapeDtypeStruct((b, HEADS, t // b, QK_PAD), BF16),
                   jax.ShapeDtypeStruct((t // tm, SB_WIDTH, tm), BF16)],
        compiler_params=_params("parallel"),
        name="mla_kv_proj_t",
    )(ckv, kr, wk, wvt)


def _gate_kernel(h_ref, w_ref, o_ref):
    o_ref[...] = jax.nn.sigmoid(_dot(h_ref[...], w_ref[...])).astype(BF16)


def _gates(h, w, tm, tn):
    t, d = h.shape
    n = w.shape[1]
    return pl.pallas_call(
        _gate_kernel,
        grid=(t // tm, n // tn),
        in_specs=[pl.BlockSpec((tm, d), lambda i, j: (i, 0)),
                  pl.BlockSpec((d, tn), lambda i, j: (0, j))],
        out_specs=pl.BlockSpec((tm, tn), lambda i, j: (i, j)),
        out_shape=jax.ShapeDtypeStruct((t, n), BF16),
        compiler_params=_params("parallel", "arbitrary"),
        name="branch_gates",
    )(h, w)


def _mix_kernel(osb_ref, omla_ref, g_ref, wb_ref, wo_ref, x_ref, gn_ref, o_ref):
    d = x_ref.shape[1]
    if len(omla_ref.shape) == 3:
        omla = jnp.concatenate([omla_ref[h] for h in range(HEADS)], axis=1)
    else:
        omla = omla_ref[...]
    merged = (g_ref[:, :d].astype(F32) * _dot(osb_ref[...], wb_ref[0])
              + g_ref[:, d:].astype(F32) * _dot(omla, wb_ref[1]))
    o_ref[...] = x_ref[...] + _rms(_dot(merged.astype(BF16), wo_ref[...]), gn_ref[...])


def _mix(osb, omla, gates, wb, wo, x, gn, tm):
    t, d = x.shape
    row = lambda i: (i, 0)
    once = pl.Buffered(1)
    if omla.ndim == 4:
        per = omla.shape[2] // tm
        omla_spec = pl.BlockSpec((None, HEADS, tm, HEAD_DIM), lambda i: (i // per, 0, i % per, 0))
    else:
        omla_spec = pl.BlockSpec((tm, SB_WIDTH), row)
    return pl.pallas_call(
        _mix_kernel,
        grid=(t // tm,),
        in_specs=[pl.BlockSpec((tm, SB_WIDTH), row), omla_spec,
                  pl.BlockSpec((tm, 2 * d), row),
                  pl.BlockSpec(wb.shape, lambda i: (0, 0, 0), pipeline_mode=once),
                  pl.BlockSpec(wo.shape, lambda i: (0, 0), pipeline_mode=once),
                  pl.BlockSpec((tm, d), row), pl.BlockSpec((1, d), lambda i: (0, 0))],
        out_specs=pl.BlockSpec((tm, d), row),
        out_shape=jax.ShapeDtypeStruct((t, d), F32),
        compiler_params=_params("parallel"),
        name="branch_mix_out",
    )(osb, omla, gates, wb, wo, x, gn)


def _ffn_kernel(x_ref, gpre_ref, wu_ref, wd_ref, gpost_ref, o_ref, h_scr, acc_scr):
    j = pl.program_id(1)

    @pl.when(j == 0)
    def _():
        h_scr[...] = _rms(x_ref[...], gpre_ref[...]).astype(BF16)
        acc_scr[...] = jnp.zeros_like(acc_scr)

    u = jnp.maximum(_dot(h_scr[...], wu_ref[...]), 0.0)
    acc_scr[...] += _dot((u * u).astype(BF16), wd_ref[...])

    @pl.when(j == pl.num_programs(1) - 1)
    def _():
        o_ref[...] = x_ref[...] + _rms(acc_scr[...], gpost_ref[...])


def _ffn(x, gpre, wu, wd, gpost, tm, tf):
    t, d = x.shape
    row = lambda i, j: (i, 0)
    const = lambda i, j: (0, 0)
    return pl.pallas_call(
        _ffn_kernel,
        grid=(t // tm, D_FF // tf),
        in_specs=[pl.BlockSpec((tm, d), row), pl.BlockSpec((1, d), const),
                  pl.BlockSpec((d, tf), lambda i, j: (0, j)),
                  pl.BlockSpec((tf, d), lambda i, j: (j, 0)),
                  pl.BlockSpec((1, d), const)],
        out_specs=pl.BlockSpec((tm, d), row),
        out_shape=jax.ShapeDtypeStruct((t, d), F32),
        scratch_shapes=[pltpu.VMEM((tm, d), BF16), pltpu.VMEM((tm, d), F32)],
        compiler_params=_params("parallel", "arbitrary", vmem=FFN_VMEM_LIMIT),
        name="sqrelu_ffn",
    )(x, gpre, wu, wd, gpost)


def _ple_kernel(x_ref, p_ref, gg_ref, wg_ref, wp_ref, gpost_ref, o_ref):
    x = x_ref[...]
    gate = jax.nn.sigmoid(_dot(_rms(x, gg_ref[...]).astype(BF16), wg_ref[...]))
    pe = _dot(p_ref[...].astype(BF16), wp_ref[...])
    o_ref[...] = x + _rms(pe * gate, gpost_ref[...])


def _ple(x, p, gg, wg, wp, gpost, tm):
    t, d = x.shape
    row = lambda i: (i, 0)
    const = lambda i: (0, 0)
    return pl.pallas_call(
        _ple_kernel,
        grid=(t // tm,),
        in_specs=[pl.BlockSpec((tm, d), row), pl.BlockSpec((tm, PLE_DIM), row),
                  pl.BlockSpec((1, d), const), pl.BlockSpec((d, d), const),
                  pl.BlockSpec((PLE_DIM, d), const), pl.BlockSpec((1, d), const)],
        out_specs=pl.BlockSpec((tm, d), row),
        out_shape=jax.ShapeDtypeStruct((t, d), F32),
        compiler_params=_params("parallel"),
        name="ple_embed",
    )(x, p, gg, wg, wp, gpost)


def _sb_step(qs, ks, vs, tri, carry, mask):
    rows = qs[0].shape[0]
    z = jnp.concatenate([_dot_nt(q, k) for q, k in zip(qs, ks)], axis=0)
    sp = jnp.maximum(z, 0.0) + jnp.log(1.0 + jnp.exp(-jnp.abs(z)))
    spm = sp if mask is None else jnp.where(mask, sp, 0.0)
    hi = spm.astype(BF16)
    lo = (spm - hi.astype(F32)).astype(BF16)
    cs = _dot(jnp.concatenate([hi, lo], axis=1), tri)
    after = cs[:, :KEY_BLOCK] if carry is None else cs[:, :KEY_BLOCK] + carry
    w = jnp.exp((z - sp) + after)
    if mask is not None:
        w = jnp.where(mask, w, 0.0)
    w = w.astype(BF16)
    pvs = [_dot(w[i * rows:(i + 1) * rows], v) for i, v in enumerate(vs)]
    carry = cs[:, KEY_BLOCK:] if carry is None else carry + cs[:, KEY_BLOCK:]
    return carry, pvs


def _sb_walk(first_blocks, later_blocks, tri_ref, carry_scr, acc_scr, o_ref, alongside=None):
    def run(chains, carry, mask, first):
        carry, pvs = _sb_step([c[2] for c in chains], [c[3] for c in chains], [c[4] for c in chains],
                              tri_ref[...], carry, mask)
        carry_scr[...] = carry
        for (rows, cols, _, _, _), pv in zip(chains, pvs):
            if first:
                acc_scr[rows, cols] = pv
            else:
                acc_scr[rows, cols] += pv
        return jnp.max(carry)

    def step0():
        chains, mask = first_blocks()
        return run(chains, None, mask, True)

    def step(j):
        chains, valid = later_blocks(j)
        rows = chains[0][2].shape[0]
        carry = jnp.concatenate(
            [jnp.where(ok, carry_scr[i * rows:(i + 1) * rows, :], MASKED) for i, ok in enumerate(valid)],
            axis=0)
        return run(chains, carry, None, False)

    if alongside is None:
        start = (jnp.int32(1), step0())
    else:
        finish = alongside()
        step0()
        step(1)
        start = (jnp.int32(3), step(2))
        finish()
    lax.while_loop(lambda st: st[1] > SB_EXIT, lambda st: (st[0] + 1, step(st[0])), start)
    o_ref[...] = acc_scr[...].astype(BF16)


def _head_cols(h):
    return slice(h * HEAD_DIM, (h + 1) * HEAD_DIM)


def _sb_prompt_kernel(q_ref, k_ref, v_ref, tri_ref, h_ref, wg_ref, o_ref, gate_ref, carry_scr, acc_scr,
                      *, rsub):
    bk = KEY_BLOCK
    base = pl.program_id(1) * rsub
    n_chain = rsub * HEADS
    r = lax.broadcasted_iota(jnp.int32, (n_chain * bk, bk), 0)
    c = lax.broadcasted_iota(jnp.int32, (n_chain * bk, bk), 1)
    causal = c < jnp.bitwise_and(r, bk - 1)

    def blocks(j):
        chains, valid = [], []
        for s in range(rsub):
            kb = base + s - j
            start = pl.multiple_of(jnp.maximum(kb, 0) * bk, bk)
            rows = slice(s * bk, (s + 1) * bk)
            for h in range(HEADS):
                cols = _head_cols(h)
                chains.append((rows, cols, q_ref[rows, cols],
                               k_ref[pl.ds(start, bk), cols], v_ref[pl.ds(start, bk), cols]))
                valid.append(kb >= 0)
        return chains, valid

    def gates():
        logits = _dot(h_ref[...], wg_ref[...])

        def finish():
            gate_ref[...] = jax.nn.sigmoid(logits).astype(BF16)
        return finish

    _sb_walk(lambda: (blocks(0)[0], causal), blocks, tri_ref, carry_scr, acc_scr, o_ref, alongside=gates)


GATE_TILE = 1024


def _sb_prompt_gates(qkv, tri, h, wg):
    _, b, n, _ = qkv.shape
    t, d = h.shape
    gw = wg.shape[1]
    bq = GATE_TILE * GATE_TILE // gw
    rsub = bq // KEY_BLOCK
    assert bq % KEY_BLOCK == 0 and n % bq == 0 and t % GATE_TILE == 0 and gw % GATE_TILE == 0
    nq = n // bq
    ncol = gw // GATE_TILE
    full = (None, None, n, SB_WIDTH)
    once = pl.Buffered(1)
    return pl.pallas_call(
        functools.partial(_sb_prompt_kernel, rsub=rsub),
        grid=(b, nq),
        in_specs=[pl.BlockSpec((None, None, bq, SB_WIDTH), lambda i, j: (0, i, j, 0)),
                  pl.BlockSpec(full, lambda i, j: (1, i, 0, 0), pipeline_mode=once),
                  pl.BlockSpec(full, lambda i, j: (2, i, 0, 0), pipeline_mode=once),
                  pl.BlockSpec(tri.shape, lambda i, j: (0, 0)),
                  pl.BlockSpec((GATE_TILE, d), lambda i, j: ((i * nq + j) // ncol, 0)),
                  pl.BlockSpec((d, GATE_TILE), lambda i, j: (0, (i * nq + j) % ncol))],
        out_specs=[pl.BlockSpec((None, bq, SB_WIDTH), lambda i, j: (i, j, 0)),
                   pl.BlockSpec((GATE_TILE, GATE_TILE), lambda i, j: ((i * nq + j) // ncol, (i * nq + j) % ncol))],
        out_shape=[jax.ShapeDtypeStruct((b, n, SB_WIDTH), BF16),
                   jax.ShapeDtypeStruct((t, gw), BF16)],
        scratch_shapes=[pltpu.VMEM((rsub * HEADS * KEY_BLOCK, KEY_BLOCK), F32),
                        pltpu.VMEM((bq, SB_WIDTH), F32)],
        compiler_params=_params("arbitrary", "arbitrary"),
        name="stickbreak_attn_gates",
    )(qkv, qkv, qkv, tri, h, wg)


def _sb_decode_kernel(q_ref, kn_ref, vn_ref, kp_ref, vp_ref, tri_ref, o_ref, carry_scr, acc_scr,
                      *, n, past):
    bk = KEY_BLOCK
    nb = past // bk
    assert n & (n - 1) == 0
    r = lax.broadcasted_iota(jnp.int32, (HEADS * n, bk), 0)
    c = lax.broadcasted_iota(jnp.int32, (HEADS * n, bk), 1)
    causal = c < jnp.bitwise_and(r, n - 1)
    zpad = jnp.zeros((bk - n, HEAD_DIM), BF16)
    rows = slice(0, n)

    def first():
        return [(rows, _head_cols(h), q_ref[:, _head_cols(h)],
                 jnp.concatenate([kn_ref[:, _head_cols(h)], zpad], axis=0),
                 jnp.concatenate([vn_ref[:, _head_cols(h)], zpad], axis=0))
                for h in range(HEADS)], causal

    def later(j):
        kb = nb - j
        start = jnp.maximum(kb, 0) * (bk * HEADS)
        chains = []
        for h in range(HEADS):
            sl = pl.ds(start + h, bk, stride=HEADS)
            chains.append((rows, _head_cols(h), q_ref[:, _head_cols(h)],
                           kp_ref[sl, :].astype(BF16), vp_ref[sl, :].astype(BF16)))
        return chains, [kb >= 0] * HEADS

    _sb_walk(first, later, tri_ref, carry_scr, acc_scr, o_ref)


def _sb_decode(qkv, k_past, v_past, tri):
    _, b, n, _ = qkv.shape
    past = k_past.shape[1]
    assert past % KEY_BLOCK == 0 and n <= KEY_BLOCK and n % 16 == 0
    new = (None, None, n, SB_WIDTH)
    old = (None, past * HEADS, HEAD_DIM)
    return pl.pallas_call(
        functools.partial(_sb_decode_kernel, n=n, past=past),
        grid=(b,),
        in_specs=[pl.BlockSpec(new, lambda i: (0, i, 0, 0)),
                  pl.BlockSpec(new, lambda i: (1, i, 0, 0)),
                  pl.BlockSpec(new, lambda i: (2, i, 0, 0)),
                  pl.BlockSpec(old, lambda i: (i, 0, 0)),
                  pl.BlockSpec(old, lambda i: (i, 0, 0)),
                  pl.BlockSpec(tri.shape, lambda i: (0, 0))],
        out_specs=pl.BlockSpec((None, n, SB_WIDTH), lambda i: (i, 0, 0)),
        out_shape=jax.ShapeDtypeStruct((b, n, SB_WIDTH), BF16),
        scratch_shapes=[pltpu.VMEM((HEADS * n, KEY_BLOCK), F32),
                        pltpu.VMEM((n, SB_WIDTH), F32)],
        compiler_params=_params("parallel"),
        name="stickbreak_attn_decode",
    )(qkv, qkv, qkv, k_past.reshape(b, past * HEADS, HEAD_DIM),
      v_past.reshape(b, past * HEADS, HEAD_DIM), tri)


CHUNK_SHIFT = CHUNK.bit_length() - 1
assert 1 << CHUNK_SHIFT == CHUNK


def _chunk(pos):
    return lax.shift_right_arithmetic(pos, CHUNK_SHIFT)


def _mla_prompt_kernel(qt_ref, k_ref, vt_ref, o_ref, s_scr, *, bk):
    bq = 2 * bk
    qi = pl.program_id(2)

    def scores(kb, slot):
        s0 = pl.multiple_of(kb * bk, bk)
        s_scr[slot] = _dot(k_ref[pl.ds(s0, bk), :], qt_ref[...])

    def absorb(st, s, kb):
        m, l, acc = st
        m_new = jnp.maximum(m, jnp.max(s, axis=0, keepdims=True))
        alpha = jnp.exp2(m - m_new)
        p = jnp.exp2(s - m_new)
        l = alpha * l + jnp.sum(p, axis=0, keepdims=True)
        acc = alpha * acc + _dot(vt_ref[kb], p.astype(BF16))
        return m_new, l, acc

    def pair(i, st):
        kb = 2 * i
        scores(kb + 1, 1)
        st = absorb(st, s_scr[0], kb)
        scores(kb + 2, 0)
        return absorb(st, s_scr[1], kb + 1)

    init = (jnp.full((1, bq), MASKED, F32), jnp.zeros((1, bq), F32), jnp.zeros((HEAD_DIM, bq), F32))
    scores(0, 0)
    st = lax.fori_loop(0, qi, pair, init)
    late = (slice(None), slice(bk, bq))
    s_late = _dot(k_ref[pl.ds(pl.multiple_of((2 * qi + 1) * bk, bk), bk), :], qt_ref[late])
    vis = (_chunk(lax.broadcasted_iota(jnp.int32, (bk, bq), 0))
           <= _chunk(lax.broadcasted_iota(jnp.int32, (bk, bq), 1)))
    m, l, acc = absorb(st, jnp.where(vis, s_scr[0], MASKED), 2 * qi)
    _, l2, acc2 = absorb((m[late], l[late], acc[late]), jnp.where(vis[:, :bk], s_late, MASKED), 2 * qi + 1)
    l = jnp.concatenate([l[:, :bk], l2], axis=1)
    acc = jnp.concatenate([acc[:, :bk], acc2], axis=1)
    o_ref[...] = (acc / l).T.astype(BF16)


def _mla_prompt(qt, k, vt):
    b, _, n, _ = k.shape
    bk = vt.shape[-1]
    bq = 2 * bk
    assert bk % CHUNK == 0 and n % bq == 0
    nq = n // bq
    return pl.pallas_call(
        functools.partial(_mla_prompt_kernel, bk=bk),
        grid=(b, HEADS, nq),
        in_specs=[pl.BlockSpec((QK_PAD, bq), lambda i, h, j: (h, i * nq + j)),
                  pl.BlockSpec((None, None, n, QK_PAD), lambda i, h, j: (i, h, 0, 0)),
                  pl.BlockSpec((None, n // bk, HEAD_DIM, bk), lambda i, h, j: (i, 0, h, 0))],
        out_specs=pl.BlockSpec((None, None, bq, HEAD_DIM), lambda i, h, j: (i, h, j, 0)),
        out_shape=jax.ShapeDtypeStruct((b, HEADS, n, HEAD_DIM), BF16),
        scratch_shapes=[pltpu.VMEM((2, bk, bq), F32)],
        compiler_params=_params("parallel", "parallel", "arbitrary"),
        name="mla_attn",
    )(qt, k, vt)


def _mla_decode_kernel(q_ref, ckv_old_ref, kr_old_ref, ckv_new_ref, kr_new_ref, wuk_ref, wuv_ref, o_ref,
                       *, n, past):
    rows = HEADS * n
    ck_old = ckv_old_ref[...].astype(BF16)
    kr_old = kr_old_ref[...].astype(BF16)
    ck_new = jnp.concatenate([ckv_new_ref[...], jnp.zeros((LANES - n, KV_LORA), BF16)], axis=0)
    kr_new = jnp.concatenate([kr_new_ref[...], jnp.zeros((LANES - n, ROPE_DIM), BF16)], axis=0)
    qa = jnp.concatenate(
        [_dot(q_ref[:, h * QK_PAD:h * QK_PAD + HEAD_DIM], wuk_ref[h]) for h in range(HEADS)],
        axis=0).astype(BF16)
    qr = jnp.concatenate(
        [q_ref[:, h * QK_PAD + HEAD_DIM:h * QK_PAD + HEAD_DIM + ROPE_DIM] for h in range(HEADS)], axis=0)
    s_old = _dot_nt(qa, ck_old) + _dot_nt(qr, kr_old)
    r = jnp.bitwise_and(lax.broadcasted_iota(jnp.int32, (rows, LANES), 0), n - 1)
    c = lax.broadcasted_iota(jnp.int32, (rows, LANES), 1)
    vis = jnp.logical_and(c < n, _chunk(past + c) <= _chunk(past + r))
    s_new = jnp.where(vis, _dot_nt(qa, ck_new) + _dot_nt(qr, kr_new), MASKED)
    m = jnp.maximum(jnp.max(s_old, axis=-1, keepdims=True), jnp.max(s_new, axis=-1, keepdims=True))
    p_old = jnp.exp(s_old - m)
    p_new = jnp.exp(s_new - m)
    l = jnp.sum(p_old, axis=-1, keepdims=True) + jnp.sum(p_new, axis=-1, keepdims=True)
    ctx = ((_dot(p_old.astype(BF16), ck_old) + _dot(p_new.astype(BF16), ck_new)) / l).astype(BF16)
    for h in range(HEADS):
        o_ref[:, _head_cols(h)] = _dot(ctx[h * n:(h + 1) * n], wuv_ref[:, _head_cols(h)]).astype(BF16)


def _mla_decode(q, ckv_old, kr_old, ckv_new, kr_new, wuk_t, wuv):
    b, n, _ = q.shape
    past = ckv_old.shape[1]
    assert n <= LANES and n % 16 == 0 and n & (n - 1) == 0
    spec = lambda rows, width: pl.BlockSpec((None, rows, width), lambda i: (i, 0, 0))
    return pl.pallas_call(
        functools.partial(_mla_decode_kernel, n=n, past=past),
        grid=(b,),
        in_specs=[spec(n, HEADS * QK_PAD), spec(past, KV_LORA), spec(past, ROPE_DIM),
                  spec(n, KV_LORA), spec(n, ROPE_DIM),
                  pl.BlockSpec(wuk_t.shape, lambda i: (0, 0, 0)), pl.BlockSpec(wuv.shape, lambda i: (0, 0))],
        out_specs=spec(n, SB_WIDTH),
        out_shape=jax.ShapeDtypeStruct((b, n, SB_WIDTH), BF16),
        compiler_params=_params("parallel"),
        name="mla_attn_decode",
    )(q, ckv_old, kr_old, ckv_new, kr_new, wuk_t, wuv)


def _rope_tables(pos):
    half = ROPE_DIM // 2
    freqs = ROPE_THETA ** (-jnp.arange(half, dtype=F32) / half)
    ang = pos.astype(F32)[:, None] * freqs[None, :]
    cos, sin = jnp.cos(ang), jnp.sin(ang)
    z = jnp.zeros((pos.shape[0], LANES - ROPE_DIM), F32)
    return jnp.concatenate([cos, cos, z], axis=1), jnp.concatenate([-sin, sin, z], axis=1)


def _swap_halves(w):
    half = w.shape[-1] // 2
    return jnp.concatenate([w[..., half:], w[..., :half]], axis=-1)


def _prep_weights(w_in, w_uq, w_uk, w_uv, w_branch, w_out, w_up, w_down, w_ple_gate, w_ple):
    d = w_in.shape[0]
    o = 3 * SB_WIDTH
    w_kr = w_in[:, o + Q_LORA + KV_LORA:o + Q_LORA + KV_LORA + ROPE_DIM]
    z64 = jnp.zeros((d, LANES - ROPE_DIM), w_in.dtype)
    w_lat = jnp.concatenate([w_in[:, o:o + Q_LORA + KV_LORA], w_kr, z64, _swap_halves(w_kr), z64], axis=1)
    wq3 = w_uq.reshape(Q_LORA, HEADS, HEAD_DIM + ROPE_DIM)
    rp = wq3[:, :, HEAD_DIM:]
    zq = jnp.zeros((Q_LORA, HEADS, LANES - ROPE_DIM), w_uq.dtype)
    w_q = jnp.concatenate([
        jnp.concatenate([wq3[:, :, :HEAD_DIM], rp, zq], axis=-1).reshape(Q_LORA, HEADS * QK_PAD),
        jnp.concatenate([_swap_halves(rp), zq], axis=-1).reshape(Q_LORA, HEADS * LANES)], axis=1)
    return dict(
        qkv=w_in.astype(BF16),
        lat=w_lat.astype(BF16),
        gate=w_in[:, o + Q_LORA + KV_LORA + ROPE_DIM:].astype(BF16),
        q=w_q.astype(BF16),
        qt=w_q[:, :HEADS * QK_PAD].T.astype(BF16),
        uk=w_uk.astype(BF16),
        uk_t=w_uk.reshape(KV_LORA, HEADS, HEAD_DIM).transpose(1, 2, 0).astype(BF16),
        uv=w_uv.astype(BF16),
        uvt=w_uv.T.astype(BF16),
        branch=w_branch.astype(BF16),
        out=w_out.astype(BF16),
        up=w_up.astype(BF16),
        down=w_down.astype(BF16),
        ple_gate=w_ple_gate.astype(BF16),
        ple=w_ple.astype(BF16),
    )


def _tile(t, want):
    return want if t % want == 0 else t


def _layer(x, ple, past, w, g, tri):
    b, n, d = x.shape
    t = b * n
    xf = x.reshape(t, d)
    tm = _tile(t, 1024)
    tm_s = _tile(t, 512)
    past_len = 0 if past is None else past[0].shape[1]
    if n >= tm:
        pos = jnp.arange(n) + past_len
    else:
        pos = jnp.tile(jnp.arange(n) + past_len, tm // n)
    cos2, sin2 = _rope_tables(pos)

    h = _norm(xf, g['mix_pre'], tm)
    qkv_b, k_f, v_f = _qkv(h, w['qkv'], tm)
    cq, ckv_f, ckv_b, kr_f, kr_b = _latent(h, w['lat'], g['q'], g['kv'], cos2, sin2, tm)
    qkv_b = qkv_b.reshape(3, b, n, SB_WIDTH)

    if past is None:
        blk = _tile(n, 512)
        qt = _mla_qt(cq, w['qt'], cos2[:, :ROPE_DIM // 2].T, sin2[:, ROPE_DIM // 2:ROPE_DIM].T, blk)
        k_mla, vt = _mla_kvt(ckv_b, kr_b, w['uk'], w['uvt'], b, blk)
        o_sb, gates = _sb_prompt_gates(qkv_b, tri, h, w['gate'])
        o_mla = _mla_prompt(qt, k_mla, vt.reshape(b, n // blk, SB_WIDTH, blk))
    else:
        gates = _gates(h, w['gate'], tm, 1024)
        q_mla = _mla_q(cq, w['q'], cos2, sin2, tm_s).reshape(b, n, -1)
        o_sb = _sb_decode(qkv_b, past[0], past[1], tri)
        o_mla = _mla_decode(q_mla, past[2], past[3], ckv_b.reshape(b, n, -1), kr_b.reshape(b, n, -1),
                            w['uk_t'], w['uv'])

    if o_mla.ndim == 3:
        o_mla = o_mla.reshape(t, -1)
    x1 = _mix(o_sb.reshape(t, -1), o_mla, gates, w['branch'], w['out'], xf, g['mix_post'], tm_s)
    x2 = _ffn(x1, g['ffn_pre'], w['up'], w['down'], g['ffn_post'], tm_s, FFN_TILE)
    x3 = _ple(x2, ple.reshape(t, -1), g['ple_gate'], w['ple_gate'], w['ple'], g['ple_post'], tm_s)
    state = (k_f.reshape(1, b, n, HEADS, HEAD_DIM), v_f.reshape(1, b, n, HEADS, HEAD_DIM),
             ckv_f.reshape(1, b, n, KV_LORA), kr_f.reshape(1, b, n, ROPE_DIM))
    return x3.reshape(b, n, d), state


def kernel(x_prompt, x_sample, cache_sb_k, cache_sb_v, cache_mla_ckv, cache_mla_krope, p_prompt, p_sample, g_mix_pre, w_in, g_q, w_uq, g_kv, w_uk, w_uv, w_branch, w_out, g_mix_post, g_ffn_pre, w_up, w_down, g_ffn_post, g_ple_gate, w_ple_gate, w_ple, g_ple_post):
    assert w_in.shape[0] == 1, "single layer"
    w = _prep_weights(w_in[0], w_uq[0], w_uk[0], w_uv[0], w_branch[0], w_out[0], w_up[0], w_down[0],
                      w_ple_gate[0], w_ple[0])
    g = dict(mix_pre=g_mix_pre, q=g_q, kv=g_kv, mix_post=g_mix_post, ffn_pre=g_ffn_pre,
             ffn_post=g_ffn_post, ple_gate=g_ple_gate, ple_post=g_ple_post)
    idx = jnp.arange(KEY_BLOCK)
    lower = -(idx[:, None] > idx[None, :]).astype(BF16)
    tri = jnp.concatenate([lower, jnp.full((KEY_BLOCK, KEY_BLOCK), -1, BF16)], axis=1)
    tri = jnp.concatenate([tri, tri], axis=0)

    yp, sp = _layer(x_prompt, p_prompt[0], None, w, g, tri)
    past = (cache_sb_k[0], cache_sb_v[0], cache_mla_ckv[0], cache_mla_krope[0])
    ys, ss = _layer(x_sample, p_sample[0], past, w, g, tri)
    return (yp, ys) + sp + ss
```

```python
import functools

import jax
import jax.numpy as jnp
from jax import lax
from jax.experimental import pallas as pl
from jax.experimental.pallas import tpu as pltpu

F32 = jnp.float32
BF16 = jnp.bfloat16

D_MODEL = 2048
CHUNK = 64
PLE_DIM = 256
HEADS = 8
HEAD_DIM = 128
SB_WIDTH = HEADS * HEAD_DIM
ROPE_DIM = 64
Q_LORA = 512
KV_LORA = 512
QK_PAD = 256
D_FF = 4 * D_MODEL
ROPE_THETA = 10000.0
EPS = 1e-6
SB_SCALE = HEAD_DIM ** -0.5
MLA_SCALE = (HEAD_DIM + ROPE_DIM) ** -0.5
LOG2E = 1.4426950408889634

LANES = 128
KEY_BLOCK = 128
SB_EXIT = -104.0
MASKED = -1e30
VMEM_LIMIT = 56 * 1024 * 1024
FFN_VMEM_LIMIT = 62 * 1024 * 1024
FFN_TILE = 2048


def _params(*sem, vmem=VMEM_LIMIT):
    return pltpu.CompilerParams(dimension_semantics=sem, vmem_limit_bytes=vmem)


def _rms(xf, g):
    ms = jnp.mean(xf * xf, axis=-1, keepdims=True)
    return xf * lax.rsqrt(ms + EPS) * g


def _dot(a, b):
    return jnp.dot(a, b, preferred_element_type=F32)


def _dot_nt(a, b):
    return lax.dot_general(a, b, (((1,), (1,)), ((), ())), preferred_element_type=F32)


def _qkv_kernel(h_ref, w_ref, qkv_ref, kf_ref, vf_ref):
    j = pl.program_id(1)
    acc = _dot(h_ref[...], w_ref[...])
    qkv_ref[...] = (acc * jnp.where(j == 0, SB_SCALE, 1.0)).astype(BF16)

    def store_state(ref):
        for h in range(HEADS):
            ref[pl.ds(h, acc.shape[0], stride=HEADS), :] = acc[:, _head_cols(h)]

    pl.when(j == 1)(lambda: store_state(kf_ref))
    pl.when(j == 2)(lambda: store_state(vf_ref))


def _qkv(h, w, tm):
    t, d = h.shape
    state = pl.BlockSpec((tm * HEADS, HEAD_DIM), lambda i, j: (i, 0))
    return pl.pallas_call(
        _qkv_kernel,
        grid=(t // tm, 3),
        in_specs=[pl.BlockSpec((tm, d), lambda i, j: (i, 0)),
                  pl.BlockSpec((d, SB_WIDTH), lambda i, j: (0, j))],
        out_specs=[pl.BlockSpec((None, tm, SB_WIDTH), lambda i, j: (j, i, 0)), state, state],
        out_shape=[jax.ShapeDtypeStruct((3, t, SB_WIDTH), BF16),
                   jax.ShapeDtypeStruct((t * HEADS, HEAD_DIM), F32),
                   jax.ShapeDtypeStruct((t * HEADS, HEAD_DIM), F32)],
        compiler_params=_params("parallel", "arbitrary"),
        name="sb_qkv_proj",
    )(h, w)


def _latent_kernel(x_ref, g_ref, w_ref, gq_ref, gkv_ref, cos_ref, sin_ref,
                   h_ref, cq_ref, ckvf_ref, ckvb_ref, krf_ref, krb_ref):
    h = _rms(x_ref[...], g_ref[...]).astype(BF16)
    h_ref[...] = h
    acc = _dot(h, w_ref[...])
    cq_ref[...] = _rms(acc[:, :Q_LORA], gq_ref[...]).astype(BF16)
    ckv = _rms(acc[:, Q_LORA:Q_LORA + KV_LORA], gkv_ref[...])
    ckvf_ref[...] = ckv
    ckvb_ref[...] = ckv.astype(BF16)
    base = Q_LORA + KV_LORA
    kr = acc[:, base:base + LANES] * cos_ref[...] + acc[:, base + LANES:base + 2 * LANES] * sin_ref[...]
    krf_ref[...] = kr[:, :ROPE_DIM]
    krb_ref[...] = kr[:, :ROPE_DIM].astype(BF16)


def _latent(x, g, w, gq, gkv, cos2, sin2, tm):
    t, d = x.shape
    n_tab = cos2.shape[0] // tm
    row = lambda i: (i, 0)
    const = lambda i: (0, 0)
    tab = lambda i: (i % n_tab, 0)
    wn = w.shape[1]
    return pl.pallas_call(
        _latent_kernel,
        grid=(t // tm,),
        in_specs=[pl.BlockSpec((tm, d), row), pl.BlockSpec((1, d), const), pl.BlockSpec((d, wn), const),
                  pl.BlockSpec((1, Q_LORA), const), pl.BlockSpec((1, KV_LORA), const),
                  pl.BlockSpec((tm, LANES), tab), pl.BlockSpec((tm, LANES), tab)],
        out_specs=[pl.BlockSpec((tm, d), row),
                   pl.BlockSpec((tm, Q_LORA), row), pl.BlockSpec((tm, KV_LORA), row),
                   pl.BlockSpec((tm, KV_LORA), row), pl.BlockSpec((tm, ROPE_DIM), row),
                   pl.BlockSpec((tm, ROPE_DIM), row)],
        out_shape=[jax.ShapeDtypeStruct((t, d), BF16),
                   jax.ShapeDtypeStruct((t, Q_LORA), BF16),
                   jax.ShapeDtypeStruct((t, KV_LORA), F32),
                   jax.ShapeDtypeStruct((t, KV_LORA), BF16),
                   jax.ShapeDtypeStruct((t, ROPE_DIM), F32),
                   jax.ShapeDtypeStruct((t, ROPE_DIM), BF16)],
        compiler_params=_params("parallel"),
        name="norm_latent_proj",
    )(x, g, w, gq, gkv, cos2, sin2)


def _mla_q_kernel(cq_ref, w_ref, cos_ref, sin_ref, q_ref):
    acc = _dot(cq_ref[...], w_ref[...])
    cos, sin = cos_ref[...], sin_ref[...]
    for h in range(HEADS):
        b = h * QK_PAD
        q_ref[:, b:b + LANES] = (acc[:, b:b + LANES] * MLA_SCALE).astype(BF16)
        sw = HEADS * QK_PAD + h * LANES
        rot = acc[:, b + LANES:b + QK_PAD] * cos + acc[:, sw:sw + LANES] * sin
        q_ref[:, b + LANES:b + QK_PAD] = (rot * MLA_SCALE).astype(BF16)


def _mla_q(cq, w, cos2, sin2, tm):
    t = cq.shape[0]
    n_tab = cos2.shape[0] // tm
    row = lambda i: (i, 0)
    tab = lambda i: (i % n_tab, 0)
    return pl.pallas_call(
        _mla_q_kernel,
        grid=(t // tm,),
        in_specs=[pl.BlockSpec((tm, Q_LORA), row),
                  pl.BlockSpec(w.shape, lambda i: (0, 0)),
                  pl.BlockSpec((tm, LANES), tab), pl.BlockSpec((tm, LANES), tab)],
        out_specs=pl.BlockSpec((tm, HEADS * QK_PAD), row),
        out_shape=jax.ShapeDtypeStruct((t, HEADS * QK_PAD), BF16),
        compiler_params=_params("parallel"),
        name="mla_q_proj",
    )(cq, w, cos2, sin2)


def _mla_qt_kernel(cq_ref, wt_ref, cos_ref, sin_ref, qt_ref):
    acc = _dot_nt(wt_ref[...], cq_ref[...])
    cos, sin = cos_ref[...], sin_ref[...]
    half = ROPE_DIM // 2
    scale = MLA_SCALE * LOG2E
    for h in range(HEADS):
        b = h * QK_PAD
        r = b + HEAD_DIM
        qt_ref[b:r, :] = (acc[b:r] * scale).astype(BF16)
        x1, x2 = acc[r:r + half], acc[r + half:r + ROPE_DIM]
        qt_ref[r:r + half, :] = ((x1 * cos - x2 * sin) * scale).astype(BF16)
        qt_ref[r + half:r + ROPE_DIM, :] = ((x1 * sin + x2 * cos) * scale).astype(BF16)
        qt_ref[r + ROPE_DIM:b + QK_PAD, :] = jnp.zeros((QK_PAD - HEAD_DIM - ROPE_DIM, acc.shape[1]), BF16)


def _mla_qt(cq, wt, cos_t, sin_t, tm):
    t = cq.shape[0]
    n_tab = cos_t.shape[1] // tm
    tab = lambda i: (0, i % n_tab)
    half = ROPE_DIM // 2
    return pl.pallas_call(
        _mla_qt_kernel,
        grid=(t // tm,),
        in_specs=[pl.BlockSpec((tm, Q_LORA), lambda i: (i, 0)),
                  pl.BlockSpec(wt.shape, lambda i: (0, 0)),
                  pl.BlockSpec((half, tm), tab), pl.BlockSpec((half, tm), tab)],
        out_specs=pl.BlockSpec((HEADS * QK_PAD, tm), lambda i: (0, i)),
        out_shape=jax.ShapeDtypeStruct((HEADS * QK_PAD, t), BF16),
        compiler_params=_params("parallel"),
        name="mla_q_proj_t",
    )(cq, wt, cos_t, sin_t)


def _mla_kvt_kernel(ckv_ref, kr_ref, wk_ref, wvt_ref, k_ref, vt_ref):
    ckv = ckv_ref[...]
    acc = _dot(ckv, wk_ref[...])
    kr = kr_ref[...]
    zero = jnp.zeros((kr.shape[0], QK_PAD - LANES - ROPE_DIM), BF16)
    for h in range(HEADS):
        k_ref[h, :, :LANES] = acc[:, h * LANES:(h + 1) * LANES].astype(BF16)
        k_ref[h, :, LANES:LANES + ROPE_DIM] = kr
        k_ref[h, :, LANES + ROPE_DIM:] = zero
    vt_ref[...] = _dot_nt(wvt_ref[...], ckv).astype(BF16)


def _mla_kvt(ckv, kr, wk, wvt, b, tm):
    t = ckv.shape[0]
    per = t // b // tm
    row = lambda i: (i, 0)
    const = lambda i: (0, 0)
    return pl.pallas_call(
        _mla_kvt_kernel,
        grid=(t // tm,),
        in_specs=[pl.BlockSpec((tm, KV_LORA), row), pl.BlockSpec((tm, ROPE_DIM), row),
                  pl.BlockSpec(wk.shape, const), pl.BlockSpec(wvt.shape, const)],
        out_specs=[pl.BlockSpec((None, HEADS, tm, QK_PAD), lambda i: (i // per, 0, i % per, 0)),
                   pl.BlockSpec((None, SB_WIDTH, tm), lambda i: (i, 0, 0))],
        out_shape=[jax.ShapeDtypeStruct((b, HEADS, t // b, QK_PAD), BF16),
                   jax.ShapeDtypeStruct((t // tm, SB_WIDTH, tm), BF16)],
        compiler_params=_params("parallel"),
        name="mla_kv_proj_t",
    )(ckv, kr, wk, wvt)


def _gate_kernel(h_ref, w_ref, o_ref):
    o_ref[...] = jax.nn.sigmoid(_dot(h_ref[...], w_ref[...])).astype(BF16)


def _gates(h, w, tm, tn):
    t, d = h.shape
    n = w.shape[1]
    return pl.pallas_call(
        _gate_kernel,
        grid=(t // tm, n // tn),
        in_specs=[pl.BlockSpec((tm, d), lambda i, j: (i, 0)),
                  pl.BlockSpec((d, tn), lambda i, j: (0, j))],
        out_specs=pl.BlockSpec((tm, tn), lambda i, j: (i, j)),
        out_shape=jax.ShapeDtypeStruct((t, n), BF16),
        compiler_params=_params("parallel", "arbitrary"),
        name="branch_gates",
    )(h, w)


def _mix_kernel(osb_ref, omla_ref, g_ref, wb_ref, wo_ref, x_ref, gn_ref, o_ref):
    d = x_ref.shape[1]
    if len(omla_ref.shape) == 3:
        omla = jnp.concatenate([omla_ref[h] for h in range(HEADS)], axis=1)
    else:
        omla = omla_ref[...]
    merged = (g_ref[:, :d].astype(F32) * _dot(osb_ref[...], wb_ref[0])
              + g_ref[:, d:].astype(F32) * _dot(omla, wb_ref[1]))
    o_ref[...] = x_ref[...] + _rms(_dot(merged.astype(BF16), wo_ref[...]), gn_ref[...])


def _mix(osb, omla, gates, wb, wo, x, gn, tm):
    t, d = x.shape
    row = lambda i: (i, 0)
    once = pl.Buffered(1)
    if omla.ndim == 4:
        per = omla.shape[2] // tm
        omla_spec = pl.BlockSpec((None, HEADS, tm, HEAD_DIM), lambda i: (i // per, 0, i % per, 0))
    else:
        omla_spec = pl.BlockSpec((tm, SB_WIDTH), row)
    return pl.pallas_call(
        _mix_kernel,
        grid=(t // tm,),
        in_specs=[pl.BlockSpec((tm, SB_WIDTH), row), omla_spec,
                  pl.BlockSpec((tm, 2 * d), row),
                  pl.BlockSpec(wb.shape, lambda i: (0, 0, 0), pipeline_mode=once),
                  pl.BlockSpec(wo.shape, lambda i: (0, 0), pipeline_mode=once),
                  pl.BlockSpec((tm, d), row), pl.BlockSpec((1, d), lambda i: (0, 0))],
        out_specs=pl.BlockSpec((tm, d), row),
        out_shape=jax.ShapeDtypeStruct((t, d), F32),
        compiler_params=_params("parallel"),
        name="branch_mix_out",
    )(osb, omla, gates, wb, wo, x, gn)


def _ffn_kernel(x_ref, gpre_ref, wu_ref, wd_ref, gpost_ref, o_ref, h_scr, acc_scr):
    j = pl.program_id(1)

    @pl.when(j == 0)
    def _():
        h_scr[...] = _rms(x_ref[...], gpre_ref[...]).astype(BF16)
        acc_scr[...] = jnp.zeros_like(acc_scr)

    u = jnp.maximum(_dot(h_scr[...], wu_ref[...]), 0.0)
    acc_scr[...] += _dot((u * u).astype(BF16), wd_ref[...])

    @pl.when(j == pl.num_programs(1) - 1)
    def _():
        o_ref[...] = x_ref[...] + _rms(acc_scr[...], gpost_ref[...])


def _ffn(x, gpre, wu, wd, gpost, tm, tf):
    t, d = x.shape
    row = lambda i, j: (i, 0)
    const = lambda i, j: (0, 0)
    return pl.pallas_call(
        _ffn_kernel,
        grid=(t // tm, D_FF // tf),
        in_specs=[pl.BlockSpec((tm, d), row), pl.BlockSpec((1, d), const),
                  pl.BlockSpec((d, tf), lambda i, j: (0, j)),
                  pl.BlockSpec((tf, d), lambda i, j: (j, 0)),
                  pl.BlockSpec((1, d), const)],
        out_specs=pl.BlockSpec((tm, d), row),
        out_shape=jax.ShapeDtypeStruct((t, d), F32),
        scratch_shapes=[pltpu.VMEM((tm, d), BF16), pltpu.VMEM((tm, d), F32)],
        compiler_params=_params("parallel", "arbitrary", vmem=FFN_VMEM_LIMIT),
        name="sqrelu_ffn",
    )(x, gpre, wu, wd, gpost)


def _ple_kernel(x_ref, p_ref, gg_ref, wg_ref, wp_ref, gpost_ref, o_ref):
    x = x_ref[...]
    gate = jax.nn.sigmoid(_dot(_rms(x, gg_ref[...]).astype(BF16), wg_ref[...]))
    pe = _dot(p_ref[...].astype(BF16), wp_ref[...])
    o_ref[...] = x + _rms(pe * gate, gpost_ref[...])


def _ple(x, p, gg, wg, wp, gpost, tm):
    t, d = x.shape
    row = lambda i: (i, 0)
    const = lambda i: (0, 0)
    return pl.pallas_call(
        _ple_kernel,
        grid=(t // tm,),
        in_specs=[pl.BlockSpec((tm, d), row), pl.BlockSpec((tm, PLE_DIM), row),
                  pl.BlockSpec((1, d), const), pl.BlockSpec((d, d), const),
                  pl.BlockSpec((PLE_DIM, d), const), pl.BlockSpec((1, d), const)],
        out_specs=pl.BlockSpec((tm, d), row),
        out_shape=jax.ShapeDtypeStruct((t, d), F32),
        compiler_params=_params("parallel"),
        name="ple_embed",
    )(x, p, gg, wg, wp, gpost)


def _sb_step(qs, ks, vs, tri, carry, mask):
    rows = qs[0].shape[0]
    z = jnp.concatenate([_dot_nt(q, k) for q, k in zip(qs, ks)], axis=0)
    sp = jnp.maximum(z, 0.0) + jnp.log(1.0 + jnp.exp(-jnp.abs(z)))
    spm = sp if mask is None else jnp.where(mask, sp, 0.0)
    hi = spm.astype(BF16)
    lo = (spm - hi.astype(F32)).astype(BF16)
    cs = _dot(jnp.concatenate([hi, lo], axis=1), tri)
    after = cs[:, :KEY_BLOCK] if carry is None else cs[:, :KEY_BLOCK] + carry
    w = jnp.exp((z - sp) + after)
    if mask is not None:
        w = jnp.where(mask, w, 0.0)
    w = w.astype(BF16)
    pvs = [_dot(w[i * rows:(i + 1) * rows], v) for i, v in enumerate(vs)]
    carry = cs[:, KEY_BLOCK:] if carry is None else carry + cs[:, KEY_BLOCK:]
    return carry, pvs


def _sb_walk(first_blocks, later_blocks, tri_ref, carry_scr, acc_scr, o_ref, alongside=None):
    def run(chains, carry, mask, first):
        carry, pvs = _sb_step([c[2] for c in chains], [c[3] for c in chains], [c[4] for c in chains],
                              tri_ref[...], carry, mask)
        carry_scr[...] = carry
        for (rows, cols, _, _, _), pv in zip(chains, pvs):
            if first:
                acc_scr[rows, cols] = pv
            else:
                acc_scr[rows, cols] += pv
        return jnp.max(carry)

    def step0():
        chains, mask = first_blocks()
        return run(chains, None, mask, True)

    def step(j):
        chains, valid = later_blocks(j)
        rows = chains[0][2].shape[0]
        carry = jnp.concatenate(
            [jnp.where(ok, carry_scr[i * rows:(i + 1) * rows, :], MASKED) for i, ok in enumerate(valid)],
            axis=0)
        return run(chains, carry, None, False)

    if alongside is None:
        start = (jnp.int32(1), step0())
    else:
        finish = alongside()
        step0()
        step(1)
        start = (jnp.int32(3), step(2))
        finish()
    lax.while_loop(lambda st: st[1] > SB_EXIT, lambda st: (st[0] + 1, step(st[0])), start)
    o_ref[...] = acc_scr[...].astype(BF16)


def _head_cols(h):
    return slice(h * HEAD_DIM, (h + 1) * HEAD_DIM)


def _sb_prompt_kernel(q_ref, k_ref, v_ref, tri_ref, h_ref, wg_ref, o_ref, gate_ref, carry_scr, acc_scr,
                      *, rsub):
    bk = KEY_BLOCK
    base = pl.program_id(1) * rsub
    n_chain = rsub * HEADS
    r = lax.broadcasted_iota(jnp.int32, (n_chain * bk, bk), 0)
    c = lax.broadcasted_iota(jnp.int32, (n_chain * bk, bk), 1)
    causal = c < jnp.bitwise_and(r, bk - 1)

    def blocks(j):
        chains, valid = [], []
        for s in range(rsub):
            kb = base + s - j
            start = pl.multiple_of(jnp.maximum(kb, 0) * bk, bk)
            rows = slice(s * bk, (s + 1) * bk)
            for h in range(HEADS):
                cols = _head_cols(h)
                chains.append((rows, cols, q_ref[rows, cols],
                               k_ref[pl.ds(start, bk), cols], v_ref[pl.ds(start, bk), cols]))
                valid.append(kb >= 0)
        return chains, valid

    def gates():
        logits = _dot(h_ref[...], wg_ref[...])

        def finish():
            gate_ref[...] = jax.nn.sigmoid(logits).astype(BF16)
        return finish

    _sb_walk(lambda: (blocks(0)[0], causal), blocks, tri_ref, carry_scr, acc_scr, o_ref, alongside=gates)


GATE_TILE = 1024


def _sb_prompt_gates(qkv, tri, h, wg):
    _, b, n, _ = qkv.shape
    t, d = h.shape
    gw = wg.shape[1]
    bq = GATE_TILE * GATE_TILE // gw
    rsub = bq // KEY_BLOCK
    assert bq % KEY_BLOCK == 0 and n % bq == 0 and t % GATE_TILE == 0 and gw % GATE_TILE == 0
    nq = n // bq
    ncol = gw // GATE_TILE
    full = (None, None, n, SB_WIDTH)
    once = pl.Buffered(1)
    return pl.pallas_call(
        functools.partial(_sb_prompt_kernel, rsub=rsub),
        grid=(b, nq),
        in_specs=[pl.BlockSpec((None, None, bq, SB_WIDTH), lambda i, j: (0, i, j, 0)),
                  pl.BlockSpec(full, lambda i, j: (1, i, 0, 0), pipeline_mode=once),
                  pl.BlockSpec(full, lambda i, j: (2, i, 0, 0), pipeline_mode=once),
                  pl.BlockSpec(tri.shape, lambda i, j: (0, 0)),
                  pl.BlockSpec((GATE_TILE, d), lambda i, j: ((i * nq + j) // ncol, 0)),
                  pl.BlockSpec((d, GATE_TILE), lambda i, j: (0, (i * nq + j) % ncol))],
        out_specs=[pl.BlockSpec((None, bq, SB_WIDTH), lambda i, j: (i, j, 0)),
                   pl.BlockSpec((GATE_TILE, GATE_TILE), lambda i, j: ((i * nq + j) // ncol, (i * nq + j) % ncol))],
        out_shape=[jax.ShapeDtypeStruct((b, n, SB_WIDTH), BF16),
                   jax.ShapeDtypeStruct((t, gw), BF16)],
        scratch_shapes=[pltpu.VMEM((rsub * HEADS * KEY_BLOCK, KEY_BLOCK), F32),
                        pltpu.VMEM((bq, SB_WIDTH), F32)],
        compiler_params=_params("arbitrary", "arbitrary"),
        name="stickbreak_attn_gates",
    )(qkv, qkv, qkv, tri, h, wg)


def _sb_decode_kernel(q_ref, kn_ref, vn_ref, kp_ref, vp_ref, tri_ref, o_ref, carry_scr, acc_scr,
                      *, n, past):
    bk = KEY_BLOCK
    nb = past // bk
    assert n & (n - 1) == 0
    r = lax.broadcasted_iota(jnp.int32, (HEADS * n, bk), 0)
    c = lax.broadcasted_iota(jnp.int32, (HEADS * n, bk), 1)
    causal = c < jnp.bitwise_and(r, n - 1)
    zpad = jnp.zeros((bk - n, HEAD_DIM), BF16)
    rows = slice(0, n)

    def first():
        return [(rows, _head_cols(h), q_ref[:, _head_cols(h)],
                 jnp.concatenate([kn_ref[:, _head_cols(h)], zpad], axis=0),
                 jnp.concatenate([vn_ref[:, _head_cols(h)], zpad], axis=0))
                for h in range(HEADS)], causal

    def later(j):
        kb = nb - j
        start = jnp.maximum(kb, 0) * (bk * HEADS)
        chains = []
        for h in range(HEADS):
            sl = pl.ds(start + h, bk, stride=HEADS)
            chains.append((rows, _head_cols(h), q_ref[:, _head_cols(h)],
                           kp_ref[sl, :].astype(BF16), vp_ref[sl, :].astype(BF16)))
        return chains, [kb >= 0] * HEADS

    _sb_walk(first, later, tri_ref, carry_scr, acc_scr, o_ref)


def _sb_decode(qkv, k_past, v_past, tri):
    _, b, n, _ = qkv.shape
    past = k_past.shape[1]
    assert past % KEY_BLOCK == 0 and n <= KEY_BLOCK and n % 16 == 0
    new = (None, None, n, SB_WIDTH)
    old = (None, past * HEADS, HEAD_DIM)
    return pl.pallas_call(
        functools.partial(_sb_decode_kernel, n=n, past=past),
        grid=(b,),
        in_specs=[pl.BlockSpec(new, lambda i: (0, i, 0, 0)),
                  pl.BlockSpec(new, lambda i: (1, i, 0, 0)),
                  pl.BlockSpec(new, lambda i: (2, i, 0, 0)),
                  pl.BlockSpec(old, lambda i: (i, 0, 0)),
                  pl.BlockSpec(old, lambda i: (i, 0, 0)),
                  pl.BlockSpec(tri.shape, lambda i: (0, 0))],
        out_specs=pl.BlockSpec((None, n, SB_WIDTH), lambda i: (i, 0, 0)),
        out_shape=jax.ShapeDtypeStruct((b, n, SB_WIDTH), BF16),
        scratch_shapes=[pltpu.VMEM((HEADS * n, KEY_BLOCK), F32),
                        pltpu.VMEM((n, SB_WIDTH), F32)],
        compiler_params=_params("parallel"),
        name="stickbreak_attn_decode",
    )(qkv, qkv, qkv, k_past.reshape(b, past * HEADS, HEAD_DIM),
      v_past.reshape(b, past * HEADS, HEAD_DIM), tri)


CHUNK_SHIFT = CHUNK.bit_length() - 1
assert 1 << CHUNK_SHIFT == CHUNK


def _chunk(pos):
    return lax.shift_right_arithmetic(pos, CHUNK_SHIFT)


def _mla_prompt_kernel(qt_ref, k_ref, vt_ref, o_ref, s_scr, *, bk):
    bq = 2 * bk
    qi = pl.program_id(2)

    def scores(kb, slot):
        s0 = pl.multiple_of(kb * bk, bk)
        s_scr[slot] = _dot(k_ref[pl.ds(s0, bk), :], qt_ref[...])

    def absorb(st, s, kb):
        m, l, acc = st
        m_new = jnp.maximum(m, jnp.max(s, axis=0, keepdims=True))
        alpha = jnp.exp2(m - m_new)
        p = jnp.exp2(s - m_new)
        l = alpha * l + jnp.sum(p, axis=0, keepdims=True)
        acc = alpha * acc + _dot(vt_ref[kb], p.astype(BF16))
        return m_new, l, acc

    def pair(i, st):
        kb = 2 * i
        scores(kb + 1, 1)
        st = absorb(st, s_scr[0], kb)
        scores(kb + 2, 0)
        return absorb(st, s_scr[1], kb + 1)

    init = (jnp.full((1, bq), MASKED, F32), jnp.zeros((1, bq), F32), jnp.zeros((HEAD_DIM, bq), F32))
    scores(0, 0)
    st = lax.fori_loop(0, qi, pair, init)
    late = (slice(None), slice(bk, bq))
    s_late = _dot(k_ref[pl.ds(pl.multiple_of((2 * qi + 1) * bk, bk), bk), :], qt_ref[late])
    vis = (_chunk(lax.broadcasted_iota(jnp.int32, (bk, bq), 0))
           <= _chunk(lax.broadcasted_iota(jnp.int32, (bk, bq), 1)))
    m, l, acc = absorb(st, jnp.where(vis, s_scr[0], MASKED), 2 * qi)
    _, l2, acc2 = absorb((m[late], l[late], acc[late]), jnp.where(vis[:, :bk], s_late, MASKED), 2 * qi + 1)
    l = jnp.concatenate([l[:, :bk], l2], axis=1)
    acc = jnp.concatenate([acc[:, :bk], acc2], axis=1)
    o_ref[...] = (acc / l).T.astype(BF16)


def _mla_prompt(qt, k, vt):
    b, _, n, _ = k.shape
    bk = vt.shape[-1]
    bq = 2 * bk
    assert bk % CHUNK == 0 and n % bq == 0
    nq = n // bq
    return pl.pallas_call(
        functools.partial(_mla_prompt_kernel, bk=bk),
        grid=(b, HEADS, nq),
        in_specs=[pl.BlockSpec((QK_PAD, bq), lambda i, h, j: (h, i * nq + j)),
                  pl.BlockSpec((None, None, n, QK_PAD), lambda i, h, j: (i, h, 0, 0)),
                  pl.BlockSpec((None, n // bk, HEAD_DIM, bk), lambda i, h, j: (i, 0, h, 0))],
        out_specs=pl.BlockSpec((None, None, bq, HEAD_DIM), lambda i, h, j: (i, h, j, 0)),
        out_shape=jax.ShapeDtypeStruct((b, HEADS, n, HEAD_DIM), BF16),
        scratch_shapes=[pltpu.VMEM((2, bk, bq), F32)],
        compiler_params=_params("parallel", "parallel", "arbitrary"),
        name="mla_attn",
    )(qt, k, vt)


def _mla_decode_kernel(q_ref, ckv_old_ref, kr_old_ref, ckv_new_ref, kr_new_ref, wuk_ref, wuv_ref, o_ref,
                       *, n, past):
    rows = HEADS * n
    ck_old = ckv_old_ref[...].astype(BF16)
    kr_old = kr_old_ref[...].astype(BF16)
    ck_new = jnp.concatenate([ckv_new_ref[...], jnp.zeros((LANES - n, KV_LORA), BF16)], axis=0)
    kr_new = jnp.concatenate([kr_new_ref[...], jnp.zeros((LANES - n, ROPE_DIM), BF16)], axis=0)
    qa = jnp.concatenate(
        [_dot(q_ref[:, h * QK_PAD:h * QK_PAD + HEAD_DIM], wuk_ref[h]) for h in range(HEADS)],
        axis=0).astype(BF16)
    qr = jnp.concatenate(
        [q_ref[:, h * QK_PAD + HEAD_DIM:h * QK_PAD + HEAD_DIM + ROPE_DIM] for h in range(HEADS)], axis=0)
    s_old = _dot_nt(qa, ck_old) + _dot_nt(qr, kr_old)
    r = jnp.bitwise_and(lax.broadcasted_iota(jnp.int32, (rows, LANES), 0), n - 1)
    c = lax.broadcasted_iota(jnp.int32, (rows, LANES), 1)
    vis = jnp.logical_and(c < n, _chunk(past + c) <= _chunk(past + r))
    s_new = jnp.where(vis, _dot_nt(qa, ck_new) + _dot_nt(qr, kr_new), MASKED)
    m = jnp.maximum(jnp.max(s_old, axis=-1, keepdims=True), jnp.max(s_new, axis=-1, keepdims=True))
    p_old = jnp.exp(s_old - m)
    p_new = jnp.exp(s_new - m)
    l = jnp.sum(p_old, axis=-1, keepdims=True) + jnp.sum(p_new, axis=-1, keepdims=True)
    ctx = ((_dot(p_old.astype(BF16), ck_old) + _dot(p_new.astype(BF16), ck_new)) / l).astype(BF16)
    for h in range(HEADS):
        o_ref[:, _head_cols(h)] = _dot(ctx[h * n:(h + 1) * n], wuv_ref[:, _head_cols(h)]).astype(BF16)


def _mla_decode(q, ckv_old, kr_old, ckv_new, kr_new, wuk_t, wuv):
    b, n, _ = q.shape
    past = ckv_old.shape[1]
    assert n <= LANES and n % 16 == 0 and n & (n - 1) == 0
    spec = lambda rows, width: pl.BlockSpec((None, rows, width), lambda i: (i, 0, 0))
    return pl.pallas_call(
        functools.partial(_mla_decode_kernel, n=n, past=past),
        grid=(b,),
        in_specs=[spec(n, HEADS * QK_PAD), spec(past, KV_LORA), spec(past, ROPE_DIM),
                  spec(n, KV_LORA), spec(n, ROPE_DIM),
                  pl.BlockSpec(wuk_t.shape, lambda i: (0, 0, 0)), pl.BlockSpec(wuv.shape, lambda i: (0, 0))],
        out_specs=spec(n, SB_WIDTH),
        out_shape=jax.ShapeDtypeStruct((b, n, SB_WIDTH), BF16),
        compiler_params=_params("parallel"),
        name="mla_attn_decode",
    )(q, ckv_old, kr_old, ckv_new, kr_new, wuk_t, wuv)


def _rope_tables(pos):
    half = ROPE_DIM // 2
    freqs = ROPE_THETA ** (-jnp.arange(half, dtype=F32) / half)
    ang = pos.astype(F32)[:, None] * freqs[None, :]
    cos, sin = jnp.cos(ang), jnp.sin(ang)
    z = jnp.zeros((pos.shape[0], LANES - ROPE_DIM), F32)
    return jnp.concatenate([cos, cos, z], axis=1), jnp.concatenate([-sin, sin, z], axis=1)


def _swap_halves(w):
    half = w.shape[-1] // 2
    return jnp.concatenate([w[..., half:], w[..., :half]], axis=-1)


def _prep_weights(w_in, w_uq, w_uk, w_uv, w_branch, w_out, w_up, w_down, w_ple_gate, w_ple):
    d = w_in.shape[0]
    o = 3 * SB_WIDTH
    w_kr = w_in[:, o + Q_LORA + KV_LORA:o + Q_LORA + KV_LORA + ROPE_DIM]
    z64 = jnp.zeros((d, LANES - ROPE_DIM), w_in.dtype)
    w_lat = jnp.concatenate([w_in[:, o:o + Q_LORA + KV_LORA], w_kr, z64, _swap_halves(w_kr), z64], axis=1)
    wq3 = w_uq.reshape(Q_LORA, HEADS, HEAD_DIM + ROPE_DIM)
    rp = wq3[:, :, HEAD_DIM:]
    zq = jnp.zeros((Q_LORA, HEADS, LANES - ROPE_DIM), w_uq.dtype)
    w_q = jnp.concatenate([
        jnp.concatenate([wq3[:, :, :HEAD_DIM], rp, zq], axis=-1).reshape(Q_LORA, HEADS * QK_PAD),
        jnp.concatenate([_swap_halves(rp), zq], axis=-1).reshape(Q_LORA, HEADS * LANES)], axis=1)
    return dict(
        qkv=w_in.astype(BF16),
        lat=w_lat.astype(BF16),
        gate=w_in[:, o + Q_LORA + KV_LORA + ROPE_DIM:].astype(BF16),
        q=w_q.astype(BF16),
        qt=w_q[:, :HEADS * QK_PAD].T.astype(BF16),
        uk=w_uk.astype(BF16),
        uk_t=w_uk.reshape(KV_LORA, HEADS, HEAD_DIM).transpose(1, 2, 0).astype(BF16),
        uv=w_uv.astype(BF16),
        uvt=w_uv.T.astype(BF16),
        branch=w_branch.astype(BF16),
        out=w_out.astype(BF16),
        up=w_up.astype(BF16),
        down=w_down.astype(BF16),
        ple_gate=w_ple_gate.astype(BF16),
        ple=w_ple.astype(BF16),
    )


def _tile(t, want):
    return want if t % want == 0 else t


def _layer(x, ple, past, w, g, tri):
    b, n, d = x.shape
    t = b * n
    xf = x.reshape(t, d)
    tm = _tile(t, 1024)
    tm_s = _tile(t, 512)
    past_len = 0 if past is None else past[0].shape[1]
    if n >= tm:
        pos = jnp.arange(n) + past_len
    else:
        pos = jnp.tile(jnp.arange(n) + past_len, tm // n)
    cos2, sin2 = _rope_tables(pos)

    h, cq, ckv_f, ckv_b, kr_f, kr_b = _latent(xf, g['mix_pre'], w['lat'], g['q'], g['kv'], cos2, sin2, tm)
    qkv_b, k_f, v_f = _qkv(h, w['qkv'], tm)
    qkv_b = qkv_b.reshape(3, b, n, SB_WIDTH)

    if past is None:
        blk = _tile(n, 512)
        qt = _mla_qt(cq, w['qt'], cos2[:, :ROPE_DIM // 2].T, sin2[:, ROPE_DIM // 2:ROPE_DIM].T, blk)
        k_mla, vt = _mla_kvt(ckv_b, kr_b, w['uk'], w['uvt'], b, blk)
        o_sb, gates = _sb_prompt_gates(qkv_b, tri, h, w['gate'])
        o_mla = _mla_prompt(qt, k_mla, vt.reshape(b, n // blk, SB_WIDTH, blk))
    else:
        gates = _gates(h, w['gate'], tm, 1024)
        q_mla = _mla_q(cq, w['q'], cos2, sin2, tm_s).reshape(b, n, -1)
        o_sb = _sb_decode(qkv_b, past[0], past[1], tri)
        o_mla = _mla_decode(q_mla, past[2], past[3], ckv_b.reshape(b, n, -1), kr_b.reshape(b, n, -1),
                            w['uk_t'], w['uv'])

    if o_mla.ndim == 3:
        o_mla = o_mla.reshape(t, -1)
    x1 = _mix(o_sb.reshape(t, -1), o_mla, gates, w['branch'], w['out'], xf, g['mix_post'], tm_s)
    x2 = _ffn(x1, g['ffn_pre'], w['up'], w['down'], g['ffn_post'], tm_s, FFN_TILE)
    x3 = _ple(x2, ple.reshape(t, -1), g['ple_gate'], w['ple_gate'], w['ple'], g['ple_post'], tm_s)
    state = (k_f.reshape(1, b, n, HEADS, HEAD_DIM), v_f.reshape(1, b, n, HEADS, HEAD_DIM),
             ckv_f.reshape(1, b, n, KV_LORA), kr_f.reshape(1, b, n, ROPE_DIM))
    return x3.reshape(b, n, d), state


def kernel(x_prompt, x_sample, cache_sb_k, cache_sb_v, cache_mla_ckv, cache_mla_krope, p_prompt, p_sample, g_mix_pre, w_in, g_q, w_uq, g_kv, w_uk, w_uv, w_branch, w_out, g_mix_post, g_ffn_pre, w_up, w_down, g_ffn_post, g_ple_gate, w_ple_gate, w_ple, g_ple_post):
    assert w_in.shape[0] == 1, "single layer"
    w = _prep_weights(w_in[0], w_uq[0], w_uk[0], w_uv[0], w_branch[0], w_out[0], w_up[0], w_down[0],
                      w_ple_gate[0], w_ple[0])
    g = dict(mix_pre=g_mix_pre, q=g_q, kv=g_kv, mix_post=g_mix_post, ffn_pre=g_ffn_pre,
             ffn_post=g_ffn_post, ple_gate=g_ple_gate, ple_post=g_ple_post)
    idx = jnp.arange(KEY_BLOCK)
    lower = -(idx[:, None] > idx[None, :]).astype(BF16)
    tri = jnp.concatenate([lower, jnp.full((KEY_BLOCK, KEY_BLOCK), -1, BF16)], axis=1)
    tri = jnp.concatenate([tri, tri], axis=0)

    yp, sp = _layer(x_prompt, p_prompt[0], None, w, g, tri)
    past = (cache_sb_k[0], cache_sb_v[0], cache_mla_ckv[0], cache_mla_krope[0])
    ys, ss = _layer(x_sample, p_sample[0], past, w, g, tri)
    return (yp, ys) + sp + ss
```

```python
import functools

import jax
import jax.numpy as jnp
from jax import lax
from jax.experimental import pallas as pl
from jax.experimental.pallas import tpu as pltpu

F32 = jnp.float32
BF16 = jnp.bfloat16

D_MODEL = 2048
CHUNK = 64
PLE_DIM = 256
HEADS = 8
HEAD_DIM = 128
SB_WIDTH = HEADS * HEAD_DIM
ROPE_DIM = 64
Q_LORA = 512
KV_LORA = 512
QK_PAD = 256
D_FF = 4 * D_MODEL
ROPE_THETA = 10000.0
EPS = 1e-6
SB_SCALE = HEAD_DIM ** -0.5
MLA_SCALE = (HEAD_DIM + ROPE_DIM) ** -0.5
LOG2E = 1.4426950408889634

LANES = 128
KEY_BLOCK = 128
SB_EXIT = -104.0
MASKED = -1e30
VMEM_LIMIT = 56 * 1024 * 1024
FFN_VMEM_LIMIT = 62 * 1024 * 1024
FFN_TILE = 2048


def _params(*sem, vmem=VMEM_LIMIT):
    return pltpu.CompilerParams(dimension_semantics=sem, vmem_limit_bytes=vmem)


def _rms(xf, g):
    ms = jnp.mean(xf * xf, axis=-1, keepdims=True)
    return xf * lax.rsqrt(ms + EPS) * g


def _dot(a, b):
    return jnp.dot(a, b, preferred_element_type=F32)


def _dot_nt(a, b):
    return lax.dot_general(a, b, (((1,), (1,)), ((), ())), preferred_element_type=F32)


def _qkv_kernel(h_ref, w_ref, qkv_ref, kf_ref, vf_ref):
    j = pl.program_id(1)
    acc = _dot(h_ref[...], w_ref[...])
    qkv_ref[...] = (acc * jnp.where(j == 0, SB_SCALE, 1.0)).astype(BF16)

    def store_state(ref):
        for h in range(HEADS):
            ref[pl.ds(h, acc.shape[0], stride=HEADS), :] = acc[:, _head_cols(h)]

    pl.when(j == 1)(lambda: store_state(kf_ref))
    pl.when(j == 2)(lambda: store_state(vf_ref))


def _qkv(h, w, tm):
    t, d = h.shape
    state = pl.BlockSpec((tm * HEADS, HEAD_DIM), lambda i, j: (i, 0))
    return pl.pallas_call(
        _qkv_kernel,
        grid=(t // tm, 3),
        in_specs=[pl.BlockSpec((tm, d), lambda i, j: (i, 0)),
                  pl.BlockSpec((d, SB_WIDTH), lambda i, j: (0, j))],
        out_specs=[pl.BlockSpec((None, tm, SB_WIDTH), lambda i, j: (j, i, 0)), state, state],
        out_shape=[jax.ShapeDtypeStruct((3, t, SB_WIDTH), BF16),
                   jax.ShapeDtypeStruct((t * HEADS, HEAD_DIM), F32),
                   jax.ShapeDtypeStruct((t * HEADS, HEAD_DIM), F32)],
        compiler_params=_params("parallel", "arbitrary"),
        name="sb_qkv_proj",
    )(h, w)


def _latent_kernel(x_ref, g_ref, w_ref, gq_ref, gkv_ref, cos_ref, sin_ref,
                   h_ref, cq_ref, ckvf_ref, ckvb_ref, krf_ref, krb_ref):
    h = _rms(x_ref[...], g_ref[...]).astype(BF16)
    h_ref[...] = h
    acc = _dot(h, w_ref[...])
    cq_ref[...] = _rms(acc[:, :Q_LORA], gq_ref[...]).astype(BF16)
    ckv = _rms(acc[:, Q_LORA:Q_LORA + KV_LORA], gkv_ref[...])
    ckvf_ref[...] = ckv
    ckvb_ref[...] = ckv.astype(BF16)
    base = Q_LORA + KV_LORA
    kr = acc[:, base:base + LANES] * cos_ref[...] + acc[:, base + LANES:base + 2 * LANES] * sin_ref[...]
    krf_ref[...] = kr[:, :ROPE_DIM]
    krb_ref[...] = kr[:, :ROPE_DIM].astype(BF16)


def _latent(x, g, w, gq, gkv, cos2, sin2, tm):
    t, d = x.shape
    n_tab = cos2.shape[0] // tm
    row = lambda i: (i, 0)
    const = lambda i: (0, 0)
    tab = lambda i: (i % n_tab, 0)
    wn = w.shape[1]
    return pl.pallas_call(
        _latent_kernel,
        grid=(t // tm,),
        in_specs=[pl.BlockSpec((tm, d), row), pl.BlockSpec((1, d), const), pl.BlockSpec((d, wn), const),
                  pl.BlockSpec((1, Q_LORA), const), pl.BlockSpec((1, KV_LORA), const),
                  pl.BlockSpec((tm, LANES), tab), pl.BlockSpec((tm, LANES), tab)],
        out_specs=[pl.BlockSpec((tm, d), row),
                   pl.BlockSpec((tm, Q_LORA), row), pl.BlockSpec((tm, KV_LORA), row),
                   pl.BlockSpec((tm, KV_LORA), row), pl.BlockSpec((tm, ROPE_DIM), row),
                   pl.BlockSpec((tm, ROPE_DIM), row)],
        out_shape=[jax.ShapeDtypeStruct((t, d), BF16),
                   jax.ShapeDtypeStruct((t, Q_LORA), BF16),
                   jax.ShapeDtypeStruct((t, KV_LORA), F32),
                   jax.ShapeDtypeStruct((t, KV_LORA), BF16),
                   jax.ShapeDtypeStruct((t, ROPE_DIM), F32),
                   jax.ShapeDtypeStruct((t, ROPE_DIM), BF16)],
        compiler_params=_params("parallel"),
        name="norm_latent_proj",
    )(x, g, w, gq, gkv, cos2, sin2)


def _mla_q_kernel(cq_ref, w_ref, cos_ref, sin_ref, q_ref):
    acc = _dot(cq_ref[...], w_ref[...])
    cos, sin = cos_ref[...], sin_ref[...]
    for h in range(HEADS):
        b = h * QK_PAD
        q_ref[:, b:b + LANES] = (acc[:, b:b + LANES] * MLA_SCALE).astype(BF16)
        sw = HEADS * QK_PAD + h * LANES
        rot = acc[:, b + LANES:b + QK_PAD] * cos + acc[:, sw:sw + LANES] * sin
        q_ref[:, b + LANES:b + QK_PAD] = (rot * MLA_SCALE).astype(BF16)


def _mla_q(cq, w, cos2, sin2, tm):
    t = cq.shape[0]
    n_tab = cos2.shape[0] // tm
    row = lambda i: (i, 0)
    tab = lambda i: (i % n_tab, 0)
    return pl.pallas_call(
        _mla_q_kernel,
        grid=(t // tm,),
        in_specs=[pl.BlockSpec((tm, Q_LORA), row),
                  pl.BlockSpec(w.shape, lambda i: (0, 0)),
                  pl.BlockSpec((tm, LANES), tab), pl.BlockSpec((tm, LANES), tab)],
        out_specs=pl.BlockSpec((tm, HEADS * QK_PAD), row),
        out_shape=jax.ShapeDtypeStruct((t, HEADS * QK_PAD), BF16),
        compiler_params=_params("parallel"),
        name="mla_q_proj",
    )(cq, w, cos2, sin2)


def _mla_qt_kernel(cq_ref, wt_ref, cos_ref, sin_ref, qt_ref):
    acc = _dot_nt(wt_ref[...], cq_ref[...])
    cos, sin = cos_ref[...], sin_ref[...]
    half = ROPE_DIM // 2
    scale = MLA_SCALE * LOG2E
    for h in range(HEADS):
        b = h * QK_PAD
        r = b + HEAD_DIM
        qt_ref[b:r, :] = (acc[b:r] * scale).astype(BF16)
        x1, x2 = acc[r:r + half], acc[r + half:r + ROPE_DIM]
        qt_ref[r:r + half, :] = ((x1 * cos - x2 * sin) * scale).astype(BF16)
        qt_ref[r + half:r + ROPE_DIM, :] = ((x1 * sin + x2 * cos) * scale).astype(BF16)
        qt_ref[r + ROPE_DIM:b + QK_PAD, :] = jnp.zeros((QK_PAD - HEAD_DIM - ROPE_DIM, acc.shape[1]), BF16)


def _mla_qt(cq, wt, cos_t, sin_t, tm):
    t = cq.shape[0]
    n_tab = cos_t.shape[1] // tm
    tab = lambda i: (0, i % n_tab)
    half = ROPE_DIM // 2
    return pl.pallas_call(
        _mla_qt_kernel,
        grid=(t // tm,),
        in_specs=[pl.BlockSpec((tm, Q_LORA), lambda i: (i, 0)),
                  pl.BlockSpec(wt.shape, lambda i: (0, 0)),
                  pl.BlockSpec((half, tm), tab), pl.BlockSpec((half, tm), tab)],
        out_specs=pl.BlockSpec((HEADS * QK_PAD, tm), lambda i: (0, i)),
        out_shape=jax.ShapeDtypeStruct((HEADS * QK_PAD, t), BF16),
        compiler_params=_params("parallel"),
        name="mla_q_proj_t",
    )(cq, wt, cos_t, sin_t)


def _mla_kvt_kernel(ckv_ref, kr_ref, wk_ref, wvt_ref, k_ref, vt_ref):
    ckv = ckv_ref[...]
    acc = _dot(ckv, wk_ref[...])
    kr = kr_ref[...]
    zero = jnp.zeros((kr.shape[0], QK_PAD - LANES - ROPE_DIM), BF16)
    for h in range(HEADS):
        k_ref[h, :, :LANES] = acc[:, h * LANES:(h + 1) * LANES].astype(BF16)
        k_ref[h, :, LANES:LANES + ROPE_DIM] = kr
        k_ref[h, :, LANES + ROPE_DIM:] = zero
    vt_ref[...] = _dot_nt(wvt_ref[...], ckv).astype(BF16)


def _mla_kvt(ckv, kr, wk, wvt, b, tm):
    t = ckv.shape[0]
    per = t // b // tm
    row = lambda i: (i, 0)
    const = lambda i: (0, 0)
    return pl.pallas_call(
        _mla_kvt_kernel,
        grid=(t // tm,),
        in_specs=[pl.BlockSpec((tm, KV_LORA), row), pl.BlockSpec((tm, ROPE_DIM), row),
                  pl.BlockSpec(wk.shape, const), pl.BlockSpec(wvt.shape, const)],
        out_specs=[pl.BlockSpec((None, HEADS, tm, QK_PAD), lambda i: (i // per, 0, i % per, 0)),
                   pl.BlockSpec((None, SB_WIDTH, tm), lambda i: (i, 0, 0))],
        out_shape=[jax.ShapeDtypeStruct((b, HEADS, t // b, QK_PAD), BF16),
                   jax.ShapeDtypeStruct((t // tm, SB_WIDTH, tm), BF16)],
        compiler_params=_params("parallel"),
        name="mla_kv_proj_t",
    )(ckv, kr, wk, wvt)


def _gate_kernel(h_ref, w_ref, o_ref):
    o_ref[...] = jax.nn.sigmoid(_dot(h_ref[...], w_ref[...])).astype(BF16)


def _gates(h, w, tm, tn):
    t, d = h.shape
    n = w.shape[1]
    return pl.pallas_call(
        _gate_kernel,
        grid=(t // tm, n // tn),
        in_specs=[pl.BlockSpec((tm, d), lambda i, j: (i, 0)),
                  pl.BlockSpec((d, tn), lambda i, j: (0, j))],
        out_specs=pl.BlockSpec((tm, tn), lambda i, j: (i, j)),
        out_shape=jax.ShapeDtypeStruct((t, n), BF16),
        compiler_params=_params("parallel", "arbitrary"),
        name="branch_gates",
    )(h, w)


def _mix_kernel(osb_ref, omla_ref, g_ref, wb_ref, wo_ref, x_ref, gn_ref, o_ref):
    d = x_ref.shape[1]
    if len(omla_ref.shape) == 3:
        omla = jnp.concatenate([omla_ref[h] for h in range(HEADS)], axis=1)
    else:
        omla = omla_ref[...]
    merged = (g_ref[:, :d].astype(F32) * _dot(osb_ref[...], wb_ref[0])
              + g_ref[:, d:].astype(F32) * _dot(omla, wb_ref[1]))
    o_ref[...] = x_ref[...] + _rms(_dot(merged.astype(BF16), wo_ref[...]), gn_ref[...])


def _mix(osb, omla, gates, wb, wo, x, gn, tm):
    t, d = x.shape
    row = lambda i: (i, 0)
    once = pl.Buffered(1)
    if omla.ndim == 4:
        per = omla.shape[2] // tm
        omla_spec = pl.BlockSpec((None, HEADS, tm, HEAD_DIM), lambda i: (i // per, 0, i % per, 0))
    else:
        omla_spec = pl.BlockSpec((tm, SB_WIDTH), row)
    return pl.pallas_call(
        _mix_kernel,
        grid=(t // tm,),
        in_specs=[pl.BlockSpec((tm, SB_WIDTH), row), omla_spec,
                  pl.BlockSpec((tm, 2 * d), row),
                  pl.BlockSpec(wb.shape, lambda i: (0, 0, 0), pipeline_mode=once),
                  pl.BlockSpec(wo.shape, lambda i: (0, 0), pipeline_mode=once),
                  pl.BlockSpec((tm, d), row), pl.BlockSpec((1, d), lambda i: (0, 0))],
        out_specs=pl.BlockSpec((tm, d), row),
        out_shape=jax.ShapeDtypeStruct((t, d), F32),
        compiler_params=_params("parallel"),
        name="branch_mix_out",
    )(osb, omla, gates, wb, wo, x, gn)


def _ffn_kernel(x_ref, gpre_ref, wu_ref, wd_ref, gpost_ref, o_ref, h_scr, acc_scr):
    j = pl.program_id(1)

    @pl.when(j == 0)
    def _():
        h_scr[...] = _rms(x_ref[...], gpre_ref[...]).astype(BF16)
        acc_scr[...] = jnp.zeros_like(acc_scr)

    u = jnp.maximum(_dot(h_scr[...], wu_ref[...]), 0.0)
    acc_scr[...] += _dot((u * u).astype(BF16), wd_ref[...])

    @pl.when(j == pl.num_programs(1) - 1)
    def _():
        o_ref[...] = x_ref[...] + _rms(acc_scr[...], gpost_ref[...])


def _ffn(x, gpre, wu, wd, gpost, tm, tf):
    t, d = x.shape
    row = lambda i, j: (i, 0)
    const = lambda i, j: (0, 0)
    return pl.pallas_call(
        _ffn_kernel,
        grid=(t // tm, D_FF // tf),
        in_specs=[pl.BlockSpec((tm, d), row), pl.BlockSpec((1, d), const),
                  pl.BlockSpec((d, tf), lambda i, j: (0, j)),
                  pl.BlockSpec((tf, d), lambda i, j: (j, 0)),
                  pl.BlockSpec((1, d), const)],
        out_specs=pl.BlockSpec((tm, d), row),
        out_shape=jax.ShapeDtypeStruct((t, d), F32),
        scratch_shapes=[pltpu.VMEM((tm, d), BF16), pltpu.VMEM((tm, d), F32)],
        compiler_params=_params("parallel", "arbitrary", vmem=FFN_VMEM_LIMIT),
        name="sqrelu_ffn",
    )(x, gpre, wu, wd, gpost)


def _ple_kernel(x_ref, p_ref, gg_ref, wg_ref, wp_ref, gpost_ref, o_ref):
    x = x_ref[...]
    gate = jax.nn.sigmoid(_dot(_rms(x, gg_ref[...]).astype(BF16), wg_ref[...]))
    pe = _dot(p_ref[...].astype(BF16), wp_ref[...])
    o_ref[...] = x + _rms(pe * gate, gpost_ref[...])


def _ple(x, p, gg, wg, wp, gpost, tm):
    t, d = x.shape
    row = lambda i: (i, 0)
    const = lambda i: (0, 0)
    return pl.pallas_call(
        _ple_kernel,
        grid=(t // tm,),
        in_specs=[pl.BlockSpec((tm, d), row), pl.BlockSpec((tm, PLE_DIM), row),
                  pl.BlockSpec((1, d), const), pl.BlockSpec((d, d), const),
                  pl.BlockSpec((PLE_DIM, d), const), pl.BlockSpec((1, d), const)],
        out_specs=pl.BlockSpec((tm, d), row),
        out_shape=jax.ShapeDtypeStruct((t, d), F32),
        compiler_params=_params("parallel"),
        name="ple_embed",
    )(x, p, gg, wg, wp, gpost)


def _sb_step(qs, ks, vs, tri, carry, mask):
    rows = qs[0].shape[0]
    z = jnp.concatenate([_dot_nt(q, k) for q, k in zip(qs, ks)], axis=0)
    sp = jnp.maximum(z, 0.0) + jnp.log(1.0 + jnp.exp(-jnp.abs(z)))
    spm = sp if mask is None else jnp.where(mask, sp, 0.0)
    hi = spm.astype(BF16)
    lo = (spm - hi.astype(F32)).astype(BF16)
    cs = _dot(jnp.concatenate([hi, lo], axis=1), tri)
    after = cs[:, :KEY_BLOCK] if carry is None else cs[:, :KEY_BLOCK] + carry
    w = jnp.exp((z - sp) + after)
    if mask is not None:
        w = jnp.where(mask, w, 0.0)
    w = w.astype(BF16)
    pvs = [_dot(w[i * rows:(i + 1) * rows], v) for i, v in enumerate(vs)]
    carry = cs[:, KEY_BLOCK:] if carry is None else carry + cs[:, KEY_BLOCK:]
    return carry, pvs


def _sb_walk(first_blocks, later_blocks, tri_ref, carry_scr, acc_scr, o_ref, alongside=None):
    def run(chains, carry, mask, first):
        carry, pvs = _sb_step([c[2] for c in chains], [c[3] for c in chains], [c[4] for c in chains],
                              tri_ref[...], carry, mask)
        carry_scr[...] = carry
        for (rows, cols, _, _, _), pv in zip(chains, pvs):
            if first:
                acc_scr[rows, cols] = pv
            else:
                acc_scr[rows, cols] += pv
        return jnp.max(carry)

    def step0():
        chains, mask = first_blocks()
        return run(chains, None, mask, True)

    def step(j):
        chains, valid = later_blocks(j)
        rows = chains[0][2].shape[0]
        carry = jnp.concatenate(
            [jnp.where(ok, carry_scr[i * rows:(i + 1) * rows, :], MASKED) for i, ok in enumerate(valid)],
            axis=0)
        return run(chains, carry, None, False)

    if alongside is None:
        start = (jnp.int32(1), step0())
    else:
        finish = alongside()
        step0()
        step(1)
        start = (jnp.int32(3), step(2))
        finish()
    lax.while_loop(lambda st: st[1] > SB_EXIT, lambda st: (st[0] + 1, step(st[0])), start)
    o_ref[...] = acc_scr[...].astype(BF16)


def _head_cols(h):
    return slice(h * HEAD_DIM, (h + 1) * HEAD_DIM)


def _sb_prompt_kernel(q_ref, k_ref, v_ref, tri_ref, h_ref, wg_ref, o_ref, gate_ref, carry_scr, acc_scr,
                      *, rsub):
    bk = KEY_BLOCK
    base = pl.program_id(1) * rsub
    n_chain = rsub * HEADS
    r = lax.broadcasted_iota(jnp.int32, (n_chain * bk, bk), 0)
    c = lax.broadcasted_iota(jnp.int32, (n_chain * bk, bk), 1)
    causal = c < jnp.bitwise_and(r, bk - 1)

    def blocks(j):
        chains, valid = [], []
        for s in range(rsub):
            kb = base + s - j
            start = pl.multiple_of(jnp.maximum(kb, 0) * bk, bk)
            rows = slice(s * bk, (s + 1) * bk)
            for h in range(HEADS):
                cols = _head_cols(h)
                chains.append((rows, cols, q_ref[rows, cols],
                               k_ref[pl.ds(start, bk), cols], v_ref[pl.ds(start, bk), cols]))
                valid.append(kb >= 0)
        return chains, valid

    def gates():
        logits = _dot(h_ref[...], wg_ref[...])

        def finish():
            gate_ref[...] = jax.nn.sigmoid(logits).astype(BF16)
        return finish

    _sb_walk(lambda: (blocks(0)[0], causal), blocks, tri_ref, carry_scr, acc_scr, o_ref, alongside=gates)


GATE_TILE = 1024


def _sb_prompt_gates(qkv, tri, h, wg):
    _, b, n, _ = qkv.shape
    t, d = h.shape
    gw = wg.shape[1]
    bq = GATE_TILE * GATE_TILE // gw
    rsub = bq // KEY_BLOCK
    assert bq % KEY_BLOCK == 0 and n % bq == 0 and t % GATE_TILE == 0 and gw % GATE_TILE == 0
    nq = n // bq
    ncol = gw // GATE_TILE
    full = (None, None, n, SB_WIDTH)
    once = pl.Buffered(1)
    return pl.pallas_call(
        functools.partial(_sb_prompt_kernel, rsub=rsub),
        grid=(b, nq),
        in_specs=[pl.BlockSpec((None, None, bq, SB_WIDTH), lambda i, j: (0, i, j, 0)),
                  pl.BlockSpec(full, lambda i, j: (1, i, 0, 0), pipeline_mode=once),
                  pl.BlockSpec(full, lambda i, j: (2, i, 0, 0), pipeline_mode=once),
                  pl.BlockSpec(tri.shape, lambda i, j: (0, 0)),
                  pl.BlockSpec((GATE_TILE, d), lambda i, j: ((i * nq + j) // ncol, 0)),
                  pl.BlockSpec((d, GATE_TILE), lambda i, j: (0, (i * nq + j) % ncol))],
        out_specs=[pl.BlockSpec((None, bq, SB_WIDTH), lambda i, j: (i, j, 0)),
                   pl.BlockSpec((GATE_TILE, GATE_TILE), lambda i, j: ((i * nq + j) // ncol, (i * nq + j) % ncol))],
        out_shape=[jax.ShapeDtypeStruct((b, n, SB_WIDTH), BF16),
                   jax.ShapeDtypeStruct((t, gw), BF16)],
        scratch_shapes=[pltpu.VMEM((rsub * HEADS * KEY_BLOCK, KEY_BLOCK), F32),
                        pltpu.VMEM((bq, SB_WIDTH), F32)],
        compiler_params=_params("arbitrary", "arbitrary"),
        name="stickbreak_attn_gates",
    )(qkv, qkv, qkv, tri, h, wg)


def _sb_decode_kernel(q_ref, kn_ref, vn_ref, kp_ref, vp_ref, tri_ref, o_ref, carry_scr, acc_scr,
                      *, n, past):
    bk = KEY_BLOCK
    nb = past // bk
    assert n & (n - 1) == 0
    r = lax.broadcasted_iota(jnp.int32, (HEADS * n, bk), 0)
    c = lax.broadcasted_iota(jnp.int32, (HEADS * n, bk), 1)
    causal = c < jnp.bitwise_and(r, n - 1)
    zpad = jnp.zeros((bk - n, HEAD_DIM), BF16)
    rows = slice(0, n)

    def first():
        return [(rows, _head_cols(h), q_ref[:, _head_cols(h)],
                 jnp.concatenate([kn_ref[:, _head_cols(h)], zpad], axis=0),
                 jnp.concatenate([vn_ref[:, _head_cols(h)], zpad], axis=0))
                for h in range(HEADS)], causal

    def later(j):
        kb = nb - j
        start = jnp.maximum(kb, 0) * (bk * HEADS)
        chains = []
        for h in range(HEADS):
            sl = pl.ds(start + h, bk, stride=HEADS)
            chains.append((rows, _head_cols(h), q_ref[:, _head_cols(h)],
                           kp_ref[sl, :].astype(BF16), vp_ref[sl, :].astype(BF16)))
        return chains, [kb >= 0] * HEADS

    _sb_walk(first, later, tri_ref, carry_scr, acc_scr, o_ref)


def _sb_decode(qkv, k_past, v_past, tri):
    _, b, n, _ = qkv.shape
    past = k_past.shape[1]
    assert past % KEY_BLOCK == 0 and n <= KEY_BLOCK and n % 16 == 0
    new = (None, None, n, SB_WIDTH)
    old = (None, past * HEADS, HEAD_DIM)
    return pl.pallas_call(
        functools.partial(_sb_decode_kernel, n=n, past=past),
        grid=(b,),
        in_specs=[pl.BlockSpec(new, lambda i: (0, i, 0, 0)),
                  pl.BlockSpec(new, lambda i: (1, i, 0, 0)),
                  pl.BlockSpec(new, lambda i: (2, i, 0, 0)),
                  pl.BlockSpec(old, lambda i: (i, 0, 0)),
                  pl.BlockSpec(old, lambda i: (i, 0, 0)),
                  pl.BlockSpec(tri.shape, lambda i: (0, 0))],
        out_specs=pl.BlockSpec((None, n, SB_WIDTH), lambda i: (i, 0, 0)),
        out_shape=jax.ShapeDtypeStruct((b, n, SB_WIDTH), BF16),
        scratch_shapes=[pltpu.VMEM((HEADS * n, KEY_BLOCK), F32),
                        pltpu.VMEM((n, SB_WIDTH), F32)],
        compiler_params=_params("parallel"),
        name="stickbreak_attn_decode",
    )(qkv, qkv, qkv, k_past.reshape(b, past * HEADS, HEAD_DIM),
      v_past.reshape(b, past * HEADS, HEAD_DIM), tri)


CHUNK_SHIFT = CHUNK.bit_length() - 1
assert 1 << CHUNK_SHIFT == CHUNK


def _chunk(pos):
    return lax.shift_right_arithmetic(pos, CHUNK_SHIFT)


def _mla_prompt_kernel(qt_ref, k_ref, vt_ref, o_ref, s_scr, *, bk):
    bq = 2 * bk
    qi = pl.program_id(2)

    def scores(kb, slot):
        s0 = pl.multiple_of(kb * bk, bk)
        s_scr[slot] = _dot(k_ref[pl.ds(s0, bk), :], qt_ref[...])

    def absorb(st, s, kb):
        m, l, acc = st
        m_new = jnp.maximum(m, jnp.max(s, axis=0, keepdims=True))
        alpha = jnp.exp2(m - m_new)
        p = jnp.exp2(s - m_new)
        l = alpha * l + jnp.sum(p, axis=0, keepdims=True)
        acc = alpha * acc + _dot(vt_ref[kb], p.astype(BF16))
        return m_new, l, acc

    def pair(i, st):
        kb = 2 * i
        scores(kb + 1, 1)
        st = absorb(st, s_scr[0], kb)
        scores(kb + 2, 0)
        return absorb(st, s_scr[1], kb + 1)

    init = (jnp.full((1, bq), MASKED, F32), jnp.zeros((1, bq), F32), jnp.zeros((HEAD_DIM, bq), F32))
    scores(0, 0)
    st = lax.fori_loop(0, qi, pair, init)
    late = (slice(None), slice(bk, bq))
    s_late = _dot(k_ref[pl.ds(pl.multiple_of((2 * qi + 1) * bk, bk), bk), :], qt_ref[late])
    vis = (_chunk(lax.broadcasted_iota(jnp.int32, (bk, bq), 0))
           <= _chunk(lax.broadcasted_iota(jnp.int32, (bk, bq), 1)))
    m, l, acc = absorb(st, jnp.where(vis, s_scr[0], MASKED), 2 * qi)
    _, l2, acc2 = absorb((m[late], l[late], acc[late]), jnp.where(vis[:, :bk], s_late, MASKED), 2 * qi + 1)
    l = jnp.concatenate([l[:, :bk], l2], axis=1)
    acc = jnp.concatenate([acc[:, :bk], acc2], axis=1)
    o_ref[...] = (acc / l).T.astype(BF16)


def _mla_prompt(qt, k, vt):
    b, _, n, _ = k.shape
    bk = vt.shape[-1]
    bq = 2 * bk
    assert bk % CHUNK == 0 and n % bq == 0
    nq = n // bq
    return pl.pallas_call(
        functools.partial(_mla_prompt_kernel, bk=bk),
        grid=(b, HEADS, nq),
        in_specs=[pl.BlockSpec((QK_PAD, bq), lambda i, h, j: (h, i * nq + j)),
                  pl.BlockSpec((None, None, n, QK_PAD), lambda i, h, j: (i, h, 0, 0)),
                  pl.BlockSpec((None, n // bk, HEAD_DIM, bk), lambda i, h, j: (i, 0, h, 0))],
        out_specs=pl.BlockSpec((None, None, bq, HEAD_DIM), lambda i, h, j: (i, h, j, 0)),
        out_shape=jax.ShapeDtypeStruct((b, HEADS, n, HEAD_DIM), BF16),
        scratch_shapes=[pltpu.VMEM((2, bk, bq), F32)],
        compiler_params=_params("parallel", "parallel", "arbitrary"),
        name="mla_attn",
    )(qt, k, vt)


def _mla_decode_kernel(q_ref, ckv_old_ref, kr_old_ref, ckv_new_ref, kr_new_ref, wuk_ref, wuv_ref, o_ref,
                       *, n, past):
    rows = HEADS * n
    ck_old = ckv_old_ref[...].astype(BF16)
    kr_old = kr_old_ref[...].astype(BF16)
    ck_new = jnp.concatenate([ckv_new_ref[...], jnp.zeros((LANES - n, KV_LORA), BF16)], axis=0)
    kr_new = jnp.concatenate([kr_new_ref[...], jnp.zeros((LANES - n, ROPE_DIM), BF16)], axis=0)
    qa = jnp.concatenate(
        [_dot(q_ref[:, h * QK_PAD:h * QK_PAD + HEAD_DIM], wuk_ref[h]) for h in range(HEADS)],
        axis=0).astype(BF16)
    qr = jnp.concatenate(
        [q_ref[:, h * QK_PAD + HEAD_DIM:h * QK_PAD + HEAD_DIM + ROPE_DIM] for h in range(HEADS)], axis=0)
    s_old = _dot_nt(qa, ck_old) + _dot_nt(qr, kr_old)
    r = jnp.bitwise_and(lax.broadcasted_iota(jnp.int32, (rows, LANES), 0), n - 1)
    c = lax.broadcasted_iota(jnp.int32, (rows, LANES), 1)
    vis = jnp.logical_and(c < n, _chunk(past + c) <= _chunk(past + r))
    s_new = jnp.where(vis, _dot_nt(qa, ck_new) + _dot_nt(qr, kr_new), MASKED)
    m = jnp.maximum(jnp.max(s_old, axis=-1, keepdims=True), jnp.max(s_new, axis=-1, keepdims=True))
    p_old = jnp.exp(s_old - m)
    p_new = jnp.exp(s_new - m)
    l = jnp.sum(p_old, axis=-1, keepdims=True) + jnp.sum(p_new, axis=-1, keepdims=True)
    ctx = ((_dot(p_old.astype(BF16), ck_old) + _dot(p_new.astype(BF16), ck_new)) / l).astype(BF16)
    for h in range(HEADS):
        o_ref[:, _head_cols(h)] = _dot(ctx[h * n:(h + 1) * n], wuv_ref[:, _head_cols(h)]).astype(BF16)


def _mla_decode(q, ckv_old, kr_old, ckv_new, kr_new, wuk_t, wuv):
    b, n, _ = q.shape
    past = ckv_old.shape[1]
    assert n <= LANES and n % 16 == 0 and n & (n - 1) == 0
    spec = lambda rows, width: pl.BlockSpec((None, rows, width), lambda i: (i, 0, 0))
    return pl.pallas_call(
        functools.partial(_mla_decode_kernel, n=n, past=past),
        grid=(b,),
        in_specs=[spec(n, HEADS * QK_PAD), spec(past, KV_LORA), spec(past, ROPE_DIM),
                  spec(n, KV_LORA), spec(n, ROPE_DIM),
                  pl.BlockSpec(wuk_t.shape, lambda i: (0, 0, 0)), pl.BlockSpec(wuv.shape, lambda i: (0, 0))],
        out_specs=spec(n, SB_WIDTH),
        out_shape=jax.ShapeDtypeStruct((b, n, SB_WIDTH), BF16),
        compiler_params=_params("parallel"),
        name="mla_attn_decode",
    )(q, ckv_old, kr_old, ckv_new, kr_new, wuk_t, wuv)


def _rope_tables(pos):
    half = ROPE_DIM // 2
    freqs = ROPE_THETA ** (-jnp.arange(half, dtype=F32) / half)
    ang = pos.astype(F32)[:, None] * freqs[None, :]
    cos, sin = jnp.cos(ang), jnp.sin(ang)
    z = jnp.zeros((pos.shape[0], LANES - ROPE_DIM), F32)
    return jnp.concatenate([cos, cos, z], axis=1), jnp.concatenate([-sin, sin, z], axis=1)


def _swap_halves(w):
    half = w.shape[-1] // 2
    return jnp.concatenate([w[..., half:], w[..., :half]], axis=-1)


def _prep_weights(w_in, w_uq, w_uk, w_uv, w_branch, w_out, w_up, w_down, w_ple_gate, w_ple):
    d = w_in.shape[0]
    o = 3 * SB_WIDTH
    w_kr = w_in[:, o + Q_LORA + KV_LORA:o + Q_LORA + KV_LORA + ROPE_DIM]
    z64 = jnp.zeros((d, LANES - ROPE_DIM), w_in.dtype)
    w_lat = jnp.concatenate([w_in[:, o:o + Q_LORA + KV_LORA], w_kr, z64, _swap_halves(w_kr), z64], axis=1)
    wq3 = w_uq.reshape(Q_LORA, HEADS, HEAD_DIM + ROPE_DIM)
    rp = wq3[:, :, HEAD_DIM:]
    zq = jnp.zeros((Q_LORA, HEADS, LANES - ROPE_DIM), w_uq.dtype)
    w_q = jnp.concatenate([
        jnp.concatenate([wq3[:, :, :HEAD_DIM], rp, zq], axis=-1).reshape(Q_LORA, HEADS * QK_PAD),
        jnp.concatenate([_swap_halves(rp), zq], axis=-1).reshape(Q_LORA, HEADS * LANES)], axis=1)
    return dict(
        qkv=w_in.astype(BF16),
        lat=w_lat.astype(BF16),
        gate=w_in[:, o + Q_LORA + KV_LORA + ROPE_DIM:].astype(BF16),
        q=w_q.astype(BF16),
        qt=w_q[:, :HEADS * QK_PAD].T.astype(BF16),
        uk=w_uk.astype(BF16),
        uk_t=w_uk.reshape(KV_LORA, HEADS, HEAD_DIM).transpose(1, 2, 0).astype(BF16),
        uv=w_uv.astype(BF16),
        uvt=w_uv.T.astype(BF16),
        branch=w_branch.astype(BF16),
        out=w_out.astype(BF16),
        up=w_up.astype(BF16),
        down=w_down.astype(BF16),
        ple_gate=w_ple_gate.astype(BF16),
        ple=w_ple.astype(BF16),
    )


def _tile(t, want):
    return want if t % want == 0 else t


ROW_TILE = 1024
ROW_TILE_F32 = 512
MLA_KEY_BLOCK = 512


def _layer(x, ple, past, w, g, tri):
    b, n, d = x.shape
    t = b * n
    xf = x.reshape(t, d)
    tm = _tile(t, ROW_TILE)
    tm_s = _tile(t, ROW_TILE_F32)
    past_len = 0 if past is None else past[0].shape[1]
    if n >= tm:
        pos = jnp.arange(n) + past_len
    else:
        pos = jnp.tile(jnp.arange(n) + past_len, tm // n)
    cos2, sin2 = _rope_tables(pos)

    h, cq, ckv_f, ckv_b, kr_f, kr_b = _latent(xf, g['mix_pre'], w['lat'], g['q'], g['kv'], cos2, sin2, tm)
    qkv_b, k_f, v_f = _qkv(h, w['qkv'], tm)
    qkv_b = qkv_b.reshape(3, b, n, SB_WIDTH)

    if past is None:
        blk = _tile(n, MLA_KEY_BLOCK)
        qt = _mla_qt(cq, w['qt'], cos2[:, :ROPE_DIM // 2].T, sin2[:, ROPE_DIM // 2:ROPE_DIM].T, blk)
        k_mla, vt = _mla_kvt(ckv_b, kr_b, w['uk'], w['uvt'], b, blk)
        o_sb, gates = _sb_prompt_gates(qkv_b, tri, h, w['gate'])
        o_mla = _mla_prompt(qt, k_mla, vt.reshape(b, n // blk, SB_WIDTH, blk))
    else:
        gates = _gates(h, w['gate'], tm, GATE_TILE)
        q_mla = _mla_q(cq, w['q'], cos2, sin2, tm_s).reshape(b, n, -1)
        o_sb = _sb_decode(qkv_b, past[0], past[1], tri)
        o_mla = _mla_decode(q_mla, past[2], past[3], ckv_b.reshape(b, n, -1), kr_b.reshape(b, n, -1),
                            w['uk_t'], w['uv'])

    if o_mla.ndim == 3:
        o_mla = o_mla.reshape(t, -1)
    x1 = _mix(o_sb.reshape(t, -1), o_mla, gates, w['branch'], w['out'], xf, g['mix_post'], tm_s)
    x2 = _ffn(x1, g['ffn_pre'], w['up'], w['down'], g['ffn_post'], tm_s, FFN_TILE)
    x3 = _ple(x2, ple.reshape(t, -1), g['ple_gate'], w['ple_gate'], w['ple'], g['ple_post'], tm_s)
    state = (k_f.reshape(1, b, n, HEADS, HEAD_DIM), v_f.reshape(1, b, n, HEADS, HEAD_DIM),
             ckv_f.reshape(1, b, n, KV_LORA), kr_f.reshape(1, b, n, ROPE_DIM))
    return x3.reshape(b, n, d), state


def kernel(x_prompt, x_sample, cache_sb_k, cache_sb_v, cache_mla_ckv, cache_mla_krope, p_prompt, p_sample, g_mix_pre, w_in, g_q, w_uq, g_kv, w_uk, w_uv, w_branch, w_out, g_mix_post, g_ffn_pre, w_up, w_down, g_ffn_post, g_ple_gate, w_ple_gate, w_ple, g_ple_post):
    assert w_in.shape[0] == 1, "single layer"
    w = _prep_weights(w_in[0], w_uq[0], w_uk[0], w_uv[0], w_branch[0], w_out[0], w_up[0], w_down[0],
                      w_ple_gate[0], w_ple[0])
    g = dict(mix_pre=g_mix_pre, q=g_q, kv=g_kv, mix_post=g_mix_post, ffn_pre=g_ffn_pre,
             ffn_post=g_ffn_post, ple_gate=g_ple_gate, ple_post=g_ple_post)
    idx = jnp.arange(KEY_BLOCK)
    lower = -(idx[:, None] > idx[None, :]).astype(BF16)
    tri = jnp.concatenate([lower, jnp.full((KEY_BLOCK, KEY_BLOCK), -1, BF16)], axis=1)
    tri = jnp.concatenate([tri, tri], axis=0)

    yp, sp = _layer(x_prompt, p_prompt[0], None, w, g, tri)
    past = (cache_sb_k[0], cache_sb_v[0], cache_mla_ckv[0], cache_mla_krope[0])
    ys, ss = _layer(x_sample, p_sample[0], past, w, g, tri)
    return (yp, ys) + sp + ss
```
